```python
import math
import jax
import jax.numpy as jnp
from jax import lax
import numpy as np

D_MODEL = 1024
BATCH = 16
SEQ = 4096
DEPTH = 1

ATT_HEADS = 8
HEAD_DIM = 64
ATT_WIDTH = ATT_HEADS * HEAD_DIM
MOBA_BLOCK = 256
MOBA_TOPK = 3
QUERY_CHUNK = 16
NUM_BUCKETS = 32
MAX_DISTANCE = 128
SSM_WIDTH = 256
SSM_GROUP = 16
SSM_GROUPS = SSM_WIDTH // SSM_GROUP
SSM_STATE = 64
DT_MIN = 0.001
DT_MAX = 0.1
IN_SPLITS = (ATT_WIDTH, ATT_WIDTH, ATT_WIDTH, SSM_WIDTH, D_MODEL, D_MODEL)
IN_WIDTH = sum(IN_SPLITS)
N_GROUPS = 4
EXPERTS_PER_GROUP = 8
N_EXPERTS = N_GROUPS * EXPERTS_PER_GROUP
EXPERT_TOPK = 2
D_EXPERT = D_MODEL // 2
MOE_BLOCK = 256
ADA_CHUNKS = 6
RMS_EPS = 1e-6
NEG_INF = -1e30

kernel_name = "hybrid_moba_s5_hiermoe_block"


def rms_norm(x, gain):
    xf = x.astype(jnp.float32)
    y = xf * lax.rsqrt(jnp.mean(xf * xf, axis=-1, keepdims=True) + RMS_EPS)
    return (y * gain.astype(jnp.float32)).astype(x.dtype)


def t5_bucket(dist):
    n = jnp.maximum(dist, 0)
    max_exact = NUM_BUCKETS // 2
    nf = jnp.maximum(n, 1).astype(jnp.float32)
    large = max_exact + (jnp.log(nf / max_exact) / math.log(MAX_DISTANCE / max_exact)
                         * (NUM_BUCKETS - max_exact)).astype(jnp.int32)
    large = jnp.minimum(large, NUM_BUCKETS - 1)
    return jnp.where(n < max_exact, n, large)


def moba_attention(q, k, v, rel_bias):
    bsz, s, nh, dh = q.shape
    nb = -(-s // MOBA_BLOCK)
    s_pad = nb * MOBA_BLOCK
    n_chunks = s_pad // QUERY_CHUNK
    k_sel = min(MOBA_TOPK, nb)
    pad = ((0, 0), (0, s_pad - s), (0, 0), (0, 0))
    q, k, v = [jnp.pad(t, pad).transpose(0, 2, 1, 3) for t in (q, k, v)]
    kb = k.reshape(bsz, nh, nb, MOBA_BLOCK, dh)
    vb = v.reshape(bsz, nh, nb, MOBA_BLOCK, dh)
    k_mean = jnp.mean(kb.astype(jnp.float32), axis=3)
    gate = jnp.einsum("bhsd,bhnd->bhsn", q.astype(jnp.float32), k_mean)
    q_blk = jnp.arange(s_pad) // MOBA_BLOCK
    past = jnp.arange(nb)[None, :] < q_blk[:, None]
    gate = jnp.where(past, gate, NEG_INF)
    _, sel = lax.top_k(gate, k_sel)
    sel_valid = sel < q_blk[:, None]

    def to_chunks(t):
        t = t.reshape(bsz, nh, n_chunks, QUERY_CHUNK, *t.shape[3:])
        return jnp.moveaxis(t, 2, 0)

    head_bias = rel_bias.T.astype(jnp.float32)
    b_ix = jnp.arange(bsz)[:, None, None, None]
    h_ix = jnp.arange(nh)[None, :, None, None]
    h_ix5 = jnp.arange(nh)[None, :, None, None, None]
    offs = jnp.arange(MOBA_BLOCK)
    scale = HEAD_DIM ** -0.5

    def chunk_attend(args):
        qc, selc, validc, ci = args
        q_pos = ci * QUERY_CHUNK + jnp.arange(QUERY_CHUNK)
        own = (ci * QUERY_CHUNK) // MOBA_BLOCK
        kg = kb[b_ix, h_ix, selc]
        vg = vb[b_ix, h_ix, selc]
        key_pos = selc[..., None] * MOBA_BLOCK + offs
        bucket = t5_bucket(q_pos[None, None, :, None, None] - key_pos)
        bias_sel = head_bias[h_ix5, bucket]
        s_sel = jnp.einsum("bhqd,bhqnkd->bhqnk", qc, kg).astype(jnp.float32) * scale + bias_sel
        s_sel = jnp.where(validc[..., None], s_sel, NEG_INF).reshape(bsz, nh, QUERY_CHUNK, k_sel * MOBA_BLOCK)
        k_own = lax.dynamic_index_in_dim(kb, own, axis=2, keepdims=False)
        v_own = lax.dynamic_index_in_dim(vb, own, axis=2, keepdims=False)
        dist = q_pos[:, None] - (own * MOBA_BLOCK + offs)[None, :]
        bias_own = head_bias[:, t5_bucket(dist)]
        s_own = jnp.einsum("bhqd,bhkd->bhqk", qc, k_own).astype(jnp.float32) * scale + bias_own
        s_own = jnp.where(dist >= 0, s_own, NEG_INF)
        p = jax.nn.softmax(jnp.concatenate([s_sel, s_own], axis=-1), axis=-1)
        p_sel = p[..., :k_sel * MOBA_BLOCK].reshape(bsz, nh, QUERY_CHUNK, k_sel, MOBA_BLOCK).astype(vg.dtype)
        p_own = p[..., k_sel * MOBA_BLOCK:].astype(v_own.dtype)
        return (jnp.einsum("bhqnk,bhqnkd->bhqd", p_sel, vg)
                + jnp.einsum("bhqk,bhkd->bhqd", p_own, v_own))

    out = lax.map(chunk_attend, (to_chunks(q), to_chunks(sel), to_chunks(sel_valid),
                                 jnp.arange(n_chunks)))
    out = jnp.moveaxis(out, 0, 2).reshape(bsz, nh, s_pad, dh)[:, :, :s]
    return out.transpose(0, 2, 1, 3).reshape(bsz, s, nh * dh)


def s5_ssm(u, lam_re, lam_im, log_dt, b_re, b_im, c_re, c_im, d_skip):
    bsz, s, _ = u.shape
    uf = u.astype(jnp.float32).reshape(bsz, s, SSM_GROUPS, SSM_GROUP)
    dt = jnp.exp(log_dt.astype(jnp.float32))[:, None]
    lr = lam_re.astype(jnp.float32)
    li = lam_im.astype(jnp.float32)
    mag = jnp.exp(lr * dt)
    ab_re = mag * jnp.cos(li * dt)
    ab_im = mag * jnp.sin(li * dt)
    den = lr * lr + li * li
    nr = ab_re - 1.0
    ni = ab_im
    f_re = (nr * lr + ni * li) / den
    f_im = (ni * lr - nr * li) / den
    br = b_re.astype(jnp.float32)
    bi = b_im.astype(jnp.float32)
    bb_re = f_re[..., None] * br - f_im[..., None] * bi
    bb_im = f_re[..., None] * bi + f_im[..., None] * br
    bu_re = jnp.einsum("gpc,bsgc->bsgp", bb_re, uf)
    bu_im = jnp.einsum("gpc,bsgc->bsgp", bb_im, uf)
    a_re = jnp.broadcast_to(ab_re, (1, s, SSM_GROUPS, SSM_STATE))
    a_im = jnp.broadcast_to(ab_im, (1, s, SSM_GROUPS, SSM_STATE))

    def combine(left, right):
        a1r, a1i, b1r, b1i = left
        a2r, a2i, b2r, b2i = right
        return (a1r * a2r - a1i * a2i, a1r * a2i + a1i * a2r,
                a2r * b1r - a2i * b1i + b2r, a2r * b1i + a2i * b1r + b2i)

    _, _, xr, xi = lax.associative_scan(combine, (a_re, a_im, bu_re, bu_im), axis=1)
    y = (jnp.einsum("gcp,bsgp->bsgc", c_re.astype(jnp.float32), xr)
         - jnp.einsum("gcp,bsgp->bsgc", c_im.astype(jnp.float32), xi))
    y = y.reshape(bsz, s, SSM_WIDTH) + d_skip.astype(jnp.float32) * u.astype(jnp.float32)
    return y.astype(u.dtype)


def hier_moe(h, w_rg, b_rg, w_re, b_re, w_gate, w_up, w_down):
    bsz, s, d = h.shape
    n_tok = bsz * s
    xt = h.reshape(n_tok, d)
    xf = xt.astype(jnp.float32)
    g_prob = jax.nn.softmax(xf @ w_rg.astype(jnp.float32) + b_rg.astype(jnp.float32), axis=-1)
    g_p, g_idx = lax.top_k(g_prob, 1)
    e_logits = (xf @ w_re.astype(jnp.float32) + b_re.astype(jnp.float32)).reshape(n_tok, N_GROUPS, EXPERTS_PER_GROUP)
    e_logits = e_logits[jnp.arange(n_tok), g_idx[:, 0]]
    e_prob = jax.nn.softmax(e_logits, axis=-1)
    e_p, e_local = lax.top_k(e_prob, EXPERT_TOPK)
    weights = g_p * e_p / jnp.sum(e_p, axis=-1, keepdims=True)
    experts = g_idx * EXPERTS_PER_GROUP + e_local
    n_assign = n_tok * EXPERT_TOPK
    flat_e = experts.reshape(-1)
    flat_w = weights.reshape(-1)
    flat_tok = jnp.arange(n_assign, dtype=jnp.int32) // EXPERT_TOPK
    order = jnp.argsort(flat_e)
    e_sorted = flat_e[order]
    counts = jnp.bincount(flat_e, length=N_EXPERTS)
    start = jnp.cumsum(counts) - counts
    padded = (counts + MOE_BLOCK - 1) // MOE_BLOCK * MOE_BLOCK
    pend = jnp.cumsum(padded)
    pstart = pend - padded
    dest = pstart[e_sorted] + (jnp.arange(n_assign) - start[e_sorted])
    n_blocks = -(-n_assign // MOE_BLOCK) + N_EXPERTS
    n_rows = n_blocks * MOE_BLOCK
    row_tok = jnp.full((n_rows,), n_tok, jnp.int32).at[dest].set(flat_tok[order])
    row_w = jnp.zeros((n_rows,), jnp.float32).at[dest].set(flat_w[order])
    block_e = jnp.minimum(jnp.searchsorted(pend, jnp.arange(n_blocks) * MOE_BLOCK, side="right"), N_EXPERTS - 1)
    x_pad = jnp.concatenate([xt, jnp.zeros((1, d), xt.dtype)], axis=0)
    x_rows = x_pad[row_tok].reshape(n_blocks, MOE_BLOCK, d)

    def expert_block(args):
        xb, e = args
        hid = jax.nn.silu(xb @ w_gate[e]) * (xb @ w_up[e])
        return hid @ w_down[e]

    y_rows = lax.map(expert_block, (x_rows, block_e)).reshape(n_rows, d)
    y = jax.ops.segment_sum(y_rows.astype(jnp.float32) * row_w[:, None], row_tok,
                            num_segments=n_tok + 1)[:n_tok]
    return y.reshape(bsz, s, d).astype(h.dtype)


def setup_inputs(seed: int = 0) -> dict:
    key = jax.random.key(seed)
    ks = iter(jax.random.split(key, 40))
    L = DEPTH

    def nrm(shape, scale):
        return jax.random.normal(next(ks), shape, jnp.float32) * scale

    n_idx = jnp.arange(SSM_STATE, dtype=jnp.float32)
    return {
        "x": nrm((BATCH, SEQ, D_MODEL), 1.0),
        "c": nrm((BATCH, D_MODEL), 1.0),
        "rel_bias": nrm((NUM_BUCKETS, ATT_HEADS), 0.5),
        "w_ada": nrm((L, D_MODEL, ADA_CHUNKS * D_MODEL), 0.2 * D_MODEL ** -0.5),
        "b_ada": nrm((L, ADA_CHUNKS * D_MODEL), 0.02),
        "g_pre_mix": 1.0 + nrm((L, D_MODEL), 0.05),
        "g_post_mix": 1.0 + nrm((L, D_MODEL), 0.05),
        "w_in": nrm((L, D_MODEL, IN_WIDTH), D_MODEL ** -0.5),
        "w_att_out": nrm((L, ATT_WIDTH, D_MODEL), ATT_WIDTH ** -0.5),
        "ssm_lambda_re": -0.5 + nrm((L, SSM_GROUPS, SSM_STATE), 0.01),
        "ssm_lambda_im": math.pi * n_idx + nrm((L, SSM_GROUPS, SSM_STATE), 0.01),
        "ssm_log_dt": jax.random.uniform(next(ks), (L, SSM_GROUPS), jnp.float32,
                                         minval=math.log(DT_MIN), maxval=math.log(DT_MAX)),
        "ssm_b_re": nrm((L, SSM_GROUPS, SSM_STATE, SSM_GROUP), (2 * SSM_GROUP) ** -0.5),
        "ssm_b_im": nrm((L, SSM_GROUPS, SSM_STATE, SSM_GROUP), (2 * SSM_GROUP) ** -0.5),
        "ssm_c_re": nrm((L, SSM_GROUPS, SSM_GROUP, SSM_STATE), (2 * SSM_STATE) ** -0.5),
        "ssm_c_im": nrm((L, SSM_GROUPS, SSM_GROUP, SSM_STATE), (2 * SSM_STATE) ** -0.5),
        "ssm_d": nrm((L, SSM_WIDTH), 1.0),
        "w_glu_val": nrm((L, SSM_WIDTH, D_MODEL), SSM_WIDTH ** -0.5),
        "w_glu_gate": nrm((L, SSM_WIDTH, D_MODEL), SSM_WIDTH ** -0.5),
        "w_mix_out": nrm((L, D_MODEL, D_MODEL), D_MODEL ** -0.5),
        "g_pre_ffn": 1.0 + nrm((L, D_MODEL), 0.05),
        "g_post_ffn": 1.0 + nrm((L, D_MODEL), 0.05),
        "w_router_group": nrm((L, D_MODEL, N_GROUPS), D_MODEL ** -0.5),
        "b_router_group": nrm((L, N_GROUPS), 0.01),
        "w_router_expert": nrm((L, D_MODEL, N_EXPERTS), D_MODEL ** -0.5),
        "b_router_expert": nrm((L, N_EXPERTS), 0.01),
        "w_exp_gate": nrm((L, N_EXPERTS, D_MODEL, D_EXPERT), D_MODEL ** -0.5),
        "w_exp_up": nrm((L, N_EXPERTS, D_MODEL, D_EXPERT), D_MODEL ** -0.5),
        "w_exp_down": nrm((L, N_EXPERTS, D_EXPERT, D_MODEL), D_EXPERT ** -0.5),
    }


def reference(x, c, rel_bias, w_ada, b_ada, g_pre_mix, g_post_mix, w_in, w_att_out,
              ssm_lambda_re, ssm_lambda_im, ssm_log_dt, ssm_b_re, ssm_b_im, ssm_c_re, ssm_c_im,
              ssm_d, w_glu_val, w_glu_gate, w_mix_out, g_pre_ffn, g_post_ffn,
              w_router_group, b_router_group, w_router_expert, b_router_expert,
              w_exp_gate, w_exp_up, w_exp_down):
    bsz, s, _ = x.shape
    split_points = [int(p) for p in np.cumsum(IN_SPLITS)[:-1]]
    c_act = jax.nn.silu(c)
    for l in range(DEPTH):
        mod = c_act @ w_ada[l] + b_ada[l]
        shift1, scale1, gate1, shift2, scale2, gate2 = [m[:, None, :] for m in jnp.split(mod, ADA_CHUNKS, axis=-1)]
        h = rms_norm(x, g_pre_mix[l]) * (1.0 + scale1) + shift1
        proj = h @ w_in[l]
        q, k, v, u, g_att, g_ssm = jnp.split(proj, split_points, axis=-1)
        q = q.reshape(bsz, s, ATT_HEADS, HEAD_DIM)
        k = k.reshape(bsz, s, ATT_HEADS, HEAD_DIM)
        v = v.reshape(bsz, s, ATT_HEADS, HEAD_DIM)
        a_br = moba_attention(q, k, v, rel_bias) @ w_att_out[l]
        y_ssm = s5_ssm(u, ssm_lambda_re[l], ssm_lambda_im[l], ssm_log_dt[l], ssm_b_re[l], ssm_b_im[l],
                       ssm_c_re[l], ssm_c_im[l], ssm_d[l])
        z = jax.nn.gelu(y_ssm)
        s_br = (z @ w_glu_val[l]) * jax.nn.sigmoid(z @ w_glu_gate[l])
        merged = jax.nn.sigmoid(g_att) * a_br + jax.nn.sigmoid(g_ssm) * s_br
        x = x + gate1 * rms_norm(merged @ w_mix_out[l], g_post_mix[l])
        h = rms_norm(x, g_pre_ffn[l]) * (1.0 + scale2) + shift2
        f = hier_moe(h, w_router_group[l], b_router_group[l], w_router_expert[l], b_router_expert[l],
                     w_exp_gate[l], w_exp_up[l], w_exp_down[l])
        x = x + gate2 * rms_norm(f, g_post_ffn[l])
    return x
```

```python
import functools
import math

import numpy as np
import jax
import jax.numpy as jnp
from jax import lax
from jax.experimental import pallas as pl
from jax.experimental.pallas import tpu as pltpu

F32 = jnp.float32
BF16 = jnp.bfloat16
I32 = jnp.int32

ATT_HEADS = 8
HEAD_DIM = 64
MOBA_BLOCK = 256
MOBA_TOPK = 3
NUM_BUCKETS = 32
MAX_DISTANCE = 128
SSM_GROUP = 16
SSM_STATE = 64
SSM_CHUNK = 128
N_GROUPS = 4
EXPERTS_PER_GROUP = 8
N_EXPERTS = N_GROUPS * EXPERTS_PER_GROUP
MOE_BLOCK = 256
RMS_EPS = 1e-6
NEG_INF = -1e30
LANES = 128
ROUTER_ROWS = 40
VMEM_LIMIT = 56 * 1024 * 1024

_NT = (((1,), (1,)), ((), ()))


def _params(sem, vmem=VMEM_LIMIT):
    return pltpu.CompilerParams(dimension_semantics=sem, vmem_limit_bytes=vmem)


def _split_bf16(a):
    hi = a.astype(BF16)
    lo = (a - hi.astype(F32)).astype(BF16)
    return hi, lo


def _dot3(a, b, dims):
    a_hi, a_lo = _split_bf16(a)
    b_hi, b_lo = _split_bf16(b)
    dg = functools.partial(lax.dot_general, dimension_numbers=dims, preferred_element_type=F32)
    return dg(a_hi, b_hi) + (dg(a_hi, b_lo) + dg(a_lo, b_hi))


def _pack_pairs(a):
    w = a.shape[1] // 2
    bits = pltpu.bitcast(a.astype(BF16).astype(F32), jnp.uint32)
    return (bits[:, :w] >> 16) | (bits[:, w:] & jnp.uint32(0xFFFF0000))


def _unpack_pairs(words):
    lo = pltpu.bitcast(words << 16, F32)
    hi = pltpu.bitcast(words & jnp.uint32(0xFFFF0000), F32)
    return lo, hi


def _ada_kernel(c_ref, w_ref, b_ref, o_ref):
    c = c_ref[...]
    ca = c * jax.nn.sigmoid(c)
    o_ref[...] = _dot3(ca, w_ref[...], (((1,), (0,)), ((), ()))) + b_ref[...]


def _ada(c, w, b):
    bsz, d = c.shape
    n = w.shape[1]
    tn = 1536
    return pl.pallas_call(
        _ada_kernel,
        out_shape=jax.ShapeDtypeStruct((bsz, n), F32),
        grid=(n // tn,),
        in_specs=[pl.BlockSpec((bsz, d), lambda j: (0, 0)),
                  pl.BlockSpec((d, tn), lambda j: (0, j)),
                  pl.BlockSpec((1, tn), lambda j: (0, j))],
        out_specs=pl.BlockSpec((bsz, tn), lambda j: (0, j)),
        compiler_params=_params(("parallel",)),
        name="ada",
    )(c, w, b.reshape(1, n))


def _t5_bucket_np(dist):
    n = np.maximum(dist, 0)
    max_exact = NUM_BUCKETS // 2
    nf = np.maximum(n, 1).astype(np.float32)
    large = max_exact + (np.log(nf / np.float32(max_exact)) / np.float32(math.log(MAX_DISTANCE / max_exact))
                         * np.float32(NUM_BUCKETS - max_exact)).astype(np.int32)
    large = np.minimum(large, NUM_BUCKETS - 1)
    return np.where(n < max_exact, n, large).astype(np.int32)


def _bias_kernel(rb_ref, bk_ref, o_ref):
    h = pl.program_id(0)
    for t in range(2):
        bk = bk_ref[t]
        acc = jnp.where(bk < 0, NEG_INF, 0.0).astype(F32)
        for b in range(NUM_BUCKETS):
            acc = jnp.where(bk == b, rb_ref[b, h], acc)
        o_ref[0, t] = acc


def _bias_tiles(rel_bias):
    qi = np.arange(MOBA_BLOCK)[:, None]
    kj = np.arange(MOBA_BLOCK)[None, :]
    own = np.where(qi >= kj, _t5_bucket_np(qi - kj), -1)
    adj = _t5_bucket_np(qi - kj + MOBA_BLOCK)
    buckets = jnp.asarray(np.stack([own, adj]).astype(np.int32))
    return pl.pallas_call(
        _bias_kernel,
        out_shape=jax.ShapeDtypeStruct((ATT_HEADS, 2, MOBA_BLOCK, MOBA_BLOCK), F32),
        grid=(ATT_HEADS,),
        in_specs=[pl.BlockSpec(memory_space=pltpu.SMEM),
                  pl.BlockSpec((2, MOBA_BLOCK, MOBA_BLOCK), lambda h: (0, 0, 0))],
        out_specs=pl.BlockSpec((1, 2, MOBA_BLOCK, MOBA_BLOCK), lambda h: (h, 0, 0, 0)),
        compiler_params=_params(("parallel",)),
        name="t5_bias",
    )(rel_bias, buckets)


def _rms(x, gain):
    ms = jnp.mean(x * x, axis=-1, keepdims=True)
    return x * lax.rsqrt(ms + RMS_EPS) * gain


def _inproj_kernel(x_ref, mod_ref, g_ref, w_ref, wut_ref,
                   q_ref, k_ref, v_ref, ut_ref, sga_ref, sgs_ref):
    aw = q_ref.shape[2]
    d = x_ref.shape[2]
    x = x_ref[0]
    h = _rms(x, g_ref[...]) * (1.0 + mod_ref[0, 1:2, :]) + mod_ref[0, 0:1, :]
    hb = h.astype(BF16)

    def proj(lo, width):
        return jnp.dot(hb, w_ref[:, lo:lo + width], preferred_element_type=F32)

    q_ref[0] = (proj(0, aw) * (HEAD_DIM ** -0.5)).astype(BF16)
    k_ref[0] = proj(aw, aw).astype(BF16)
    v_ref[0] = proj(2 * aw, aw).astype(BF16)
    ut_ref[0] = lax.dot_general(wut_ref[...], hb, _NT, preferred_element_type=F32)
    sga_ref[0] = jax.nn.sigmoid(proj(3 * aw, d)).astype(BF16)
    sgs_ref[0] = jax.nn.sigmoid(proj(3 * aw + d, d)).astype(BF16)


def _inproj(x, mod, gain, w_rest, w_ut, tm):
    bsz, s, d = x.shape
    aw = ATT_HEADS * HEAD_DIM
    sw = w_ut.shape[0]
    tok = lambda width, dt: jax.ShapeDtypeStruct((bsz, s, width), dt)
    tspec = lambda width: pl.BlockSpec((1, tm, width), lambda b, i: (b, i, 0))
    return pl.pallas_call(
        _inproj_kernel,
        out_shape=(tok(aw, BF16), tok(aw, BF16), tok(aw, BF16),
                   jax.ShapeDtypeStruct((bsz, sw, s), F32), tok(d, BF16), tok(d, BF16)),
        grid=(bsz, s // tm),
        in_specs=[tspec(d),
                  pl.BlockSpec((1, mod.shape[1], d), lambda b, i: (b, 0, 0)),
                  pl.BlockSpec((1, d), lambda b, i: (0, 0)),
                  pl.BlockSpec(w_rest.shape, lambda b, i: (0, 0)),
                  pl.BlockSpec(w_ut.shape, lambda b, i: (0, 0))],
        out_specs=(tspec(aw), tspec(aw), tspec(aw),
                   pl.BlockSpec((1, sw, tm), lambda b, i: (b, 0, i)), tspec(d), tspec(d)),
        compiler_params=_params(("parallel", "parallel")),
        name="inproj",
    )(x, mod, gain, w_rest, w_ut)


def _moba_kernel(bfar_ref, q_ref, k_ref, v_ref, bias_ref, o_ref,
                 kmh_sc, kml_sc, qaug_sc, m_sc, l_sc, acc_sc):
    blk = MOBA_BLOCK
    p = pl.program_id(1)
    i = pl.program_id(2)
    s = k_ref.shape[1]
    nb = s // blk
    lane = lax.broadcasted_iota(I32, (blk, LANES), 1)
    low_half = lane < HEAD_DIM

    @pl.when(i == 0)
    def _():
        r = lax.broadcasted_iota(I32, (LANES, s), 0)
        c = lax.broadcasted_iota(I32, (LANES, s), 1)
        avg = jnp.where((c >= r * blk) & (c < (r + 1) * blk), 1.0 / blk, 0.0).astype(BF16)
        km = jnp.dot(avg, k_ref[0], preferred_element_type=F32)
        hi, lo = _split_bf16(km)
        kmh_sc[...] = hi
        kml_sc[...] = lo

    q2 = q_ref[0]
    zero = jnp.zeros_like(q2)
    for hh in range(2):
        qh = jnp.where(low_half if hh == 0 else jnp.logical_not(low_half), q2, zero)
        gate = (lax.dot_general(qh, kmh_sc[...], _NT, preferred_element_type=F32)
                + lax.dot_general(qh, kml_sc[...], _NT, preferred_element_type=F32))
        gate = jnp.where(lane < i, gate, NEG_INF)
        cnt = jnp.zeros((blk, LANES), F32)
        for m in range(nb):
            col = jnp.broadcast_to(gate[:, m:m + 1], (blk, LANES))
            tie = jnp.where(lane > m, 1.0, 0.0)
            cnt = cnt + jnp.where(col > gate, 1.0, 0.0) + jnp.where(col == gate, tie, 0.0)
        chosen = jnp.where(lane < i, cnt, float(MOBA_TOPK)) < float(MOBA_TOPK)
        keep = jnp.where(lane == i, 0.0, jnp.where(chosen, 0.0, NEG_INF))
        qaug_sc[hh] = jnp.concatenate([qh, keep.astype(BF16)], axis=1)

    def tile(j, bias_of_head, first):
        start = pl.multiple_of(j * blk, blk)
        kj = k_ref[0, pl.ds(start, blk), :]
        vj = v_ref[0, pl.ds(start, blk), :]
        onehot = jnp.where(lane == j, 1.0, 0.0).astype(BF16)
        kaug = jnp.concatenate([kj, onehot], axis=1)
        pv = []
        alpha = []
        for hh in range(2):
            sc = lax.dot_general(qaug_sc[hh], kaug, _NT, preferred_element_type=F32)
            sc = sc + bias_of_head(hh)
            m_cur = jnp.max(sc, axis=1, keepdims=True)
            if first:
                m_new = jnp.broadcast_to(m_cur, (blk, LANES))
            else:
                m_prev = m_sc[hh]
                m_new = jnp.maximum(m_prev, m_cur)
            pr = jnp.exp(sc - jnp.concatenate([m_new, m_new], axis=1))
            l_cur = jnp.sum(pr, axis=1, keepdims=True)
            if first:
                l_sc[hh] = jnp.broadcast_to(l_cur, (blk, LANES))
            else:
                a = jnp.exp(m_prev - m_new)
                alpha.append(a)
                l_sc[hh] = a * l_sc[hh] + l_cur
            m_sc[hh] = m_new
            pv.append(jnp.dot(pr.astype(BF16), vj, preferred_element_type=F32))
        upd = jnp.where(low_half, pv[0], pv[1])
        if first:
            acc_sc[...] = upd
        else:
            acc_sc[...] = jnp.where(low_half, alpha[0], alpha[1]) * acc_sc[...] + upd

    tile(i, lambda hh: bias_ref[hh, 0], True)

    @pl.when(i >= 1)
    def _():
        tile(i - 1, lambda hh: bias_ref[hh, 1], False)

    def far(j, carry):
        tile(j, lambda hh: bfar_ref[2 * p + hh], False)
        return carry

    lax.fori_loop(0, jnp.maximum(i - 1, 0), far, 0)

    inv = jnp.where(low_half, 1.0 / l_sc[0], 1.0 / l_sc[1])
    o_ref[0] = (acc_sc[...] * inv).astype(BF16)


def _moba(q, k, v, bias, bfar):
    bsz, s, aw = q.shape
    blk = MOBA_BLOCK
    pairs = aw // LANES
    grid_spec = pltpu.PrefetchScalarGridSpec(
        num_scalar_prefetch=1,
        grid=(bsz, pairs, s // blk),
        in_specs=[pl.BlockSpec((1, blk, LANES), lambda b, p, i, _: (b, i, p)),
                  pl.BlockSpec((1, s, LANES), lambda b, p, i, _: (b, 0, p)),
                  pl.BlockSpec((1, s, LANES), lambda b, p, i, _: (b, 0, p)),
                  pl.BlockSpec((2, 2, blk, blk), lambda b, p, i, _: (p, 0, 0, 0))],
        out_specs=pl.BlockSpec((1, blk, LANES), lambda b, p, i, _: (b, i, p)),
        scratch_shapes=[pltpu.VMEM((LANES, LANES), BF16), pltpu.VMEM((LANES, LANES), BF16),
                        pltpu.VMEM((2, blk, 2 * LANES), BF16),
                        pltpu.VMEM((2, blk, LANES), F32), pltpu.VMEM((2, blk, LANES), F32),
                        pltpu.VMEM((blk, LANES), F32)],
    )
    return pl.pallas_call(
        _moba_kernel,
        out_shape=jax.ShapeDtypeStruct((bsz, s, aw), BF16),
        grid_spec=grid_spec,
        compiler_params=_params(("parallel", "parallel", "arbitrary")),
        name="moba",
    )(bfar, q, k, v, bias)


def _ssm_tables(lam_re, lam_im, log_dt, b_re, b_im, c_re, c_im, d_skip):
    L = SSM_CHUNK
    g = lam_re.shape[0]
    dt = jnp.exp(log_dt)[:, None]
    lr, li = lam_re, lam_im
    mag = jnp.exp(lr * dt)
    ab_re, ab_im = mag * jnp.cos(li * dt), mag * jnp.sin(li * dt)
    den = lr * lr + li * li
    nr, ni = ab_re - 1.0, ab_im
    f_re, f_im = (nr * lr + ni * li) / den, (ni * lr - nr * li) / den
    bb_re = f_re[..., None] * b_re - f_im[..., None] * b_im
    bb_im = f_re[..., None] * b_im + f_im[..., None] * b_re
    n = jnp.arange(L + 1, dtype=F32)[:, None, None]
    pw_mag = jnp.exp(n * (lr * dt)[None])
    pw_re, pw_im = pw_mag * jnp.cos(n * (li * dt)[None]), pw_mag * jnp.sin(n * (li * dt)[None])
    ca_re = c_re[None] * pw_re[:L, :, None, :] - c_im[None] * pw_im[:L, :, None, :]
    ca_im = c_re[None] * pw_im[:L, :, None, :] + c_im[None] * pw_re[:L, :, None, :]
    hp = lax.Precision.HIGHEST
    kern = (jnp.einsum("ngcp,gpd->ngcd", ca_re, bb_re, precision=hp)
            - jnp.einsum("ngcp,gpd->ngcd", ca_im, bb_im, precision=hp))
    s_ix = np.arange(L)[:, None]
    t_ix = np.arange(L)[None, :]
    lag = np.where(t_ix >= s_ix, t_ix - s_ix, L)
    kern_pad = jnp.concatenate([kern, jnp.zeros((1,) + kern.shape[1:], F32)], axis=0)
    toep = kern_pad[lag]
    toep = toep.transpose(2, 4, 0, 3, 1)
    eye = jnp.asarray(np.eye(SSM_GROUP, dtype=np.float32)[:, None, :, None]
                      * np.eye(L, dtype=np.float32)[None, :, None, :])
    toep = toep + d_skip.reshape(g, 1, 1, SSM_GROUP, 1) * eye[None]
    toep = toep.reshape(g, SSM_GROUP * L, SSM_GROUP * L)
    rev_re, rev_im = pw_re[L - 1::-1][:L], pw_im[L - 1::-1][:L]
    win_re = rev_re[..., None] * bb_re[None] - rev_im[..., None] * bb_im[None]
    win_im = rev_re[..., None] * bb_im[None] + rev_im[..., None] * bb_re[None]
    w_in = jnp.concatenate([win_re, win_im], axis=2)
    w_in = w_in.transpose(1, 3, 0, 2).reshape(g, SSM_GROUP * L, 2 * SSM_STATE)
    fw_re, fw_im = pw_re[1:], pw_im[1:]
    wo_re = c_re[None] * fw_re[:, :, None, :] - c_im[None] * fw_im[:, :, None, :]
    wo_im = -(c_re[None] * fw_im[:, :, None, :] + c_im[None] * fw_re[:, :, None, :])
    w_out = jnp.concatenate([wo_re, wo_im], axis=3)
    w_out = w_out.transpose(1, 3, 2, 0).reshape(g, 2 * SSM_STATE, SSM_GROUP * L)
    a_chunk = jnp.stack([jnp.concatenate([pw_re[L], pw_re[L]], axis=-1),
                         jnp.concatenate([-pw_im[L], pw_im[L]], axis=-1)], axis=1)
    return toep.astype(BF16), w_in.astype(BF16), w_out.astype(BF16), a_chunk


def _ssm_kernel(u_ref, toep_ref, win_ref, wout_ref, ac_ref, y_ref, st_sc):
    bsz, ng, nc, L = u_ref.shape
    rows = bsz * nc
    u = jnp.concatenate([u_ref[:, c].reshape(rows, L) for c in range(ng)], axis=1).astype(BF16)
    st_sc[...] = jnp.dot(u, win_ref[0], preferred_element_type=F32).reshape(bsz, nc, 2 * SSM_STATE)
    a1 = ac_ref[0, 0:1, :]
    a2 = ac_ref[0, 1:2, :]
    state = jnp.zeros((bsz, 2 * SSM_STATE), F32)
    for ch in range(nc):
        inc = st_sc[:, ch, :]
        st_sc[:, ch, :] = state
        state = a1 * state + a2 * pltpu.roll(state, SSM_STATE, 1) + inc
    prev = st_sc[...].reshape(rows, 2 * SSM_STATE).astype(BF16)
    y = (jnp.dot(u, toep_ref[0], preferred_element_type=F32)
         + jnp.dot(prev, wout_ref[0], preferred_element_type=F32))
    for c in range(ng):
        y_ref[:, c] = y[:, c * L:(c + 1) * L].reshape(bsz, nc, L)


def _ssm(u4, toep, w_in, w_out, a_chunk):
    bsz, sw, nc, L = u4.shape
    g = sw // SSM_GROUP
    blk = pl.BlockSpec((bsz, SSM_GROUP, nc, L), lambda j: (0, j, 0, 0))
    per_group = lambda a: pl.BlockSpec((1,) + a.shape[1:], lambda j: (j, 0, 0))
    return pl.pallas_call(
        _ssm_kernel,
        out_shape=jax.ShapeDtypeStruct(u4.shape, F32),
        grid=(g,),
        in_specs=[blk, per_group(toep), per_group(w_in), per_group(w_out), per_group(a_chunk)],
        out_specs=blk,
        scratch_shapes=[pltpu.VMEM((bsz, nc, 2 * SSM_STATE), F32)],
        compiler_params=_params(("parallel",)),
        name="ssm",
    )(u4, toep, w_in, w_out, a_chunk)


def _gelu_tanh(x):
    return 0.5 * x * (1.0 + jnp.tanh(math.sqrt(2.0 / math.pi) * (x + 0.044715 * (x * x * x))))


def _merge_kernel(x_ref, att_ref, yt_ref, sga_ref, sgs_ref, mod_ref, gpost_ref, gpre_ref,
                  wao_ref, wgv_ref, wgg_ref, wmo_ref, wrt_ref, brt_ref,
                  x1_ref, h2_ref, lt_ref):
    a_br = jnp.dot(att_ref[0], wao_ref[...], preferred_element_type=F32)
    z = _gelu_tanh(yt_ref[0]).T.astype(BF16)
    s_br = (jnp.dot(z, wgv_ref[...], preferred_element_type=F32)
            * jax.nn.sigmoid(jnp.dot(z, wgg_ref[...], preferred_element_type=F32)))
    merged = sga_ref[0].astype(F32) * a_br + sgs_ref[0].astype(F32) * s_br
    mix = jnp.dot(merged.astype(BF16), wmo_ref[...], preferred_element_type=F32)
    x1 = x_ref[0] + mod_ref[0, 2:3, :] * _rms(mix, gpost_ref[...])
    x1_ref[0] = x1
    h2 = _rms(x1, gpre_ref[...]) * (1.0 + mod_ref[0, 4:5, :]) + mod_ref[0, 3:4, :]
    h2_ref[0] = _pack_pairs(h2)
    lt_ref[...] = _dot3(wrt_ref[...], h2, _NT) + brt_ref[...]


def _merge(x, att, yt, sga, sgs, mod, g_post, g_pre, w_ao, w_gv, w_gg, w_mo, w_rt, b_rt, tm):
    bsz, s, d = x.shape
    aw = att.shape[2]
    sw = yt.shape[1]
    nt = s // tm
    tspec = lambda width: pl.BlockSpec((1, tm, width), lambda b, i: (b, i, 0))
    full = lambda a: pl.BlockSpec(a.shape, lambda b, i: (0,) * a.ndim)
    return pl.pallas_call(
        _merge_kernel,
        out_shape=(jax.ShapeDtypeStruct((bsz, s, d), F32), jax.ShapeDtypeStruct((bsz, s, d // 2), jnp.uint32),
                   jax.ShapeDtypeStruct((ROUTER_ROWS, bsz * s), F32)),
        grid=(bsz, nt),
        in_specs=[tspec(d), tspec(aw), pl.BlockSpec((1, sw, tm), lambda b, i: (b, 0, i)),
                  tspec(d), tspec(d),
                  pl.BlockSpec((1, mod.shape[1], d), lambda b, i: (b, 0, 0)),
                  full(g_post), full(g_pre), full(w_ao), full(w_gv), full(w_gg), full(w_mo),
                  full(w_rt), full(b_rt)],
        out_specs=(tspec(d), tspec(d // 2), pl.BlockSpec((ROUTER_ROWS, tm), lambda b, i: (0, b * nt + i))),
        compiler_params=_params(("parallel", "parallel")),
        name="merge",
    )(x, att, yt, sga, sgs, mod, g_post, g_pre, w_ao, w_gv, w_gg, w_mo, w_rt, b_rt)


def _route_kernel(lt_ref, eid_ref, wt_ref, rank_ref, cnt_ref, base_sc):
    tn = lt_ref.shape[1]
    epg = EXPERTS_PER_GROUP

    @pl.when(pl.program_id(0) == 0)
    def _():
        base_sc[...] = jnp.zeros_like(base_sc)

    row8 = lax.broadcasted_iota(I32, (epg, tn), 0)
    gl = lt_ref[N_EXPERTS:N_EXPERTS + epg, :]
    gl = jnp.where(row8 < N_GROUPS, gl, -jnp.inf)
    gmax = jnp.max(gl, axis=0, keepdims=True)
    gidx = jnp.min(jnp.where(gl == gmax, row8, epg), axis=0, keepdims=True)
    g_p = 1.0 / jnp.sum(jnp.exp(gl - gmax), axis=0, keepdims=True)

    el = jnp.zeros((epg, tn), F32)
    for g in range(N_GROUPS):
        el = jnp.where(gidx == g, lt_ref[g * epg:(g + 1) * epg, :], el)
    m1 = jnp.max(el, axis=0, keepdims=True)
    i1 = jnp.min(jnp.where(el == m1, row8, epg), axis=0, keepdims=True)
    el2 = jnp.where(row8 == i1, -jnp.inf, el)
    m2 = jnp.max(el2, axis=0, keepdims=True)
    i2 = jnp.min(jnp.where(el2 == m2, row8, epg), axis=0, keepdims=True)
    zsum = jnp.sum(jnp.exp(el - m1), axis=0, keepdims=True)
    p1 = 1.0 / zsum
    p2 = jnp.exp(m2 - m1) / zsum
    w1 = g_p * p1 / (p1 + p2)
    w2 = g_p * p2 / (p1 + p2)
    e1 = gidx * epg + i1
    e2 = gidx * epg + i2

    row32 = lax.broadcasted_iota(I32, (N_EXPERTS, tn), 0)
    hit1 = row32 == e1
    hit2 = row32 == e2
    onehot = jnp.where(hit1, 1.0, jnp.where(hit2, 1.0, 0.0))
    sr = lax.broadcasted_iota(I32, (tn, tn), 0)
    tc = lax.broadcasted_iota(I32, (tn, tn), 1)
    before = jnp.where(sr < tc, 1.0, 0.0).astype(BF16)
    prior = jnp.dot(onehot.astype(BF16), before, preferred_element_type=F32) + base_sc[:, 0:1]
    r1 = jnp.sum(jnp.where(hit1, prior, 0.0), axis=0, keepdims=True)
    r2 = jnp.sum(jnp.where(hit2, prior, 0.0), axis=0, keepdims=True)
    base_sc[...] = base_sc[...] + jnp.sum(onehot, axis=1, keepdims=True)

    eid_ref[...] = jnp.concatenate([e1, e2], axis=0)
    wt_ref[...] = jnp.concatenate([w1, w2], axis=0)
    rank_ref[...] = jnp.concatenate([r1, r2], axis=0).astype(I32)
    cnt_ref[...] = base_sc[...].astype(I32)


def _route(logits_t, tn):
    n = logits_t.shape[1]
    two = lambda dt: jax.ShapeDtypeStruct((2, n), dt)
    tspec = pl.BlockSpec((2, tn), lambda i: (0, i))
    return pl.pallas_call(
        _route_kernel,
        out_shape=(two(I32), two(F32), two(I32), jax.ShapeDtypeStruct((N_EXPERTS, LANES), I32)),
        grid=(n // tn,),
        in_specs=[pl.BlockSpec((ROUTER_ROWS, tn), lambda i: (0, i))],
        out_specs=(tspec, tspec, tspec, pl.BlockSpec((N_EXPERTS, LANES), lambda i: (0, 0))),
        scratch_shapes=[pltpu.VMEM((N_EXPERTS, LANES), F32)],
        compiler_params=_params(("arbitrary",)),
        name="route",
    )(logits_t)


def _row_copy(src_hbm, idx_ref, buf, sem, slot, r):
    return pltpu.make_async_copy(src_hbm.at[pl.ds(idx_ref[0, 0, r], 1), :],
                                 buf.at[slot, pl.ds(r, 1), :], sem.at[slot])


def _start_rows(src_hbm, idx_ref, buf, sem, slot, n_rows):
    def body(r, c):
        _row_copy(src_hbm, idx_ref, buf, sem, slot, r).start()
        return c
    lax.fori_loop(0, n_rows, body, 0)


def _wait_rows(src_hbm, idx_ref, buf, sem, slot, n_rows):
    def body(r, c):
        _row_copy(src_hbm, idx_ref, buf, sem, slot, r).wait()
        return c
    lax.fori_loop(0, n_rows, body, 0)


def _gathered_rows(src_hbm, idx_ref, idx_next_ref, buf, sem, n_rows):
    i = pl.program_id(0)
    slot = i % 2

    @pl.when(i == 0)
    def _():
        _start_rows(src_hbm, idx_ref, buf, sem, 0, n_rows)

    @pl.when(i + 1 < pl.num_programs(0))
    def _():
        _start_rows(src_hbm, idx_next_ref, buf, sem, 1 - slot, n_rows)

    _wait_rows(src_hbm, idx_ref, buf, sem, slot, n_rows)
    return slot


def _expert_kernel(be_ref, idx_ref, idxn_ref, h_hbm, wg_ref, wu_ref, wd_ref, y_ref, buf, sem):
    slot = _gathered_rows(h_hbm, idx_ref, idxn_ref, buf, sem, MOE_BLOCK)
    lo, hi = _unpack_pairs(buf[slot])
    xb = jnp.concatenate([lo.astype(BF16), hi.astype(BF16)], axis=1)
    gate = jnp.dot(xb, wg_ref[0], preferred_element_type=F32)
    up = jnp.dot(xb, wu_ref[0], preferred_element_type=F32)
    hid = (gate * jax.nn.sigmoid(gate) * up).astype(BF16)
    y_ref[...] = _pack_pairs(jnp.dot(hid, wd_ref[0], preferred_element_type=F32))


def _experts(h2_rows, row_tok, block_e, w_gate, w_up, w_down):
    n_blocks = block_e.shape[0]
    d = w_gate.shape[1]
    de = w_gate.shape[2]
    idx = row_tok.reshape(n_blocks, 1, MOE_BLOCK)
    smem_blk = lambda f: pl.BlockSpec((1, 1, MOE_BLOCK), f, memory_space=pltpu.SMEM)
    grid_spec = pltpu.PrefetchScalarGridSpec(
        num_scalar_prefetch=1,
        grid=(n_blocks,),
        in_specs=[smem_blk(lambda i, be: (i, 0, 0)),
                  smem_blk(lambda i, be: (jnp.minimum(i + 1, n_blocks - 1), 0, 0)),
                  pl.BlockSpec(memory_space=pl.ANY),
                  pl.BlockSpec((1, d, de), lambda i, be: (be[i], 0, 0)),
                  pl.BlockSpec((1, d, de), lambda i, be: (be[i], 0, 0)),
                  pl.BlockSpec((1, de, d), lambda i, be: (be[i], 0, 0))],
        out_specs=pl.BlockSpec((MOE_BLOCK, d // 2), lambda i, be: (i, 0)),
        scratch_shapes=[pltpu.VMEM((2, MOE_BLOCK, d // 2), jnp.uint32), pltpu.SemaphoreType.DMA((2,))],
    )
    return pl.pallas_call(
        _expert_kernel,
        out_shape=jax.ShapeDtypeStruct((n_blocks * MOE_BLOCK, d // 2), jnp.uint32),
        grid_spec=grid_spec,
        compiler_params=_params(("arbitrary",)),
        name="experts",
    )(block_e, idx, idx, h2_rows, w_gate, w_up, w_down)


def _combine_kernel(idx_ref, idxn_ref, y_hbm, x1_ref, wt_ref, g2_ref, gain_ref, o_ref, buf, sem):
    tm = x1_ref.shape[0]
    slot = _gathered_rows(y_hbm, idx_ref, idxn_ref, buf, sem, 2 * tm)
    y0 = jnp.concatenate(_unpack_pairs(buf[slot, 0:tm, :]), axis=1)
    y1 = jnp.concatenate(_unpack_pairs(buf[slot, tm:2 * tm, :]), axis=1)
    f = wt_ref[:, 0:1] * y0 + wt_ref[:, 1:2] * y1
    o_ref[...] = x1_ref[...] + g2_ref[0] * _rms(f, gain_ref[...])


def _combine(y_rows, dest, wts, x1, gate2, gain, tm):
    n, d = x1.shape
    s = n // gate2.shape[0]
    nt = n // tm
    idx = dest.reshape(nt, 1, 2 * tm)
    smem_blk = lambda f: pl.BlockSpec((1, 1, 2 * tm), f, memory_space=pltpu.SMEM)
    return pl.pallas_call(
        _combine_kernel,
        out_shape=jax.ShapeDtypeStruct((n, d), F32),
        grid=(nt,),
        in_specs=[smem_blk(lambda i: (i, 0, 0)),
                  smem_blk(lambda i: (jnp.minimum(i + 1, nt - 1), 0, 0)),
                  pl.BlockSpec(memory_space=pl.ANY),
                  pl.BlockSpec((tm, d), lambda i: (i, 0)),
                  pl.BlockSpec((tm, 2), lambda i: (i, 0)),
                  pl.BlockSpec((1, 1, d), lambda i: ((i * tm) // s, 0, 0)),
                  pl.BlockSpec((1, d), lambda i: (0, 0))],
        out_specs=pl.BlockSpec((tm, d), lambda i: (i, 0)),
        scratch_shapes=[pltpu.VMEM((2, 2 * tm, d // 2), jnp.uint32), pltpu.SemaphoreType.DMA((2,))],
        compiler_params=_params(("arbitrary",)),
        name="combine",
    )(idx, idx, y_rows, x1, wts, gate2, gain)


def _layer(x, mod, bias, bfar, p):
    bsz, s, d = x.shape
    n_tok = bsz * s
    aw = ATT_HEADS * HEAD_DIM
    sw = p["ssm_d"].shape[0]

    w_in = p["w_in"]
    w_rest = jnp.concatenate([w_in[:, :3 * aw], w_in[:, 3 * aw + sw:]], axis=1).astype(BF16)
    w_ut = w_in[:, 3 * aw:3 * aw + sw].T.astype(BF16)
    q, k, v, u_t, sga, sgs = _inproj(x, mod, p["g_pre_mix"].reshape(1, d), w_rest, w_ut, tm=512)
    att = _moba(q, k, v, bias, bfar)
    tables = _ssm_tables(p["ssm_lambda_re"], p["ssm_lambda_im"], p["ssm_log_dt"], p["ssm_b_re"],
                         p["ssm_b_im"], p["ssm_c_re"], p["ssm_c_im"], p["ssm_d"])
    y4 = _ssm(u_t.reshape(bsz, sw, s // SSM_CHUNK, SSM_CHUNK), *tables)
    y_t = y4.reshape(bsz, sw, s)

    w_rt = jnp.concatenate([p["w_router_expert"].T, p["w_router_group"].T,
                            jnp.zeros((ROUTER_ROWS - N_EXPERTS - N_GROUPS, d), F32)], axis=0)
    b_rt = jnp.concatenate([p["b_router_expert"], p["b_router_group"],
                            jnp.zeros((ROUTER_ROWS - N_EXPERTS - N_GROUPS,), F32)]).reshape(ROUTER_ROWS, 1)
    x1, h2, logits_t = _merge(
        x, att, y_t, sga, sgs, mod, p["g_post_mix"].reshape(1, d), p["g_pre_ffn"].reshape(1, d),
        p["w_att_out"].astype(BF16), p["w_glu_val"].astype(BF16), p["w_glu_gate"].astype(BF16),
        p["w_mix_out"].astype(BF16), w_rt, b_rt, tm=512)

    eid, wts, rank, counts = _route(logits_t, tn=1024)
    counts = counts[:, 0]
    padded = (counts + MOE_BLOCK - 1) // MOE_BLOCK * MOE_BLOCK
    pend = jnp.cumsum(padded)
    pstart = pend - padded
    dest = pstart[eid] + rank
    n_blocks = -(-(n_tok * 2) // MOE_BLOCK) + N_EXPERTS
    row_tok = jnp.zeros((n_blocks * MOE_BLOCK,), I32).at[dest.reshape(-1)].set(
        jnp.tile(jnp.arange(n_tok, dtype=I32), 2))
    block_e = jnp.minimum(jnp.searchsorted(pend, jnp.arange(n_blocks, dtype=I32) * MOE_BLOCK, side="right"),
                          N_EXPERTS - 1).astype(I32)
    y_rows = _experts(h2.reshape(n_tok, d // 2), row_tok, block_e, p["w_exp_gate"].astype(BF16),
                      p["w_exp_up"].astype(BF16), p["w_exp_down"].astype(BF16))
    tm = 256
    dest_tiles = dest.reshape(2, n_tok // tm, tm).transpose(1, 0, 2).reshape(-1)
    out = _combine(y_rows, dest_tiles, wts.T, x1.reshape(n_tok, d), mod[:, 5:6, :],
                   p["g_post_ffn"].reshape(1, d), tm)
    return out.reshape(bsz, s, d)


def kernel(x, c, rel_bias, w_ada, b_ada, g_pre_mix, g_post_mix, w_in, w_att_out, ssm_lambda_re, ssm_lambda_im, ssm_log_dt, ssm_b_re, ssm_b_im, ssm_c_re, ssm_c_im, ssm_d, w_glu_val, w_glu_gate, w_mix_out, g_pre_ffn, g_post_ffn, w_router_group, b_router_group, w_router_expert, b_router_expert, w_exp_gate, w_exp_up, w_exp_down):
    layered = dict(
        w_ada=w_ada, b_ada=b_ada, g_pre_mix=g_pre_mix, g_post_mix=g_post_mix, w_in=w_in,
        w_att_out=w_att_out, ssm_lambda_re=ssm_lambda_re, ssm_lambda_im=ssm_lambda_im,
        ssm_log_dt=ssm_log_dt, ssm_b_re=ssm_b_re, ssm_b_im=ssm_b_im, ssm_c_re=ssm_c_re,
        ssm_c_im=ssm_c_im, ssm_d=ssm_d, w_glu_val=w_glu_val, w_glu_gate=w_glu_gate,
        w_mix_out=w_mix_out, g_pre_ffn=g_pre_ffn, g_post_ffn=g_post_ffn,
        w_router_group=w_router_group, b_router_group=b_router_group,
        w_router_expert=w_router_expert, b_router_expert=b_router_expert,
        w_exp_gate=w_exp_gate, w_exp_up=w_exp_up, w_exp_down=w_exp_down)
    depth = w_ada.shape[0]
    bsz, d = c.shape
    bias = _bias_tiles(rel_bias)
    bfar = rel_bias[NUM_BUCKETS - 1]
    for l in range(depth):
        p = {name: a[l] for name, a in layered.items()}
        mod = _ada(c, p["w_ada"], p["b_ada"]).reshape(bsz, -1, d)
        x = _layer(x, mod, bias, bfar, p)
    return x
```

```python
import functools
import math

import numpy as np
import jax
import jax.numpy as jnp
from jax import lax
from jax.experimental import pallas as pl
from jax.experimental.pallas import tpu as pltpu

F32 = jnp.float32
BF16 = jnp.bfloat16
I32 = jnp.int32

ATT_HEADS = 8
HEAD_DIM = 64
MOBA_BLOCK = 256
MOBA_TOPK = 3
NUM_BUCKETS = 32
MAX_DISTANCE = 128
SSM_GROUP = 16
SSM_STATE = 64
SSM_CHUNK = 128
N_GROUPS = 4
EXPERTS_PER_GROUP = 8
N_EXPERTS = N_GROUPS * EXPERTS_PER_GROUP
MOE_BLOCK = 256
RMS_EPS = 1e-6
NEG_INF = -1e30
LOG2E = math.log2(math.e)
LANES = 128
ROUTER_ROWS = 40
VMEM_LIMIT = 56 * 1024 * 1024
GATHER_UNROLL = 8

_NT = (((1,), (1,)), ((), ()))
_NN = (((1,), (0,)), ((), ()))


def _params(sem, vmem=VMEM_LIMIT):
    return pltpu.CompilerParams(dimension_semantics=sem, vmem_limit_bytes=vmem)


def _split_bf16(a):
    hi = a.astype(BF16)
    lo = (a - hi.astype(F32)).astype(BF16)
    return hi, lo


def _dot3(a, b, dims):
    a_hi, a_lo = _split_bf16(a)
    b_hi, b_lo = _split_bf16(b)
    dg = functools.partial(lax.dot_general, dimension_numbers=dims, preferred_element_type=F32)
    return dg(a_hi, b_hi) + (dg(a_hi, b_lo) + dg(a_lo, b_hi))


def _pack_pairs(a):
    w = a.shape[1] // 2
    bits = pltpu.bitcast(a.astype(BF16).astype(F32), jnp.uint32)
    return (bits[:, :w] >> 16) | (bits[:, w:] & jnp.uint32(0xFFFF0000))


def _unpack_pairs(words):
    lo = pltpu.bitcast(words << 16, F32)
    hi = pltpu.bitcast(words & jnp.uint32(0xFFFF0000), F32)
    return lo, hi


def _ada_kernel(c_ref, w_ref, b_ref, o_ref):
    c = c_ref[...]
    ca = c * jax.nn.sigmoid(c)
    o_ref[...] = _dot3(ca, w_ref[...], _NN) + b_ref[...]


def _ada(c, w, b):
    bsz, d = c.shape
    n = w.shape[1]
    tn = 1536
    return pl.pallas_call(
        _ada_kernel,
        out_shape=jax.ShapeDtypeStruct((bsz, n), F32),
        grid=(n // tn,),
        in_specs=[pl.BlockSpec((bsz, d), lambda j: (0, 0)),
                  pl.BlockSpec((d, tn), lambda j: (0, j)),
                  pl.BlockSpec((1, tn), lambda j: (0, j))],
        out_specs=pl.BlockSpec((bsz, tn), lambda j: (0, j)),
        compiler_params=_params(("parallel",)),
        name="ada",
    )(c, w, b.reshape(1, n))


def _t5_bucket_np(dist):
    n = np.maximum(dist, 0)
    max_exact = NUM_BUCKETS // 2
    nf = np.maximum(n, 1).astype(np.float32)
    large = max_exact + (np.log(nf / np.float32(max_exact)) / np.float32(math.log(MAX_DISTANCE / max_exact))
                         * np.float32(NUM_BUCKETS - max_exact)).astype(np.int32)
    large = np.minimum(large, NUM_BUCKETS - 1)
    return np.where(n < max_exact, n, large).astype(np.int32)


def _bias_kernel(rb_ref, bk_ref, o_ref):
    h = pl.program_id(0)
    for t in range(2):
        bk = bk_ref[t]
        acc = jnp.where(bk < 0, NEG_INF, 0.0).astype(F32)
        for b in range(NUM_BUCKETS):
            acc = jnp.where(bk == b, rb_ref[b, h] * LOG2E, acc)
        o_ref[0, t] = acc


def _bias_tiles(rel_bias):
    qi = np.arange(MOBA_BLOCK)[:, None]
    kj = np.arange(MOBA_BLOCK)[None, :]
    own = np.where(qi >= kj, _t5_bucket_np(qi - kj), -1)
    adj = _t5_bucket_np(qi - kj + MOBA_BLOCK)
    buckets = jnp.asarray(np.stack([own, adj]).astype(np.int32))
    return pl.pallas_call(
        _bias_kernel,
        out_shape=jax.ShapeDtypeStruct((ATT_HEADS, 2, MOBA_BLOCK, MOBA_BLOCK), F32),
        grid=(ATT_HEADS,),
        in_specs=[pl.BlockSpec(memory_space=pltpu.SMEM),
                  pl.BlockSpec((2, MOBA_BLOCK, MOBA_BLOCK), lambda h: (0, 0, 0))],
        out_specs=pl.BlockSpec((1, 2, MOBA_BLOCK, MOBA_BLOCK), lambda h: (h, 0, 0, 0)),
        compiler_params=_params(("parallel",)),
        name="t5_bias",
    )(rel_bias, buckets)


def _rms(x, gain):
    ms = jnp.mean(x * x, axis=-1, keepdims=True)
    return x * lax.rsqrt(ms + RMS_EPS) * gain


def _inproj_kernel(x_ref, mod_ref, g_ref, w_ref, wut_ref,
                   q_ref, k_ref, v_ref, ut_ref, sga_ref, sgs_ref):
    aw = q_ref.shape[2]
    d = x_ref.shape[2]
    x = x_ref[0]
    h = _rms(x, g_ref[...]) * (1.0 + mod_ref[0, 1:2, :]) + mod_ref[0, 0:1, :]
    hb = h.astype(BF16)

    def proj(lo, width):
        return jnp.dot(hb, w_ref[:, lo:lo + width], preferred_element_type=F32)

    q_ref[0] = (proj(0, aw) * (HEAD_DIM ** -0.5 * LOG2E)).astype(BF16)
    k_ref[0] = proj(aw, aw).astype(BF16)
    v_ref[0] = proj(2 * aw, aw).astype(BF16)
    ut_ref[0] = lax.dot_general(wut_ref[...], hb, _NT, preferred_element_type=F32)
    sga_ref[0] = jax.nn.sigmoid(proj(3 * aw, d)).astype(BF16)
    sgs_ref[0] = jax.nn.sigmoid(proj(3 * aw + d, d)).astype(BF16)


def _inproj(x, mod, gain, w_rest, w_ut, tm):
    bsz, s, d = x.shape
    aw = ATT_HEADS * HEAD_DIM
    sw = w_ut.shape[0]
    tok = lambda width, dt: jax.ShapeDtypeStruct((bsz, s, width), dt)
    tspec = lambda width: pl.BlockSpec((1, tm, width), lambda b, i: (b, i, 0))
    return pl.pallas_call(
        _inproj_kernel,
        out_shape=(tok(aw, BF16), tok(aw, BF16), tok(aw, BF16),
                   jax.ShapeDtypeStruct((bsz, sw, s), F32), tok(d, BF16), tok(d, BF16)),
        grid=(bsz, s // tm),
        in_specs=[tspec(d),
                  pl.BlockSpec((1, mod.shape[1], d), lambda b, i: (b, 0, 0)),
                  pl.BlockSpec((1, d), lambda b, i: (0, 0)),
                  pl.BlockSpec(w_rest.shape, lambda b, i: (0, 0)),
                  pl.BlockSpec(w_ut.shape, lambda b, i: (0, 0))],
        out_specs=(tspec(aw), tspec(aw), tspec(aw),
                   pl.BlockSpec((1, sw, tm), lambda b, i: (b, 0, i)), tspec(d), tspec(d)),
        compiler_params=_params(("parallel", "parallel")),
        name="inproj",
    )(x, mod, gain, w_rest, w_ut)


def _moba_kernel(bfar_ref, q_ref, k_ref, v_ref, bias_ref, o_ref,
                 kmh_sc, kml_sc, qaug_sc, m_sc, l_sc, acc_sc):
    blk = MOBA_BLOCK
    i = pl.program_id(1)
    s = k_ref.shape[1]
    nb = s // blk
    nbp = kmh_sc.shape[0]
    npair = q_ref.shape[2] // LANES
    lane = lax.broadcasted_iota(I32, (blk, LANES), 1)
    low_half = lane < HEAD_DIM

    @pl.when(i == 0)
    def _():
        r = lax.broadcasted_iota(I32, (nbp, s), 0)
        c = lax.broadcasted_iota(I32, (nbp, s), 1)
        avg = jnp.where((c >= r * blk) & (c < (r + 1) * blk), 1.0 / blk, 0.0).astype(BF16)
        km = jnp.dot(avg, k_ref[0], preferred_element_type=F32)
        hi, lo = _split_bf16(km)
        kmh_sc[...] = hi
        kml_sc[...] = lo

    row = lax.broadcasted_iota(I32, (nbp, blk), 0)
    half_k = lax.broadcasted_iota(I32, (nbp, LANES), 1) < HEAD_DIM
    for pr in range(npair):
        q2 = q_ref[0, :, pr * LANES:(pr + 1) * LANES]
        kmh = kmh_sc[:, pr * LANES:(pr + 1) * LANES]
        kml = kml_sc[:, pr * LANES:(pr + 1) * LANES]
        for hh in range(2):
            mine = half_k if hh == 0 else jnp.logical_not(half_k)
            gate = (lax.dot_general(jnp.where(mine, kmh, jnp.zeros_like(kmh)), q2, _NT,
                                    preferred_element_type=F32)
                    + lax.dot_general(jnp.where(mine, kml, jnp.zeros_like(kml)), q2, _NT,
                                      preferred_element_type=F32))
            gate = jnp.where(row < i, gate, NEG_INF)
            cnt = jnp.zeros((nbp, blk), F32)
            for m in range(nb):
                other = jnp.broadcast_to(gate[m:m + 1, :], (nbp, blk))
                tie = jnp.where(row > m, 1.0, 0.0)
                cnt = cnt + jnp.where(other > gate, 1.0, 0.0) + jnp.where(other == gate, tie, 0.0)
            chosen = jnp.where(row < i, cnt, float(MOBA_TOPK)) < float(MOBA_TOPK)
            keep_t = jnp.where(row == i, 0.0, jnp.where(chosen, 0.0, NEG_INF))
            keep_t = jnp.concatenate([keep_t, jnp.full((LANES - nbp, blk), NEG_INF, F32)], axis=0)
            qh = jnp.where(low_half if hh == 0 else jnp.logical_not(low_half), q2, jnp.zeros_like(q2))
            qaug_sc[2 * pr + hh] = jnp.concatenate([qh, keep_t.T.astype(BF16)], axis=1)

    def tile(j, bias_of_head, scalar_bias, first):
        start = pl.multiple_of(j * blk, blk)
        onehot = jnp.where(lane == j, 1.0, 0.0).astype(BF16)
        for pr in range(npair):
            kj = k_ref[0, pl.ds(start, blk), pr * LANES:(pr + 1) * LANES]
            vj = v_ref[0, pl.ds(start, blk), pr * LANES:(pr + 1) * LANES]
            kaug = jnp.concatenate([kj, onehot], axis=1)
            pv = []
            alpha = []
            for hh in range(2):
                h = 2 * pr + hh
                sc = lax.dot_general(qaug_sc[h], kaug, _NT, preferred_element_type=F32)
                bias = bias_of_head(h)
                if scalar_bias:
                    m_cur = jnp.max(sc, axis=1, keepdims=True) + bias
                else:
                    sc = sc + bias
                    m_cur = jnp.max(sc, axis=1, keepdims=True)
                if first:
                    m_new = jnp.broadcast_to(m_cur, (blk, LANES))
                else:
                    m_prev = m_sc[h]
                    m_new = jnp.maximum(m_prev, m_cur)
                    a = jnp.exp2(m_prev - m_new)
                    alpha.append(a)
                shift = m_new - bias if scalar_bias else m_new
                pexp = jnp.exp2(sc - jnp.concatenate([shift, shift], axis=1))
                l_cur = jnp.sum(pexp, axis=1, keepdims=True)
                if first:
                    l_sc[h] = jnp.broadcast_to(l_cur, (blk, LANES))
                else:
                    l_sc[h] = a * l_sc[h] + l_cur
                m_sc[h] = m_new
                pv.append(jnp.dot(pexp.astype(BF16), vj, preferred_element_type=F32))
            upd = jnp.where(low_half, pv[0], pv[1])
            if first:
                acc_sc[pr] = upd
            else:
                acc_sc[pr] = jnp.where(low_half, alpha[0], alpha[1]) * acc_sc[pr] + upd

    tile(i, lambda h: bias_ref[h, 0], False, True)

    @pl.when(i >= 1)
    def _():
        tile(i - 1, lambda h: bias_ref[h, 1], False, False)

    def far(j, carry):
        tile(j, lambda h: bfar_ref[h], True, False)
        return carry

    lax.fori_loop(0, jnp.maximum(i - 1, 0), far, 0)

    for pr in range(npair):
        inv = jnp.where(low_half, 1.0 / l_sc[2 * pr], 1.0 / l_sc[2 * pr + 1])
        o_ref[0, :, pr * LANES:(pr + 1) * LANES] = (acc_sc[pr] * inv).astype(BF16)


def _moba(q, k, v, bias, bfar):
    bsz, s, aw = q.shape
    blk = MOBA_BLOCK
    nb = s // blk
    nbp = -(-nb // 16) * 16
    heads = bias.shape[0]
    grid_spec = pltpu.PrefetchScalarGridSpec(
        num_scalar_prefetch=1,
        grid=(bsz, nb),
        in_specs=[pl.BlockSpec((1, blk, aw), lambda b, i, _: (b, i, 0)),
                  pl.BlockSpec((1, s, aw), lambda b, i, _: (b, 0, 0)),
                  pl.BlockSpec((1, s, aw), lambda b, i, _: (b, 0, 0)),
                  pl.BlockSpec(bias.shape, lambda b, i, _: (0, 0, 0, 0))],
        out_specs=pl.BlockSpec((1, blk, aw), lambda b, i, _: (b, i, 0)),
        scratch_shapes=[pltpu.VMEM((nbp, aw), BF16), pltpu.VMEM((nbp, aw), BF16),
                        pltpu.VMEM((heads, blk, 2 * LANES), BF16),
                        pltpu.VMEM((heads, blk, LANES), F32), pltpu.VMEM((heads, blk, LANES), F32),
                        pltpu.VMEM((heads // 2, blk, LANES), F32)],
    )
    return pl.pallas_call(
        _moba_kernel,
        out_shape=jax.ShapeDtypeStruct((bsz, s, aw), BF16),
        grid_spec=grid_spec,
        compiler_params=_params(("parallel", "arbitrary")),
        name="moba",
    )(bfar, q, k, v, bias)


def _ssm_tables(lam_re, lam_im, log_dt, b_re, b_im, c_re, c_im, d_skip):
    L = SSM_CHUNK
    g = lam_re.shape[0]
    dt = jnp.exp(log_dt)[:, None]
    lr, li = lam_re, lam_im
    mag = jnp.exp(lr * dt)
    ab_re, ab_im = mag * jnp.cos(li * dt), mag * jnp.sin(li * dt)
    den = lr * lr + li * li
    nr, ni = ab_re - 1.0, ab_im
    f_re, f_im = (nr * lr + ni * li) / den, (ni * lr - nr * li) / den
    bb_re = f_re[..., None] * b_re - f_im[..., None] * b_im
    bb_im = f_re[..., None] * b_im + f_im[..., None] * b_re
    n = jnp.arange(L + 1, dtype=F32)[:, None, None]
    pw_mag = jnp.exp(n * (lr * dt)[None])
    pw_re, pw_im = pw_mag * jnp.cos(n * (li * dt)[None]), pw_mag * jnp.sin(n * (li * dt)[None])
    ca_re = c_re[None] * pw_re[:L, :, None, :] - c_im[None] * pw_im[:L, :, None, :]
    ca_im = c_re[None] * pw_im[:L, :, None, :] + c_im[None] * pw_re[:L, :, None, :]
    hp = lax.Precision.HIGHEST
    kern = (jnp.einsum("ngcp,gpd->gdcn", ca_re, bb_re, precision=hp)
            - jnp.einsum("ngcp,gpd->gdcn", ca_im, bb_im, precision=hp))
    skip = jnp.asarray(np.eye(SSM_GROUP, dtype=np.float32))[None, :, :, None] * d_skip.reshape(g, 1, SSM_GROUP, 1)
    kern = kern + skip * jnp.asarray((np.arange(L) == 0).astype(np.float32))
    kern = kern.reshape(g, SSM_GROUP * SSM_GROUP, L)
    rev_re, rev_im = pw_re[L - 1::-1][:L], pw_im[L - 1::-1][:L]
    win_re = rev_re[..., None] * bb_re[None] - rev_im[..., None] * bb_im[None]
    win_im = rev_re[..., None] * bb_im[None] + rev_im[..., None] * bb_re[None]
    w_in = jnp.concatenate([win_re, win_im], axis=2)
    w_in = w_in.transpose(1, 3, 0, 2).reshape(g, SSM_GROUP * L, 2 * SSM_STATE)
    fw_re, fw_im = pw_re[1:], pw_im[1:]
    wo_re = c_re[None] * fw_re[:, :, None, :] - c_im[None] * fw_im[:, :, None, :]
    wo_im = -(c_re[None] * fw_im[:, :, None, :] + c_im[None] * fw_re[:, :, None, :])
    w_out = jnp.concatenate([wo_re, wo_im], axis=3)
    w_out = w_out.transpose(1, 3, 2, 0).reshape(g, 2 * SSM_STATE, SSM_GROUP * L)
    a_chunk = jnp.stack([jnp.concatenate([pw_re[L], pw_re[L]], axis=-1),
                         jnp.concatenate([-pw_im[L], pw_im[L]], axis=-1)], axis=1)
    return kern, w_in.astype(BF16), w_out.astype(BF16), a_chunk


def _ssm_kernel(u_ref, kern_ref, win_ref, wout_ref, ac_ref, y_ref, toep_sc):
    bsz, ng, s = u_ref.shape
    L = SSM_CHUNK
    nc = s // L
    s_ix = lax.broadcasted_iota(I32, (L, L), 0)
    t_ix = lax.broadcasted_iota(I32, (L, L), 1)

    def build(cp, carry):
        r0 = pl.multiple_of(cp * L, L)
        for c in range(ng):
            vec = kern_ref[0, pl.ds(cp * ng + c, 1), :]
            lagged = pltpu.roll(jnp.broadcast_to(vec, (L, L)), 0, 1, stride=1, stride_axis=0)
            toep_sc[pl.ds(r0, L), c * L:(c + 1) * L] = jnp.where(t_ix >= s_ix, lagged, 0.0).astype(BF16)
        return carry

    lax.fori_loop(0, ng, build, 0)

    u = jnp.concatenate(
        [jnp.concatenate([u_ref[:, c, ch * L:(ch + 1) * L] for c in range(ng)], axis=1) for ch in range(nc)],
        axis=0).astype(BF16)
    st = jnp.dot(u, win_ref[0], preferred_element_type=F32)
    a1 = ac_ref[0, 0:1, :]
    a2 = ac_ref[0, 1:2, :]
    state = jnp.zeros((bsz, 2 * SSM_STATE), F32)
    prevs = []
    for ch in range(nc):
        prevs.append(state)
        state = a1 * state + a2 * pltpu.roll(state, SSM_STATE, 1) + st[ch * bsz:(ch + 1) * bsz]
    prev = jnp.concatenate(prevs, axis=0).astype(BF16)
    y = (jnp.dot(u, toep_sc[...], preferred_element_type=F32)
         + jnp.dot(prev, wout_ref[0], preferred_element_type=F32))
    for ch in range(nc):
        for c in range(ng):
            y_ref[:, c, ch * L:(ch + 1) * L] = y[ch * bsz:(ch + 1) * bsz, c * L:(c + 1) * L]


def _ssm(u_t, kern, w_in, w_out, a_chunk):
    bsz, sw, s = u_t.shape
    g = sw // SSM_GROUP
    blk = pl.BlockSpec((bsz, SSM_GROUP, s), lambda j: (0, j, 0))
    per_group = lambda a: pl.BlockSpec((1,) + a.shape[1:], lambda j: (j, 0, 0))
    return pl.pallas_call(
        _ssm_kernel,
        out_shape=jax.ShapeDtypeStruct(u_t.shape, F32),
        grid=(g,),
        in_specs=[blk, per_group(kern), per_group(w_in), per_group(w_out), per_group(a_chunk)],
        out_specs=blk,
        scratch_shapes=[pltpu.VMEM((SSM_GROUP * SSM_CHUNK, SSM_GROUP * SSM_CHUNK), BF16)],
        compiler_params=_params(("parallel",)),
        name="ssm",
    )(u_t, kern, w_in, w_out, a_chunk)


def _gelu_tanh(x):
    return 0.5 * x * (1.0 + jnp.tanh(math.sqrt(2.0 / math.pi) * (x + 0.044715 * (x * x * x))))


def _merge_kernel(x_ref, att_ref, yt_ref, sga_ref, sgs_ref, mod_ref, gpost_ref, gpre_ref,
                  wao_ref, wgv_ref, wgg_ref, wmo_ref, wrt_ref, brt_ref,
                  x1_ref, h2_ref, lt_ref):
    a_br = jnp.dot(att_ref[0], wao_ref[...], preferred_element_type=F32)
    z = _gelu_tanh(yt_ref[0]).T.astype(BF16)
    s_br = (jnp.dot(z, wgv_ref[...], preferred_element_type=F32)
            * jax.nn.sigmoid(jnp.dot(z, wgg_ref[...], preferred_element_type=F32)))
    merged = sga_ref[0].astype(F32) * a_br + sgs_ref[0].astype(F32) * s_br
    mix = jnp.dot(merged.astype(BF16), wmo_ref[...], preferred_element_type=F32)
    x1 = x_ref[0] + mod_ref[0, 2:3, :] * _rms(mix, gpost_ref[...])
    x1_ref[0] = x1
    h2 = _rms(x1, gpre_ref[...]) * (1.0 + mod_ref[0, 4:5, :]) + mod_ref[0, 3:4, :]
    h2_ref[0] = _pack_pairs(h2)
    lt_ref[...] = _dot3(wrt_ref[...], h2, _NT) + brt_ref[...]


def _merge(x, att, yt, sga, sgs, mod, g_post, g_pre, w_ao, w_gv, w_gg, w_mo, w_rt, b_rt, tm):
    bsz, s, d = x.shape
    aw = att.shape[2]
    sw = yt.shape[1]
    nt = s // tm
    tspec = lambda width: pl.BlockSpec((1, tm, width), lambda b, i: (b, i, 0))
    full = lambda a: pl.BlockSpec(a.shape, lambda b, i: (0,) * a.ndim)
    return pl.pallas_call(
        _merge_kernel,
        out_shape=(jax.ShapeDtypeStruct((bsz, s, d), F32), jax.ShapeDtypeStruct((bsz, s, d // 2), jnp.uint32),
                   jax.ShapeDtypeStruct((ROUTER_ROWS, bsz * s), F32)),
        grid=(bsz, nt),
        in_specs=[tspec(d), tspec(aw), pl.BlockSpec((1, sw, tm), lambda b, i: (b, 0, i)),
                  tspec(d), tspec(d),
                  pl.BlockSpec((1, mod.shape[1], d), lambda b, i: (b, 0, 0)),
                  full(g_post), full(g_pre), full(w_ao), full(w_gv), full(w_gg), full(w_mo),
                  full(w_rt), full(b_rt)],
        out_specs=(tspec(d), tspec(d // 2), pl.BlockSpec((ROUTER_ROWS, tm), lambda b, i: (0, b * nt + i))),
        compiler_params=_params(("parallel", "parallel")),
        name="merge",
    )(x, att, yt, sga, sgs, mod, g_post, g_pre, w_ao, w_gv, w_gg, w_mo, w_rt, b_rt)


def _route_kernel(lt_ref, eid_ref, wt_ref, dest_ref, cnt_ref, base_sc, pstart_sc):
    tn = lt_ref.shape[1]
    epg = EXPERTS_PER_GROUP
    ph = pl.program_id(0)
    step = pl.program_id(1)

    @pl.when((ph == 0) & (step == 0))
    def _():
        base_sc[...] = jnp.zeros_like(base_sc)

    @pl.when((ph == 1) & (step == 0))
    def _():
        total = base_sc[...]
        cnt_ref[...] = total.astype(I32)
        padded = jnp.floor((total + (MOE_BLOCK - 1.0)) * (1.0 / MOE_BLOCK)) * MOE_BLOCK
        r = lax.broadcasted_iota(I32, (N_EXPERTS, N_EXPERTS), 0)
        c = lax.broadcasted_iota(I32, (N_EXPERTS, N_EXPERTS), 1)
        before = jnp.where(c < r, 1.0, 0.0)
        pstart_sc[...] = _dot3(before, padded, _NN)
        base_sc[...] = jnp.zeros_like(base_sc)

    row8 = lax.broadcasted_iota(I32, (epg, tn), 0)
    gl = lt_ref[N_EXPERTS:N_EXPERTS + epg, :]
    gl = jnp.where(row8 < N_GROUPS, gl, -jnp.inf)
    gmax = jnp.max(gl, axis=0, keepdims=True)
    gidx = jnp.min(jnp.where(gl == gmax, row8, epg), axis=0, keepdims=True)

    el = jnp.zeros((epg, tn), F32)
    for g in range(N_GROUPS):
        el = jnp.where(gidx == g, lt_ref[g * epg:(g + 1) * epg, :], el)
    m1 = jnp.max(el, axis=0, keepdims=True)
    i1 = jnp.min(jnp.where(el == m1, row8, epg), axis=0, keepdims=True)
    el2 = jnp.where(row8 == i1, -jnp.inf, el)
    m2 = jnp.max(el2, axis=0, keepdims=True)
    i2 = jnp.min(jnp.where(el2 == m2, row8, epg), axis=0, keepdims=True)
    e1 = gidx * epg + i1
    e2 = gidx * epg + i2

    row32 = lax.broadcasted_iota(I32, (N_EXPERTS, tn), 0)
    hit1 = row32 == e1
    hit2 = row32 == e2
    onehot = jnp.where(hit1, 1.0, jnp.where(hit2, 1.0, 0.0))

    @pl.when(ph == 1)
    def _():
        g_p = 1.0 / jnp.sum(jnp.exp(gl - gmax), axis=0, keepdims=True)
        zsum = jnp.sum(jnp.exp(el - m1), axis=0, keepdims=True)
        p1 = 1.0 / zsum
        p2 = jnp.exp(m2 - m1) / zsum
        sr = lax.broadcasted_iota(I32, (tn, tn), 0)
        tc = lax.broadcasted_iota(I32, (tn, tn), 1)
        earlier = jnp.where(sr < tc, 1.0, 0.0).astype(BF16)
        place = (jnp.dot(onehot.astype(BF16), earlier, preferred_element_type=F32)
                 + base_sc[:, 0:1] + pstart_sc[:, 0:1])
        d1 = jnp.sum(jnp.where(hit1, place, 0.0), axis=0, keepdims=True)
        d2 = jnp.sum(jnp.where(hit2, place, 0.0), axis=0, keepdims=True)
        eid_ref[...] = jnp.concatenate([e1, e2], axis=0)
        wt_ref[...] = jnp.concatenate([g_p * p1 / (p1 + p2), g_p * p2 / (p1 + p2)], axis=0)
        dest_ref[...] = jnp.concatenate([d1, d2], axis=0).astype(I32)

    base_sc[...] = base_sc[...] + jnp.sum(onehot, axis=1, keepdims=True)


def _route(logits_t, tn):
    n = logits_t.shape[1]
    two = lambda dt: jax.ShapeDtypeStruct((2, n), dt)
    tspec = pl.BlockSpec((2, tn), lambda ph, i: (0, i * ph))
    return pl.pallas_call(
        _route_kernel,
        out_shape=(two(I32), two(F32), two(I32), jax.ShapeDtypeStruct((N_EXPERTS, LANES), I32)),
        grid=(2, n // tn),
        in_specs=[pl.BlockSpec((ROUTER_ROWS, tn), lambda ph, i: (0, i))],
        out_specs=(tspec, tspec, tspec, pl.BlockSpec((N_EXPERTS, LANES), lambda ph, i: (0, 0))),
        scratch_shapes=[pltpu.VMEM((N_EXPERTS, LANES), F32), pltpu.VMEM((N_EXPERTS, LANES), F32)],
        compiler_params=_params(("arbitrary", "arbitrary")),
        name="route",
    )(logits_t)


def _row_copy(src_hbm, idx_ref, buf, sem, slot, r):
    return pltpu.make_async_copy(src_hbm.at[pl.ds(idx_ref[0, 0, r], 1), :],
                                 buf.at[slot, pl.ds(r, 1), :], sem.at[slot])


def _start_rows(src_hbm, idx_ref, buf, sem, slot, n_rows):
    def body(g, c):
        for u in range(GATHER_UNROLL):
            _row_copy(src_hbm, idx_ref, buf, sem, slot, g * GATHER_UNROLL + u).start()
        return c
    lax.fori_loop(0, n_rows // GATHER_UNROLL, body, 0)


def _wait_rows(src_hbm, idx_ref, buf, sem, slot, n_rows):
    def body(g, c):
        for u in range(GATHER_UNROLL):
            _row_copy(src_hbm, idx_ref, buf, sem, slot, g * GATHER_UNROLL + u).wait()
        return c
    lax.fori_loop(0, n_rows // GATHER_UNROLL, body, 0)


def _gathered_rows(src_hbm, idx_ref, idx_next_ref, buf, sem, n_rows):
    i = pl.program_id(0)
    slot = i % 2

    @pl.when(i == 0)
    def _():
        _start_rows(src_hbm, idx_ref, buf, sem, 0, n_rows)

    @pl.when(i + 1 < pl.num_programs(0))
    def _():
        _start_rows(src_hbm, idx_next_ref, buf, sem, 1 - slot, n_rows)

    _wait_rows(src_hbm, idx_ref, buf, sem, slot, n_rows)
    return slot


def _expert_kernel(be_ref, idx_ref, idxn_ref, h_hbm, wg_ref, wu_ref, wd_ref, y_ref, buf, sem):
    slot = _gathered_rows(h_hbm, idx_ref, idxn_ref, buf, sem, MOE_BLOCK)
    lo, hi = _unpack_pairs(buf[slot])
    xb = jnp.concatenate([lo.astype(BF16), hi.astype(BF16)], axis=1)
    gate = jnp.dot(xb, wg_ref[0], preferred_element_type=F32)
    up = jnp.dot(xb, wu_ref[0], preferred_element_type=F32)
    hid = (gate * jax.nn.sigmoid(gate) * up).astype(BF16)
    y_ref[...] = _pack_pairs(jnp.dot(hid, wd_ref[0], preferred_element_type=F32))


def _experts(h2_rows, row_tok, block_e, w_gate, w_up, w_down):
    n_blocks = block_e.shape[0]
    d = w_gate.shape[1]
    de = w_gate.shape[2]
    idx = row_tok.reshape(n_blocks, 1, MOE_BLOCK)
    smem_blk = lambda f: pl.BlockSpec((1, 1, MOE_BLOCK), f, memory_space=pltpu.SMEM)
    grid_spec = pltpu.PrefetchScalarGridSpec(
        num_scalar_prefetch=1,
        grid=(n_blocks,),
        in_specs=[smem_blk(lambda i, be: (i, 0, 0)),
                  smem_blk(lambda i, be: (jnp.minimum(i + 1, n_blocks - 1), 0, 0)),
                  pl.BlockSpec(memory_space=pl.ANY),
                  pl.BlockSpec((1, d, de), lambda i, be: (be[i], 0, 0)),
                  pl.BlockSpec((1, d, de), lambda i, be: (be[i], 0, 0)),
                  pl.BlockSpec((1, de, d), lambda i, be: (be[i], 0, 0))],
        out_specs=pl.BlockSpec((MOE_BLOCK, d // 2), lambda i, be: (i, 0)),
        scratch_shapes=[pltpu.VMEM((2, MOE_BLOCK, d // 2), jnp.uint32), pltpu.SemaphoreType.DMA((2,))],
    )
    return pl.pallas_call(
        _expert_kernel,
        out_shape=jax.ShapeDtypeStruct((n_blocks * MOE_BLOCK, d // 2), jnp.uint32),
        grid_spec=grid_spec,
        compiler_params=_params(("arbitrary",)),
        name="experts",
    )(block_e, idx, idx, h2_rows, w_gate, w_up, w_down)


def _combine_kernel(idx_ref, idxn_ref, y_hbm, x1_ref, wt_ref, g2_ref, gain_ref, o_ref, buf, sem):
    tm = x1_ref.shape[0]
    slot = _gathered_rows(y_hbm, idx_ref, idxn_ref, buf, sem, 2 * tm)
    y0 = jnp.concatenate(_unpack_pairs(buf[slot, 0:tm, :]), axis=1)
    y1 = jnp.concatenate(_unpack_pairs(buf[slot, tm:2 * tm, :]), axis=1)
    f = wt_ref[:, 0:1] * y0 + wt_ref[:, 1:2] * y1
    o_ref[...] = x1_ref[...] + g2_ref[0] * _rms(f, gain_ref[...])


def _combine(y_rows, dest, wts, x1, gate2, gain, tm):
    n, d = x1.shape
    s = n // gate2.shape[0]
    nt = n // tm
    idx = dest.reshape(nt, 1, 2 * tm)
    smem_blk = lambda f: pl.BlockSpec((1, 1, 2 * tm), f, memory_space=pltpu.SMEM)
    return pl.pallas_call(
        _combine_kernel,
        out_shape=jax.ShapeDtypeStruct((n, d), F32),
        grid=(nt,),
        in_specs=[smem_blk(lambda i: (i, 0, 0)),
                  smem_blk(lambda i: (jnp.minimum(i + 1, nt - 1), 0, 0)),
                  pl.BlockSpec(memory_space=pl.ANY),
                  pl.BlockSpec((tm, d), lambda i: (i, 0)),
                  pl.BlockSpec((tm, 2), lambda i: (i, 0)),
                  pl.BlockSpec((1, 1, d), lambda i: ((i * tm) // s, 0, 0)),
                  pl.BlockSpec((1, d), lambda i: (0, 0))],
        out_specs=pl.BlockSpec((tm, d), lambda i: (i, 0)),
        scratch_shapes=[pltpu.VMEM((2, 2 * tm, d // 2), jnp.uint32), pltpu.SemaphoreType.DMA((2,))],
        compiler_params=_params(("arbitrary",)),
        name="combine",
    )(idx, idx, y_rows, x1, wts, gate2, gain)


def _layer(x, mod, bias, bfar, p):
    bsz, s, d = x.shape
    n_tok = bsz * s
    aw = ATT_HEADS * HEAD_DIM
    sw = p["ssm_d"].shape[0]

    w_in = p["w_in"]
    w_rest = jnp.concatenate([w_in[:, :3 * aw], w_in[:, 3 * aw + sw:]], axis=1).astype(BF16)
    w_ut = w_in[:, 3 * aw:3 * aw + sw].T.astype(BF16)
    q, k, v, u_t, sga, sgs = _inproj(x, mod, p["g_pre_mix"].reshape(1, d), w_rest, w_ut, tm=512)
    att = _moba(q, k, v, bias, bfar)
    tables = _ssm_tables(p["ssm_lambda_re"], p["ssm_lambda_im"], p["ssm_log_dt"], p["ssm_b_re"],
                         p["ssm_b_im"], p["ssm_c_re"], p["ssm_c_im"], p["ssm_d"])
    y_t = _ssm(u_t, *tables)

    w_rt = jnp.concatenate([p["w_router_expert"].T, p["w_router_group"].T,
                            jnp.zeros((ROUTER_ROWS - N_EXPERTS - N_GROUPS, d), F32)], axis=0)
    b_rt = jnp.concatenate([p["b_router_expert"], p["b_router_group"],
                            jnp.zeros((ROUTER_ROWS - N_EXPERTS - N_GROUPS,), F32)]).reshape(ROUTER_ROWS, 1)
    x1, h2, logits_t = _merge(
        x, att, y_t, sga, sgs, mod, p["g_post_mix"].reshape(1, d), p["g_pre_ffn"].reshape(1, d),
        p["w_att_out"].astype(BF16), p["w_glu_val"].astype(BF16), p["w_glu_gate"].astype(BF16),
        p["w_mix_out"].astype(BF16), w_rt, b_rt, tm=512)

    eid, wts, dest, counts = _route(logits_t, tn=1024)
    counts = counts[:, 0]
    pend = jnp.cumsum((counts + MOE_BLOCK - 1) // MOE_BLOCK * MOE_BLOCK)
    n_blocks = -(-(n_tok * 2) // MOE_BLOCK) + N_EXPERTS
    row_tok = jnp.zeros((n_blocks * MOE_BLOCK,), I32).at[dest.reshape(-1)].set(
        jnp.tile(jnp.arange(n_tok, dtype=I32), 2))
    block_start = jnp.arange(n_blocks, dtype=I32) * MOE_BLOCK
    block_e = jnp.minimum(jnp.sum((pend[None, :] <= block_start[:, None]).astype(I32), axis=1), N_EXPERTS - 1)
    y_rows = _experts(h2.reshape(n_tok, d // 2), row_tok, block_e, p["w_exp_gate"].astype(BF16),
                      p["w_exp_up"].astype(BF16), p["w_exp_down"].astype(BF16))
    tm = 256
    dest_tiles = dest.reshape(2, n_tok // tm, tm).transpose(1, 0, 2).reshape(-1)
    out = _combine(y_rows, dest_tiles, wts.T, x1.reshape(n_tok, d), mod[:, 5:6, :],
                   p["g_post_ffn"].reshape(1, d), tm)
    return out.reshape(bsz, s, d)


def kernel(x, c, rel_bias, w_ada, b_ada, g_pre_mix, g_post_mix, w_in, w_att_out, ssm_lambda_re, ssm_lambda_im, ssm_log_dt, ssm_b_re, ssm_b_im, ssm_c_re, ssm_c_im, ssm_d, w_glu_val, w_glu_gate, w_mix_out, g_pre_ffn, g_post_ffn, w_router_group, b_router_group, w_router_expert, b_router_expert, w_exp_gate, w_exp_up, w_exp_down):
    layered = dict(
        w_ada=w_ada, b_ada=b_ada, g_pre_mix=g_pre_mix, g_post_mix=g_post_mix, w_in=w_in,
        w_att_out=w_att_out, ssm_lambda_re=ssm_lambda_re, ssm_lambda_im=ssm_lambda_im,
        ssm_log_dt=ssm_log_dt, ssm_b_re=ssm_b_re, ssm_b_im=ssm_b_im, ssm_c_re=ssm_c_re,
        ssm_c_im=ssm_c_im, ssm_d=ssm_d, w_glu_val=w_glu_val, w_glu_gate=w_glu_gate,
        w_mix_out=w_mix_out, g_pre_ffn=g_pre_ffn, g_post_ffn=g_post_ffn,
        w_router_group=w_router_group, b_router_group=b_router_group,
        w_router_expert=w_router_expert, b_router_expert=b_router_expert,
        w_exp_gate=w_exp_gate, w_exp_up=w_exp_up, w_exp_down=w_exp_down)
    depth = w_ada.shape[0]
    bsz, d = c.shape
    bias = _bias_tiles(rel_bias)
    far_bucket = np.unique(_t5_bucket_np(np.arange(MOBA_BLOCK + 1, max(x.shape[1], MOBA_BLOCK + 2))))
    assert far_bucket.size == 1
    bfar = rel_bias[int(far_bucket[0])] * LOG2E
    for l in range(depth):
        p = {name: a[l] for name, a in layered.items()}
        mod = _ada(c, p["w_ada"], p["b_ada"]).reshape(bsz, -1, d)
        x = _layer(x, mod, bias, bfar, p)
    return x
```

```python
import functools
import math

import numpy as np
import jax
import jax.numpy as jnp
from jax import lax
from jax.experimental import pallas as pl
from jax.experimental.pallas import tpu as pltpu

F32 = jnp.float32
BF16 = jnp.bfloat16
I32 = jnp.int32

ATT_HEADS = 8
HEAD_DIM = 64
MOBA_BLOCK = 256
MOBA_TOPK = 3
NUM_BUCKETS = 32
MAX_DISTANCE = 128
SSM_GROUP = 16
SSM_STATE = 64
SSM_CHUNK = 128
N_GROUPS = 4
EXPERTS_PER_GROUP = 8
N_EXPERTS = N_GROUPS * EXPERTS_PER_GROUP
MOE_BLOCK = 256
RMS_EPS = 1e-6
NEG_INF = -1e30
LOG2E = math.log2(math.e)
LANES = 128
ROUTER_ROWS = 40
VMEM_LIMIT = 56 * 1024 * 1024
GATHER_UNROLL = 8

_NT = (((1,), (1,)), ((), ()))
_NN = (((1,), (0,)), ((), ()))


def _params(sem, vmem=VMEM_LIMIT):
    return pltpu.CompilerParams(dimension_semantics=sem, vmem_limit_bytes=vmem)


def _split_bf16(a):
    hi = a.astype(BF16)
    lo = (a - hi.astype(F32)).astype(BF16)
    return hi, lo


def _dot3(a, b, dims):
    a_hi, a_lo = _split_bf16(a)
    b_hi, b_lo = _split_bf16(b)
    dg = functools.partial(lax.dot_general, dimension_numbers=dims, preferred_element_type=F32)
    return dg(a_hi, b_hi) + (dg(a_hi, b_lo) + dg(a_lo, b_hi))


def _pack_pairs(a):
    w = a.shape[1] // 2
    bits = pltpu.bitcast(a.astype(BF16).astype(F32), jnp.uint32)
    return (bits[:, :w] >> 16) | (bits[:, w:] & jnp.uint32(0xFFFF0000))


def _unpack_pairs(words):
    lo = pltpu.bitcast(words << 16, F32)
    hi = pltpu.bitcast(words & jnp.uint32(0xFFFF0000), F32)
    return lo, hi


def _ada_kernel(c_ref, w_ref, b_ref, o_ref):
    c = c_ref[...]
    ca = c * jax.nn.sigmoid(c)
    o_ref[...] = _dot3(ca, w_ref[...], _NN) + b_ref[...]


def _ada(c, w, b):
    bsz, d = c.shape
    n = w.shape[1]
    tn = 1536
    return pl.pallas_call(
        _ada_kernel,
        out_shape=jax.ShapeDtypeStruct((bsz, n), F32),
        grid=(n // tn,),
        in_specs=[pl.BlockSpec((bsz, d), lambda j: (0, 0)),
                  pl.BlockSpec((d, tn), lambda j: (0, j)),
                  pl.BlockSpec((1, tn), lambda j: (0, j))],
        out_specs=pl.BlockSpec((bsz, tn), lambda j: (0, j)),
        compiler_params=_params(("parallel",)),
        name="ada",
    )(c, w, b.reshape(1, n))


def _t5_bucket_np(dist):
    n = np.maximum(dist, 0)
    max_exact = NUM_BUCKETS // 2
    nf = np.maximum(n, 1).astype(np.float32)
    large = max_exact + (np.log(nf / np.float32(max_exact)) / np.float32(math.log(MAX_DISTANCE / max_exact))
                         * np.float32(NUM_BUCKETS - max_exact)).astype(np.int32)
    large = np.minimum(large, NUM_BUCKETS - 1)
    return np.where(n < max_exact, n, large).astype(np.int32)


def _bias_kernel(rb_ref, bk_ref, o_ref):
    h = pl.program_id(0)
    for t in range(2):
        bk = bk_ref[t]
        acc = jnp.where(bk < 0, NEG_INF, 0.0).astype(F32)
        for b in range(NUM_BUCKETS):
            acc = jnp.where(bk == b, rb_ref[b, h] * LOG2E, acc)
        o_ref[0, t] = acc


def _bias_tiles(rel_bias):
    qi = np.arange(MOBA_BLOCK)[:, None]
    kj = np.arange(MOBA_BLOCK)[None, :]
    own = np.where(qi >= kj, _t5_bucket_np(qi - kj), -1)
    adj = _t5_bucket_np(qi - kj + MOBA_BLOCK)
    buckets = jnp.asarray(np.stack([own, adj]).astype(np.int32))
    return pl.pallas_call(
        _bias_kernel,
        out_shape=jax.ShapeDtypeStruct((ATT_HEADS, 2, MOBA_BLOCK, MOBA_BLOCK), F32),
        grid=(ATT_HEADS,),
        in_specs=[pl.BlockSpec(memory_space=pltpu.SMEM),
                  pl.BlockSpec((2, MOBA_BLOCK, MOBA_BLOCK), lambda h: (0, 0, 0))],
        out_specs=pl.BlockSpec((1, 2, MOBA_BLOCK, MOBA_BLOCK), lambda h: (h, 0, 0, 0)),
        compiler_params=_params(("parallel",)),
        name="t5_bias",
    )(rel_bias, buckets)


def _rms(x, gain):
    ms = jnp.mean(x * x, axis=-1, keepdims=True)
    return x * lax.rsqrt(ms + RMS_EPS) * gain


def _inproj_kernel(x_ref, mod_ref, g_ref, w_ref, wut_ref,
                   q_ref, k_ref, v_ref, ut_ref, sga_ref, sgs_ref):
    aw = q_ref.shape[2]
    d = x_ref.shape[2]
    x = x_ref[0]
    h = _rms(x, g_ref[...]) * (1.0 + mod_ref[0, 1:2, :]) + mod_ref[0, 0:1, :]
    hb = h.astype(BF16)

    def proj(lo, width):
        return jnp.dot(hb, w_ref[:, lo:lo + width], preferred_element_type=F32)

    q_ref[0] = (proj(0, aw) * (HEAD_DIM ** -0.5 * LOG2E)).astype(BF16)
    k_ref[0] = proj(aw, aw).astype(BF16)
    v_ref[0] = proj(2 * aw, aw).astype(BF16)
    ut_ref[0] = lax.dot_general(wut_ref[...], hb, _NT, preferred_element_type=F32)
    sga_ref[0] = jax.nn.sigmoid(proj(3 * aw, d)).astype(BF16)
    sgs_ref[0] = jax.nn.sigmoid(proj(3 * aw + d, d)).astype(BF16)


def _inproj(x, mod, gain, w_rest, w_ut, tm):
    bsz, s, d = x.shape
    aw = ATT_HEADS * HEAD_DIM
    sw = w_ut.shape[0]
    tok = lambda width, dt: jax.ShapeDtypeStruct((bsz, s, width), dt)
    tspec = lambda width: pl.BlockSpec((1, tm, width), lambda b, i: (b, i, 0))
    return pl.pallas_call(
        _inproj_kernel,
        out_shape=(tok(aw, BF16), tok(aw, BF16), tok(aw, BF16),
                   jax.ShapeDtypeStruct((bsz, sw, s), F32), tok(d, BF16), tok(d, BF16)),
        grid=(bsz, s // tm),
        in_specs=[tspec(d),
                  pl.BlockSpec((1, mod.shape[1], d), lambda b, i: (b, 0, 0)),
                  pl.BlockSpec((1, d), lambda b, i: (0, 0)),
                  pl.BlockSpec(w_rest.shape, lambda b, i: (0, 0)),
                  pl.BlockSpec(w_ut.shape, lambda b, i: (0, 0))],
        out_specs=(tspec(aw), tspec(aw), tspec(aw),
                   pl.BlockSpec((1, sw, tm), lambda b, i: (b, 0, i)), tspec(d), tspec(d)),
        compiler_params=_params(("parallel", "parallel")),
        name="inproj",
    )(x, mod, gain, w_rest, w_ut)


def _moba_kernel(bfar_ref, q_ref, k_ref, v_ref, bias_ref, o_ref,
                 kmh_sc, kml_sc, qaug_sc, m_sc, acc_sc):
    blk = MOBA_BLOCK
    i = pl.program_id(1)
    s = k_ref.shape[1]
    nb = s // blk
    nbp = kmh_sc.shape[0]
    npair = q_ref.shape[2] // LANES
    lane = lax.broadcasted_iota(I32, (blk, LANES), 1)
    low_half = lane < HEAD_DIM

    @pl.when(i == 0)
    def _():
        r = lax.broadcasted_iota(I32, (nbp, s), 0)
        c = lax.broadcasted_iota(I32, (nbp, s), 1)
        avg = jnp.where((c >= r * blk) & (c < (r + 1) * blk), 1.0 / blk, 0.0).astype(BF16)
        km = jnp.dot(avg, k_ref[0], preferred_element_type=F32)
        hi, lo = _split_bf16(km)
        kmh_sc[...] = hi
        kml_sc[...] = lo

    row = lax.broadcasted_iota(I32, (nbp, blk), 0)
    half_k = lax.broadcasted_iota(I32, (nbp, LANES), 1) < HEAD_DIM
    for pr in range(npair):
        q2 = q_ref[0, :, pr * LANES:(pr + 1) * LANES]
        kmh = kmh_sc[:, pr * LANES:(pr + 1) * LANES]
        kml = kml_sc[:, pr * LANES:(pr + 1) * LANES]
        for hh in range(2):
            mine = half_k if hh == 0 else jnp.logical_not(half_k)
            gate = (lax.dot_general(jnp.where(mine, kmh, jnp.zeros_like(kmh)), q2, _NT,
                                    preferred_element_type=F32)
                    + lax.dot_general(jnp.where(mine, kml, jnp.zeros_like(kml)), q2, _NT,
                                      preferred_element_type=F32))
            gate = jnp.where(row < i, gate, NEG_INF)
            cnt = jnp.zeros((nbp, blk), F32)
            for m in range(nb):
                other = jnp.broadcast_to(gate[m:m + 1, :], (nbp, blk))
                tie = jnp.where(row > m, 1.0, 0.0)
                cnt = cnt + jnp.where(other > gate, 1.0, 0.0) + jnp.where(other == gate, tie, 0.0)
            chosen = jnp.where(row < i, cnt, float(MOBA_TOPK)) < float(MOBA_TOPK)
            keep_t = jnp.where(row == i, 0.0, jnp.where(chosen, 0.0, NEG_INF))
            keep_t = jnp.concatenate([keep_t, jnp.full((LANES - nbp, blk), NEG_INF, F32)], axis=0)
            qh = jnp.where(low_half if hh == 0 else jnp.logical_not(low_half), q2, jnp.zeros_like(q2))
            qaug_sc[2 * pr + hh] = jnp.concatenate([qh, keep_t.T.astype(BF16)], axis=1)

    one_hi = jnp.where(lane == HEAD_DIM, 1.0, 0.0).astype(BF16)
    one_lo = jnp.where(lane == 0, 1.0, 0.0).astype(BF16)

    def tile(j, bias_of_head, scalar_bias, first):
        start = pl.multiple_of(j * blk, blk)
        onehot = jnp.where(lane == j, 1.0, 0.0).astype(BF16)
        for pr in range(npair):
            kj = k_ref[0, pl.ds(start, blk), pr * LANES:(pr + 1) * LANES]
            vj = v_ref[0, pl.ds(start, blk), pr * LANES:(pr + 1) * LANES]
            kaug = jnp.concatenate([kj, onehot], axis=1)
            vaug = (jnp.where(low_half, vj, one_hi), jnp.where(low_half, one_lo, vj))
            for hh in range(2):
                h = 2 * pr + hh
                sc = lax.dot_general(qaug_sc[h], kaug, _NT, preferred_element_type=F32)
                bias = bias_of_head(h)
                if scalar_bias:
                    m_cur = jnp.max(sc, axis=1, keepdims=True) + bias
                else:
                    sc = sc + bias
                    m_cur = jnp.max(sc, axis=1, keepdims=True)
                if first:
                    m_new = jnp.broadcast_to(m_cur, (blk, LANES))
                else:
                    m_prev = m_sc[h]
                    m_new = jnp.maximum(m_prev, m_cur)
                shift = m_new - bias if scalar_bias else m_new
                pexp = jnp.exp2(sc - jnp.concatenate([shift, shift], axis=1))
                pv = jnp.dot(pexp.astype(BF16), vaug[hh], preferred_element_type=F32)
                if first:
                    acc_sc[h] = pv
                else:
                    acc_sc[h] = jnp.exp2(m_prev - m_new) * acc_sc[h] + pv
                m_sc[h] = m_new

    tile(i, lambda h: bias_ref[h, 0], False, True)

    @pl.when(i >= 1)
    def _():
        tile(i - 1, lambda h: bias_ref[h, 1], False, False)

    def far(j, carry):
        tile(j, lambda h: bfar_ref[h], True, False)
        return carry

    lax.fori_loop(0, jnp.maximum(i - 1, 0), far, 0)

    for pr in range(npair):
        acc_e = acc_sc[2 * pr]
        acc_o = acc_sc[2 * pr + 1]
        out = jnp.where(low_half, acc_e / acc_e[:, HEAD_DIM:HEAD_DIM + 1], acc_o / acc_o[:, 0:1])
        o_ref[0, :, pr * LANES:(pr + 1) * LANES] = out.astype(BF16)


def _moba(q, k, v, bias, bfar):
    bsz, s, aw = q.shape
    blk = MOBA_BLOCK
    nb = s // blk
    nbp = -(-nb // 16) * 16
    heads = bias.shape[0]
    grid_spec = pltpu.PrefetchScalarGridSpec(
        num_scalar_prefetch=1,
        grid=(bsz, nb),
        in_specs=[pl.BlockSpec((1, blk, aw), lambda b, i, _: (b, i, 0)),
                  pl.BlockSpec((1, s, aw), lambda b, i, _: (b, 0, 0)),
                  pl.BlockSpec((1, s, aw), lambda b, i, _: (b, 0, 0)),
                  pl.BlockSpec(bias.shape, lambda b, i, _: (0, 0, 0, 0))],
        out_specs=pl.BlockSpec((1, blk, aw), lambda b, i, _: (b, i, 0)),
        scratch_shapes=[pltpu.VMEM((nbp, aw), BF16), pltpu.VMEM((nbp, aw), BF16),
                        pltpu.VMEM((heads, blk, 2 * LANES), BF16),
                        pltpu.VMEM((heads, blk, LANES), F32), pltpu.VMEM((heads, blk, LANES), F32)],
    )
    return pl.pallas_call(
        _moba_kernel,
        out_shape=jax.ShapeDtypeStruct((bsz, s, aw), BF16),
        grid_spec=grid_spec,
        compiler_params=_params(("parallel", "arbitrary")),
        name="moba",
    )(bfar, q, k, v, bias)


def _ssm_tables(lam_re, lam_im, log_dt, b_re, b_im, c_re, c_im, d_skip):
    L = SSM_CHUNK
    g = lam_re.shape[0]
    dt = jnp.exp(log_dt)[:, None]
    lr, li = lam_re, lam_im
    mag = jnp.exp(lr * dt)
    ab_re, ab_im = mag * jnp.cos(li * dt), mag * jnp.sin(li * dt)
    den = lr * lr + li * li
    nr, ni = ab_re - 1.0, ab_im
    f_re, f_im = (nr * lr + ni * li) / den, (ni * lr - nr * li) / den
    bb_re = f_re[..., None] * b_re - f_im[..., None] * b_im
    bb_im = f_re[..., None] * b_im + f_im[..., None] * b_re
    n = jnp.arange(L + 1, dtype=F32)[:, None, None]
    pw_mag = jnp.exp(n * (lr * dt)[None])
    pw_re, pw_im = pw_mag * jnp.cos(n * (li * dt)[None]), pw_mag * jnp.sin(n * (li * dt)[None])
    ca_re = c_re[None] * pw_re[:L, :, None, :] - c_im[None] * pw_im[:L, :, None, :]
    ca_im = c_re[None] * pw_im[:L, :, None, :] + c_im[None] * pw_re[:L, :, None, :]
    hp = lax.Precision.HIGHEST
    kern = (jnp.einsum("ngcp,gpd->gdcn", ca_re, bb_re, precision=hp)
            - jnp.einsum("ngcp,gpd->gdcn", ca_im, bb_im, precision=hp))
    skip = jnp.asarray(np.eye(SSM_GROUP, dtype=np.float32))[None, :, :, None] * d_skip.reshape(g, 1, SSM_GROUP, 1)
    kern = kern + skip * jnp.asarray((np.arange(L) == 0).astype(np.float32))
    kern = kern.reshape(g, SSM_GROUP * SSM_GROUP, L)
    rev_re, rev_im = pw_re[L - 1::-1][:L], pw_im[L - 1::-1][:L]
    win_re = rev_re[..., None] * bb_re[None] - rev_im[..., None] * bb_im[None]
    win_im = rev_re[..., None] * bb_im[None] + rev_im[..., None] * bb_re[None]
    w_in = jnp.concatenate([win_re, win_im], axis=2)
    w_in = w_in.transpose(1, 3, 0, 2).reshape(g, SSM_GROUP * L, 2 * SSM_STATE)
    fw_re, fw_im = pw_re[1:], pw_im[1:]
    wo_re = c_re[None] * fw_re[:, :, None, :] - c_im[None] * fw_im[:, :, None, :]
    wo_im = -(c_re[None] * fw_im[:, :, None, :] + c_im[None] * fw_re[:, :, None, :])
    w_out = jnp.concatenate([wo_re, wo_im], axis=3)
    w_out = w_out.transpose(1, 3, 2, 0).reshape(g, 2 * SSM_STATE, SSM_GROUP * L)
    a_chunk = jnp.stack([jnp.concatenate([pw_re[L], pw_re[L]], axis=-1),
                         jnp.concatenate([-pw_im[L], pw_im[L]], axis=-1)], axis=1)
    return kern, w_in.astype(BF16), w_out.astype(BF16), a_chunk


def _ssm_kernel(u_ref, kern_ref, win_ref, wout_ref, ac_ref, y_ref, toep_sc):
    bsz, ng, s = u_ref.shape
    L = SSM_CHUNK
    nc = s // L
    s_ix = lax.broadcasted_iota(I32, (L, L), 0)
    t_ix = lax.broadcasted_iota(I32, (L, L), 1)

    def build(cp, carry):
        r0 = pl.multiple_of(cp * L, L)
        for c in range(ng):
            vec = kern_ref[0, pl.ds(cp * ng + c, 1), :]
            lagged = pltpu.roll(jnp.broadcast_to(vec, (L, L)), 0, 1, stride=1, stride_axis=0)
            toep_sc[pl.ds(r0, L), c * L:(c + 1) * L] = jnp.where(t_ix >= s_ix, lagged, 0.0).astype(BF16)
        return carry

    lax.fori_loop(0, ng, build, 0)

    u = jnp.concatenate(
        [jnp.concatenate([u_ref[:, c, ch * L:(ch + 1) * L] for c in range(ng)], axis=1) for ch in range(nc)],
        axis=0).astype(BF16)
    st = jnp.dot(u, win_ref[0], preferred_element_type=F32)
    a1 = ac_ref[0, 0:1, :]
    a2 = ac_ref[0, 1:2, :]
    state = jnp.zeros((bsz, 2 * SSM_STATE), F32)
    prevs = []
    for ch in range(nc):
        prevs.append(state)
        state = a1 * state + a2 * pltpu.roll(state, SSM_STATE, 1) + st[ch * bsz:(ch + 1) * bsz]
    prev = jnp.concatenate(prevs, axis=0).astype(BF16)
    y = (jnp.dot(u, toep_sc[...], preferred_element_type=F32)
         + jnp.dot(prev, wout_ref[0], preferred_element_type=F32))
    for ch in range(nc):
        for c in range(ng):
            y_ref[:, c, ch * L:(ch + 1) * L] = y[ch * bsz:(ch + 1) * bsz, c * L:(c + 1) * L]


def _ssm(u_t, kern, w_in, w_out, a_chunk):
    bsz, sw, s = u_t.shape
    g = sw // SSM_GROUP
    blk = pl.BlockSpec((bsz, SSM_GROUP, s), lambda j: (0, j, 0))
    per_group = lambda a: pl.BlockSpec((1,) + a.shape[1:], lambda j: (j, 0, 0))
    return pl.pallas_call(
        _ssm_kernel,
        out_shape=jax.ShapeDtypeStruct(u_t.shape, F32),
        grid=(g,),
        in_specs=[blk, per_group(kern), per_group(w_in), per_group(w_out), per_group(a_chunk)],
        out_specs=blk,
        scratch_shapes=[pltpu.VMEM((SSM_GROUP * SSM_CHUNK, SSM_GROUP * SSM_CHUNK), BF16)],
        compiler_params=_params(("parallel",)),
        name="ssm",
    )(u_t, kern, w_in, w_out, a_chunk)


def _gelu_tanh(x):
    return 0.5 * x * (1.0 + jnp.tanh(math.sqrt(2.0 / math.pi) * (x + 0.044715 * (x * x * x))))


def _merge_kernel(x_ref, att_ref, yt_ref, sga_ref, sgs_ref, mod_ref, gpost_ref, gpre_ref,
                  wao_ref, wgv_ref, wgg_ref, wmo_ref, wrt_ref, brt_ref,
                  x1_ref, h2_ref, lt_ref):
    a_br = jnp.dot(att_ref[0], wao_ref[...], preferred_element_type=F32)
    z = _gelu_tanh(yt_ref[0]).T.astype(BF16)
    s_br = (jnp.dot(z, wgv_ref[...], preferred_element_type=F32)
            * jax.nn.sigmoid(jnp.dot(z, wgg_ref[...], preferred_element_type=F32)))
    merged = sga_ref[0].astype(F32) * a_br + sgs_ref[0].astype(F32) * s_br
    mix = jnp.dot(merged.astype(BF16), wmo_ref[...], preferred_element_type=F32)
    x1 = x_ref[0] + mod_ref[0, 2:3, :] * _rms(mix, gpost_ref[...])
    x1_ref[0] = x1
    h2 = _rms(x1, gpre_ref[...]) * (1.0 + mod_ref[0, 4:5, :]) + mod_ref[0, 3:4, :]
    h2_ref[0] = _pack_pairs(h2)
    lt_ref[...] = _dot3(wrt_ref[...], h2, _NT) + brt_ref[...]


def _merge(x, att, yt, sga, sgs, mod, g_post, g_pre, w_ao, w_gv, w_gg, w_mo, w_rt, b_rt, tm):
    bsz, s, d = x.shape
    aw = att.shape[2]
    sw = yt.shape[1]
    nt = s // tm
    tspec = lambda width: pl.BlockSpec((1, tm, width), lambda b, i: (b, i, 0))
    full = lambda a: pl.BlockSpec(a.shape, lambda b, i: (0,) * a.ndim)
    return pl.pallas_call(
        _merge_kernel,
        out_shape=(jax.ShapeDtypeStruct((bsz, s, d), F32), jax.ShapeDtypeStruct((bsz, s, d // 2), jnp.uint32),
                   jax.ShapeDtypeStruct((ROUTER_ROWS, bsz * s), F32)),
        grid=(bsz, nt),
        in_specs=[tspec(d), tspec(aw), pl.BlockSpec((1, sw, tm), lambda b, i: (b, 0, i)),
                  tspec(d), tspec(d),
                  pl.BlockSpec((1, mod.shape[1], d), lambda b, i: (b, 0, 0)),
                  full(g_post), full(g_pre), full(w_ao), full(w_gv), full(w_gg), full(w_mo),
                  full(w_rt), full(b_rt)],
        out_specs=(tspec(d), tspec(d // 2), pl.BlockSpec((ROUTER_ROWS, tm), lambda b, i: (0, b * nt + i))),
        compiler_params=_params(("parallel", "parallel")),
        name="merge",
    )(x, att, yt, sga, sgs, mod, g_post, g_pre, w_ao, w_gv, w_gg, w_mo, w_rt, b_rt)


def _route_kernel(lt_ref, eid_ref, wt_ref, dest_ref, cnt_ref, base_sc, pstart_sc):
    tn = lt_ref.shape[1]
    epg = EXPERTS_PER_GROUP
    ph = pl.program_id(0)
    step = pl.program_id(1)

    @pl.when((ph == 0) & (step == 0))
    def _():
        base_sc[...] = jnp.zeros_like(base_sc)

    @pl.when((ph == 1) & (step == 0))
    def _():
        total = base_sc[...]
        cnt_ref[...] = total.astype(I32)
        padded = jnp.floor((total + (MOE_BLOCK - 1.0)) * (1.0 / MOE_BLOCK)) * MOE_BLOCK
        r = lax.broadcasted_iota(I32, (N_EXPERTS, N_EXPERTS), 0)
        c = lax.broadcasted_iota(I32, (N_EXPERTS, N_EXPERTS), 1)
        before = jnp.where(c < r, 1.0, 0.0)
        pstart_sc[...] = _dot3(before, padded, _NN)
        base_sc[...] = jnp.zeros_like(base_sc)

    row8 = lax.broadcasted_iota(I32, (epg, tn), 0)
    gl = lt_ref[N_EXPERTS:N_EXPERTS + epg, :]
    gl = jnp.where(row8 < N_GROUPS, gl, -jnp.inf)
    gmax = jnp.max(gl, axis=0, keepdims=True)
    gidx = jnp.min(jnp.where(gl == gmax, row8, epg), axis=0, keepdims=True)

    el = jnp.zeros((epg, tn), F32)
    for g in range(N_GROUPS):
        el = jnp.where(gidx == g, lt_ref[g * epg:(g + 1) * epg, :], el)
    m1 = jnp.max(el, axis=0, keepdims=True)
    i1 = jnp.min(jnp.where(el == m1, row8, epg), axis=0, keepdims=True)
    el2 = jnp.where(row8 == i1, -jnp.inf, el)
    m2 = jnp.max(el2, axis=0, keepdims=True)
    i2 = jnp.min(jnp.where(el2 == m2, row8, epg), axis=0, keepdims=True)
    e1 = gidx * epg + i1
    e2 = gidx * epg + i2

    row32 = lax.broadcasted_iota(I32, (N_EXPERTS, tn), 0)
    hit1 = row32 == e1
    hit2 = row32 == e2
    onehot = jnp.where(hit1, 1.0, jnp.where(hit2, 1.0, 0.0))

    @pl.when(ph == 1)
    def _():
        g_p = 1.0 / jnp.sum(jnp.exp(gl - gmax), axis=0, keepdims=True)
        zsum = jnp.sum(jnp.exp(el - m1), axis=0, keepdims=True)
        p1 = 1.0 / zsum
        p2 = jnp.exp(m2 - m1) / zsum
        sr = lax.broadcasted_iota(I32, (tn, tn), 0)
        tc = lax.broadcasted_iota(I32, (tn, tn), 1)
        earlier = jnp.where(sr < tc, 1.0, 0.0).astype(BF16)
        place = (jnp.dot(onehot.astype(BF16), earlier, preferred_element_type=F32)
                 + base_sc[:, 0:1] + pstart_sc[:, 0:1])
        d1 = jnp.sum(jnp.where(hit1, place, 0.0), axis=0, keepdims=True)
        d2 = jnp.sum(jnp.where(hit2, place, 0.0), axis=0, keepdims=True)
        eid_ref[...] = jnp.concatenate([e1, e2], axis=0)
        wt_ref[...] = jnp.concatenate([g_p * p1 / (p1 + p2), g_p * p2 / (p1 + p2)], axis=0)
        dest_ref[...] = jnp.concatenate([d1, d2], axis=0).astype(I32)

    base_sc[...] = base_sc[...] + jnp.sum(onehot, axis=1, keepdims=True)


def _route(logits_t, tn):
    n = logits_t.shape[1]
    two = lambda dt: jax.ShapeDtypeStruct((2, n), dt)
    tspec = pl.BlockSpec((2, tn), lambda ph, i: (0, i * ph))
    return pl.pallas_call(
        _route_kernel,
        out_shape=(two(I32), two(F32), two(I32), jax.ShapeDtypeStruct((N_EXPERTS, LANES), I32)),
        grid=(2, n // tn),
        in_specs=[pl.BlockSpec((ROUTER_ROWS, tn), lambda ph, i: (0, i))],
        out_specs=(tspec, tspec, tspec, pl.BlockSpec((N_EXPERTS, LANES), lambda ph, i: (0, 0))),
        scratch_shapes=[pltpu.VMEM((N_EXPERTS, LANES), F32), pltpu.VMEM((N_EXPERTS, LANES), F32)],
        compiler_params=_params(("arbitrary", "arbitrary")),
        name="route",
    )(logits_t)


def _row_copy(src_hbm, idx_ref, buf, sem, slot, g, u):
    return pltpu.make_async_copy(src_hbm.at[pl.ds(idx_ref[0, 0, g * GATHER_UNROLL + u], 1), :],
                                 buf.at[slot, g, pl.ds(u, 1), :], sem.at[slot])


def _start_rows(src_hbm, idx_ref, buf, sem, slot):
    def body(g, c):
        for u in range(GATHER_UNROLL):
            _row_copy(src_hbm, idx_ref, buf, sem, slot, g, u).start(priority=u % 2)
        return c
    lax.fori_loop(0, buf.shape[1], body, 0)


def _wait_rows(src_hbm, idx_ref, buf, sem, slot):
    def body(g, c):
        for u in range(GATHER_UNROLL):
            _row_copy(src_hbm, idx_ref, buf, sem, slot, g, u).wait()
        return c
    lax.fori_loop(0, buf.shape[1], body, 0)


def _gathered_rows(src_hbm, idx_ref, idx_next_ref, buf, sem):
    i = pl.program_id(0)
    slot = i % 2

    @pl.when(i == 0)
    def _():
        _start_rows(src_hbm, idx_ref, buf, sem, 0)

    @pl.when(i + 1 < pl.num_programs(0))
    def _():
        _start_rows(src_hbm, idx_next_ref, buf, sem, 1 - slot)

    _wait_rows(src_hbm, idx_ref, buf, sem, slot)
    return buf[slot].reshape(buf.shape[1] * GATHER_UNROLL, buf.shape[3])


def _expert_kernel(be_ref, idx_ref, idxn_ref, h_hbm, wg_ref, wu_ref, wd_ref, y_ref, buf, sem):
    lo, hi = _unpack_pairs(_gathered_rows(h_hbm, idx_ref, idxn_ref, buf, sem))
    xb = jnp.concatenate([lo.astype(BF16), hi.astype(BF16)], axis=1)
    gate = jnp.dot(xb, wg_ref[0], preferred_element_type=F32)
    up = jnp.dot(xb, wu_ref[0], preferred_element_type=F32)
    hid = (gate * jax.nn.sigmoid(gate) * up).astype(BF16)
    y_ref[...] = _pack_pairs(jnp.dot(hid, wd_ref[0], preferred_element_type=F32))


def _experts(h2_rows, row_tok, block_e, w_gate, w_up, w_down):
    n_blocks = block_e.shape[0]
    d = w_gate.shape[1]
    de = w_gate.shape[2]
    idx = row_tok.reshape(n_blocks, 1, MOE_BLOCK)
    smem_blk = lambda f: pl.BlockSpec((1, 1, MOE_BLOCK), f, memory_space=pltpu.SMEM)
    grid_spec = pltpu.PrefetchScalarGridSpec(
        num_scalar_prefetch=1,
        grid=(n_blocks,),
        in_specs=[smem_blk(lambda i, be: (i, 0, 0)),
                  smem_blk(lambda i, be: (jnp.minimum(i + 1, n_blocks - 1), 0, 0)),
                  pl.BlockSpec(memory_space=pl.ANY),
                  pl.BlockSpec((1, d, de), lambda i, be: (be[i], 0, 0)),
                  pl.BlockSpec((1, d, de), lambda i, be: (be[i], 0, 0)),
                  pl.BlockSpec((1, de, d), lambda i, be: (be[i], 0, 0))],
        out_specs=pl.BlockSpec((MOE_BLOCK, d // 2), lambda i, be: (i, 0)),
        scratch_shapes=[pltpu.VMEM((2, MOE_BLOCK // GATHER_UNROLL, GATHER_UNROLL, d // 2), jnp.uint32),
                        pltpu.SemaphoreType.DMA((2,))],
    )
    return pl.pallas_call(
        _expert_kernel,
        out_shape=jax.ShapeDtypeStruct((n_blocks * MOE_BLOCK, d // 2), jnp.uint32),
        grid_spec=grid_spec,
        compiler_params=_params(("arbitrary",)),
        name="experts",
    )(block_e, idx, idx, h2_rows, w_gate, w_up, w_down)


def _combine_kernel(idx_ref, idxn_ref, y_hbm, x1_ref, wt_ref, g2_ref, gain_ref, o_ref, buf, sem):
    tm = x1_ref.shape[0]
    rows = _gathered_rows(y_hbm, idx_ref, idxn_ref, buf, sem)
    y0 = jnp.concatenate(_unpack_pairs(rows[0:tm]), axis=1)
    y1 = jnp.concatenate(_unpack_pairs(rows[tm:2 * tm]), axis=1)
    f = wt_ref[:, 0:1] * y0 + wt_ref[:, 1:2] * y1
    o_ref[...] = x1_ref[...] + g2_ref[0] * _rms(f, gain_ref[...])


def _combine(y_rows, dest, wts, x1, gate2, gain, tm):
    n, d = x1.shape
    s = n // gate2.shape[0]
    nt = n // tm
    idx = dest.reshape(nt, 1, 2 * tm)
    smem_blk = lambda f: pl.BlockSpec((1, 1, 2 * tm), f, memory_space=pltpu.SMEM)
    return pl.pallas_call(
        _combine_kernel,
        out_shape=jax.ShapeDtypeStruct((n, d), F32),
        grid=(nt,),
        in_specs=[smem_blk(lambda i: (i, 0, 0)),
                  smem_blk(lambda i: (jnp.minimum(i + 1, nt - 1), 0, 0)),
                  pl.BlockSpec(memory_space=pl.ANY),
                  pl.BlockSpec((tm, d), lambda i: (i, 0)),
                  pl.BlockSpec((tm, 2), lambda i: (i, 0)),
                  pl.BlockSpec((1, 1, d), lambda i: ((i * tm) // s, 0, 0)),
                  pl.BlockSpec((1, d), lambda i: (0, 0))],
        out_specs=pl.BlockSpec((tm, d), lambda i: (i, 0)),
        scratch_shapes=[pltpu.VMEM((2, 2 * tm // GATHER_UNROLL, GATHER_UNROLL, d // 2), jnp.uint32),
                        pltpu.SemaphoreType.DMA((2,))],
        compiler_params=_params(("arbitrary",)),
        name="combine",
    )(idx, idx, y_rows, x1, wts, gate2, gain)


def _layer(x, mod, bias, bfar, p):
    bsz, s, d = x.shape
    n_tok = bsz * s
    aw = ATT_HEADS * HEAD_DIM
    sw = p["ssm_d"].shape[0]

    w_in = p["w_in"]
    w_rest = jnp.concatenate([w_in[:, :3 * aw], w_in[:, 3 * aw + sw:]], axis=1).astype(BF16)
    w_ut = w_in[:, 3 * aw:3 * aw + sw].T.astype(BF16)
    q, k, v, u_t, sga, sgs = _inproj(x, mod, p["g_pre_mix"].reshape(1, d), w_rest, w_ut, tm=512)
    att = _moba(q, k, v, bias, bfar)
    tables = _ssm_tables(p["ssm_lambda_re"], p["ssm_lambda_im"], p["ssm_log_dt"], p["ssm_b_re"],
                         p["ssm_b_im"], p["ssm_c_re"], p["ssm_c_im"], p["ssm_d"])
    y_t = _ssm(u_t, *tables)

    w_rt = jnp.concatenate([p["w_router_expert"].T, p["w_router_group"].T,
                            jnp.zeros((ROUTER_ROWS - N_EXPERTS - N_GROUPS, d), F32)], axis=0)
    b_rt = jnp.concatenate([p["b_router_expert"], p["b_router_group"],
                            jnp.zeros((ROUTER_ROWS - N_EXPERTS - N_GROUPS,), F32)]).reshape(ROUTER_ROWS, 1)
    x1, h2, logits_t = _merge(
        x, att, y_t, sga, sgs, mod, p["g_post_mix"].reshape(1, d), p["g_pre_ffn"].reshape(1, d),
        p["w_att_out"].astype(BF16), p["w_glu_val"].astype(BF16), p["w_glu_gate"].astype(BF16),
        p["w_mix_out"].astype(BF16), w_rt, b_rt, tm=512)

    eid, wts, dest, counts = _route(logits_t, tn=1024)
    counts = counts[:, 0]
    pend = jnp.cumsum((counts + MOE_BLOCK - 1) // MOE_BLOCK * MOE_BLOCK)
    n_blocks = -(-(n_tok * 2) // MOE_BLOCK) + N_EXPERTS
    row_tok = jnp.zeros((n_blocks * MOE_BLOCK,), I32).at[dest.reshape(-1)].set(
        jnp.tile(jnp.arange(n_tok, dtype=I32), 2))
    block_start = jnp.arange(n_blocks, dtype=I32) * MOE_BLOCK
    block_e = jnp.minimum(jnp.sum((pend[None, :] <= block_start[:, None]).astype(I32), axis=1), N_EXPERTS - 1)
    y_rows = _experts(h2.reshape(n_tok, d // 2), row_tok, block_e, p["w_exp_gate"].astype(BF16),
                      p["w_exp_up"].astype(BF16), p["w_exp_down"].astype(BF16))
    tm = 256
    dest_tiles = dest.reshape(2, n_tok // tm, tm).transpose(1, 0, 2).reshape(-1)
    out = _combine(y_rows, dest_tiles, wts.T, x1.reshape(n_tok, d), mod[:, 5:6, :],
                   p["g_post_ffn"].reshape(1, d), tm)
    return out.reshape(bsz, s, d)


def kernel(x, c, rel_bias, w_ada, b_ada, g_pre_mix, g_post_mix, w_in, w_att_out, ssm_lambda_re, ssm_lambda_im, ssm_log_dt, ssm_b_re, ssm_b_im, ssm_c_re, ssm_c_im, ssm_d, w_glu_val, w_glu_gate, w_mix_out, g_pre_ffn, g_post_ffn, w_router_group, b_router_group, w_router_expert, b_router_expert, w_exp_gate, w_exp_up, w_exp_down):
    layered = dict(
        w_ada=w_ada, b_ada=b_ada, g_pre_mix=g_pre_mix, g_post_mix=g_post_mix, w_in=w_in,
        w_att_out=w_att_out, ssm_lambda_re=ssm_lambda_re, ssm_lambda_im=ssm_lambda_im,
        ssm_log_dt=ssm_log_dt, ssm_b_re=ssm_b_re, ssm_b_im=ssm_b_im, ssm_c_re=ssm_c_re,
        ssm_c_im=ssm_c_im, ssm_d=ssm_d, w_glu_val=w_glu_val, w_glu_gate=w_glu_gate,
        w_mix_out=w_mix_out, g_pre_ffn=g_pre_ffn, g_post_ffn=g_post_ffn,
        w_router_group=w_router_group, b_router_group=b_router_group,
        w_router_expert=w_router_expert, b_router_expert=b_router_expert,
        w_exp_gate=w_exp_gate, w_exp_up=w_exp_up, w_exp_down=w_exp_down)
    depth = w_ada.shape[0]
    bsz, d = c.shape
    bias = _bias_tiles(rel_bias)
    far_bucket = np.unique(_t5_bucket_np(np.arange(MOBA_BLOCK + 1, max(x.shape[1], MOBA_BLOCK + 2))))
    assert far_bucket.size == 1
    bfar = rel_bias[int(far_bucket[0])] * LOG2E
    for l in range(depth):
        p = {name: a[l] for name, a in layered.items()}
        mod = _ada(c, p["w_ada"], p["b_ada"]).reshape(bsz, -1, d)
        x = _layer(x, mod, bias, bfar, p)
    return x
```

```python
import functools
import math

import numpy as np
import jax
import jax.numpy as jnp
from jax import lax
from jax.experimental import pallas as pl
from jax.experimental.pallas import tpu as pltpu

F32 = jnp.float32
BF16 = jnp.bfloat16
I32 = jnp.int32

ATT_HEADS = 8
HEAD_DIM = 64
MOBA_BLOCK = 256
MOBA_TOPK = 3
NUM_BUCKETS = 32
MAX_DISTANCE = 128
SSM_GROUP = 16
SSM_STATE = 64
SSM_CHUNK = 128
N_GROUPS = 4
EXPERTS_PER_GROUP = 8
N_EXPERTS = N_GROUPS * EXPERTS_PER_GROUP
MOE_BLOCK = 256
RMS_EPS = 1e-6
NEG_INF = -1e30
LOG2E = math.log2(math.e)
LANES = 128
ROUTER_ROWS = 40
VMEM_LIMIT = 56 * 1024 * 1024
GATHER_UNROLL = 8

_NT = (((1,), (1,)), ((), ()))
_NN = (((1,), (0,)), ((), ()))


def _params(sem, vmem=VMEM_LIMIT):
    return pltpu.CompilerParams(dimension_semantics=sem, vmem_limit_bytes=vmem)


def _split_bf16(a):
    hi = a.astype(BF16)
    lo = (a - hi.astype(F32)).astype(BF16)
    return hi, lo


def _dot3(a, b, dims):
    a_hi, a_lo = _split_bf16(a)
    b_hi, b_lo = _split_bf16(b)
    dg = functools.partial(lax.dot_general, dimension_numbers=dims, preferred_element_type=F32)
    return dg(a_hi, b_hi) + (dg(a_hi, b_lo) + dg(a_lo, b_hi))


def _pack_pairs(a):
    w = a.shape[1] // 2
    bits = pltpu.bitcast(a.astype(BF16).astype(F32), jnp.uint32)
    return (bits[:, :w] >> 16) | (bits[:, w:] & jnp.uint32(0xFFFF0000))


def _unpack_pairs(words):
    lo = pltpu.bitcast(words << 16, F32)
    hi = pltpu.bitcast(words & jnp.uint32(0xFFFF0000), F32)
    return lo, hi


def _ada_kernel(c_ref, w_ref, b_ref, o_ref):
    c = c_ref[...]
    ca = c * jax.nn.sigmoid(c)
    o_ref[...] = _dot3(ca, w_ref[...], _NN) + b_ref[...]


def _ada(c, w, b):
    bsz, d = c.shape
    n = w.shape[1]
    tn = 1536
    return pl.pallas_call(
        _ada_kernel,
        out_shape=jax.ShapeDtypeStruct((bsz, n), F32),
        grid=(n // tn,),
        in_specs=[pl.BlockSpec((bsz, d), lambda j: (0, 0)),
                  pl.BlockSpec((d, tn), lambda j: (0, j)),
                  pl.BlockSpec((1, tn), lambda j: (0, j))],
        out_specs=pl.BlockSpec((bsz, tn), lambda j: (0, j)),
        compiler_params=_params(("parallel",)),
        name="ada",
    )(c, w, b.reshape(1, n))


def _t5_bucket_np(dist):
    n = np.maximum(dist, 0)
    max_exact = NUM_BUCKETS // 2
    nf = np.maximum(n, 1).astype(np.float32)
    large = max_exact + (np.log(nf / np.float32(max_exact)) / np.float32(math.log(MAX_DISTANCE / max_exact))
                         * np.float32(NUM_BUCKETS - max_exact)).astype(np.int32)
    large = np.minimum(large, NUM_BUCKETS - 1)
    return np.where(n < max_exact, n, large).astype(np.int32)


def _bias_kernel(rb_ref, bk_ref, o_ref):
    h = pl.program_id(0)
    for t in range(2):
        bk = bk_ref[t]
        acc = jnp.where(bk < 0, NEG_INF, 0.0).astype(F32)
        for b in range(NUM_BUCKETS):
            acc = jnp.where(bk == b, rb_ref[b, h] * LOG2E, acc)
        o_ref[0, t] = acc


def _bias_tiles(rel_bias):
    qi = np.arange(MOBA_BLOCK)[:, None]
    kj = np.arange(MOBA_BLOCK)[None, :]
    own = np.where(qi >= kj, _t5_bucket_np(qi - kj), -1)
    adj = _t5_bucket_np(qi - kj + MOBA_BLOCK)
    buckets = jnp.asarray(np.stack([own, adj]).astype(np.int32))
    return pl.pallas_call(
        _bias_kernel,
        out_shape=jax.ShapeDtypeStruct((ATT_HEADS, 2, MOBA_BLOCK, MOBA_BLOCK), F32),
        grid=(ATT_HEADS,),
        in_specs=[pl.BlockSpec(memory_space=pltpu.SMEM),
                  pl.BlockSpec((2, MOBA_BLOCK, MOBA_BLOCK), lambda h: (0, 0, 0))],
        out_specs=pl.BlockSpec((1, 2, MOBA_BLOCK, MOBA_BLOCK), lambda h: (h, 0, 0, 0)),
        compiler_params=_params(("parallel",)),
        name="t5_bias",
    )(rel_bias, buckets)


def _rms(x, gain):
    ms = jnp.mean(x * x, axis=-1, keepdims=True)
    return x * lax.rsqrt(ms + RMS_EPS) * gain


def _inproj_kernel(x_ref, mod_ref, g_ref, w_ref, wut_ref,
                   q_ref, k_ref, v_ref, ut_ref, sga_ref, sgs_ref):
    aw = q_ref.shape[2]
    d = x_ref.shape[2]
    x = x_ref[0]
    h = _rms(x, g_ref[...]) * (1.0 + mod_ref[0, 1:2, :]) + mod_ref[0, 0:1, :]
    hb = h.astype(BF16)

    def proj(lo, width):
        return jnp.dot(hb, w_ref[:, lo:lo + width], preferred_element_type=F32)

    q_ref[0] = (proj(0, aw) * (HEAD_DIM ** -0.5 * LOG2E)).astype(BF16)
    k_ref[0] = proj(aw, aw).astype(BF16)
    v_ref[0] = proj(2 * aw, aw).astype(BF16)
    ut_ref[0] = lax.dot_general(wut_ref[...], hb, _NT, preferred_element_type=F32)
    sga_ref[0] = jax.nn.sigmoid(proj(3 * aw, d)).astype(BF16)
    sgs_ref[0] = jax.nn.sigmoid(proj(3 * aw + d, d)).astype(BF16)


def _inproj(x, mod, gain, w_rest, w_ut, tm):
    bsz, s, d = x.shape
    aw = ATT_HEADS * HEAD_DIM
    sw = w_ut.shape[0]
    tok = lambda width, dt: jax.ShapeDtypeStruct((bsz, s, width), dt)
    tspec = lambda width: pl.BlockSpec((1, tm, width), lambda b, i: (b, i, 0))
    return pl.pallas_call(
        _inproj_kernel,
        out_shape=(tok(aw, BF16), tok(aw, BF16), tok(aw, BF16),
                   jax.ShapeDtypeStruct((bsz, sw, s), F32), tok(d, BF16), tok(d, BF16)),
        grid=(bsz, s // tm),
        in_specs=[tspec(d),
                  pl.BlockSpec((1, mod.shape[1], d), lambda b, i: (b, 0, 0)),
                  pl.BlockSpec((1, d), lambda b, i: (0, 0)),
                  pl.BlockSpec(w_rest.shape, lambda b, i: (0, 0)),
                  pl.BlockSpec(w_ut.shape, lambda b, i: (0, 0))],
        out_specs=(tspec(aw), tspec(aw), tspec(aw),
                   pl.BlockSpec((1, sw, tm), lambda b, i: (b, 0, i)), tspec(d), tspec(d)),
        compiler_params=_params(("parallel", "parallel")),
        name="inproj",
    )(x, mod, gain, w_rest, w_ut)


def _moba_kernel(bfar_ref, q_ref, k_ref, v_ref, bias_ref, o_ref,
                 kmh_sc, kml_sc, qaug_sc, m_sc, acc_sc):
    blk = MOBA_BLOCK
    i = pl.program_id(1)
    s = k_ref.shape[1]
    nb = s // blk
    nbp = kmh_sc.shape[0]
    npair = q_ref.shape[2] // LANES
    lane = lax.broadcasted_iota(I32, (blk, LANES), 1)
    low_half = lane < HEAD_DIM

    @pl.when(i == 0)
    def _():
        r = lax.broadcasted_iota(I32, (nbp, s), 0)
        c = lax.broadcasted_iota(I32, (nbp, s), 1)
        avg = jnp.where((c >= r * blk) & (c < (r + 1) * blk), 1.0 / blk, 0.0).astype(BF16)
        km = jnp.dot(avg, k_ref[0], preferred_element_type=F32)
        hi, lo = _split_bf16(km)
        kmh_sc[...] = hi
        kml_sc[...] = lo

    row = lax.broadcasted_iota(I32, (nbp, blk), 0)
    half_k = lax.broadcasted_iota(I32, (nbp, LANES), 1) < HEAD_DIM
    for pr in range(npair):
        q2 = q_ref[0, :, pr * LANES:(pr + 1) * LANES]
        kmh = kmh_sc[:, pr * LANES:(pr + 1) * LANES]
        kml = kml_sc[:, pr * LANES:(pr + 1) * LANES]
        for hh in range(2):
            mine = half_k if hh == 0 else jnp.logical_not(half_k)
            gate = (lax.dot_general(jnp.where(mine, kmh, jnp.zeros_like(kmh)), q2, _NT,
                                    preferred_element_type=F32)
                    + lax.dot_general(jnp.where(mine, kml, jnp.zeros_like(kml)), q2, _NT,
                                      preferred_element_type=F32))
            gate = jnp.where(row < i, gate, NEG_INF)
            cnt = jnp.zeros((nbp, blk), F32)
            for m in range(nb):
                other = jnp.broadcast_to(gate[m:m + 1, :], (nbp, blk))
                tie = jnp.where(row > m, 1.0, 0.0)
                cnt = cnt + jnp.where(other > gate, 1.0, 0.0) + jnp.where(other == gate, tie, 0.0)
            chosen = jnp.where(row < i, cnt, float(MOBA_TOPK)) < float(MOBA_TOPK)
            keep_t = jnp.where(row == i, 0.0, jnp.where(chosen, 0.0, NEG_INF))
            keep_t = jnp.concatenate([keep_t, jnp.full((LANES - nbp, blk), NEG_INF, F32)], axis=0)
            qh = jnp.where(low_half if hh == 0 else jnp.logical_not(low_half), q2, jnp.zeros_like(q2))
            qaug_sc[2 * pr + hh] = jnp.concatenate([qh, keep_t.T.astype(BF16)], axis=1)

    one_hi = jnp.where(lane == HEAD_DIM, 1.0, 0.0).astype(BF16)
    one_lo = jnp.where(lane == 0, 1.0, 0.0).astype(BF16)

    def tile(j, bias_of_head, scalar_bias, first):
        start = pl.multiple_of(j * blk, blk)
        onehot = jnp.where(lane == j, 1.0, 0.0).astype(BF16)
        for pr in range(npair):
            kj = k_ref[0, pl.ds(start, blk), pr * LANES:(pr + 1) * LANES]
            vj = v_ref[0, pl.ds(start, blk), pr * LANES:(pr + 1) * LANES]
            kaug = jnp.concatenate([kj, onehot], axis=1)
            vaug = (jnp.where(low_half, vj, one_hi), jnp.where(low_half, one_lo, vj))
            for hh in range(2):
                h = 2 * pr + hh
                sc = lax.dot_general(qaug_sc[h], kaug, _NT, preferred_element_type=F32)
                bias = bias_of_head(h)
                if scalar_bias:
                    m_cur = jnp.max(sc, axis=1, keepdims=True) + bias
                else:
                    sc = sc + bias
                    m_cur = jnp.max(sc, axis=1, keepdims=True)
                if first:
                    m_new = jnp.broadcast_to(m_cur, (blk, LANES))
                else:
                    m_prev = m_sc[h]
                    m_new = jnp.maximum(m_prev, m_cur)
                shift = m_new - bias if scalar_bias else m_new
                pexp = jnp.exp2(sc - jnp.concatenate([shift, shift], axis=1))
                pv = jnp.dot(pexp.astype(BF16), vaug[hh], preferred_element_type=F32)
                if first:
                    acc_sc[h] = pv
                else:
                    acc_sc[h] = jnp.exp2(m_prev - m_new) * acc_sc[h] + pv
                m_sc[h] = m_new

    tile(i, lambda h: bias_ref[h, 0], False, True)

    @pl.when(i >= 1)
    def _():
        tile(i - 1, lambda h: bias_ref[h, 1], False, False)

    def far(j, carry):
        tile(j, lambda h: bfar_ref[h], True, False)
        return carry

    lax.fori_loop(0, jnp.maximum(i - 1, 0), far, 0)

    for pr in range(npair):
        acc_e = acc_sc[2 * pr]
        acc_o = acc_sc[2 * pr + 1]
        out = jnp.where(low_half, acc_e / acc_e[:, HEAD_DIM:HEAD_DIM + 1], acc_o / acc_o[:, 0:1])
        o_ref[0, :, pr * LANES:(pr + 1) * LANES] = out.astype(BF16)


def _moba(q, k, v, bias, bfar):
    bsz, s, aw = q.shape
    blk = MOBA_BLOCK
    nb = s // blk
    nbp = -(-nb // 16) * 16
    heads = bias.shape[0]
    grid_spec = pltpu.PrefetchScalarGridSpec(
        num_scalar_prefetch=1,
        grid=(bsz, nb),
        in_specs=[pl.BlockSpec((1, blk, aw), lambda b, i, _: (b, i, 0)),
                  pl.BlockSpec((1, s, aw), lambda b, i, _: (b, 0, 0)),
                  pl.BlockSpec((1, s, aw), lambda b, i, _: (b, 0, 0)),
                  pl.BlockSpec(bias.shape, lambda b, i, _: (0, 0, 0, 0))],
        out_specs=pl.BlockSpec((1, blk, aw), lambda b, i, _: (b, i, 0)),
        scratch_shapes=[pltpu.VMEM((nbp, aw), BF16), pltpu.VMEM((nbp, aw), BF16),
                        pltpu.VMEM((heads, blk, 2 * LANES), BF16),
                        pltpu.VMEM((heads, blk, LANES), F32), pltpu.VMEM((heads, blk, LANES), F32)],
    )
    return pl.pallas_call(
        _moba_kernel,
        out_shape=jax.ShapeDtypeStruct((bsz, s, aw), BF16),
        grid_spec=grid_spec,
        compiler_params=_params(("parallel", "arbitrary")),
        name="moba",
    )(bfar, q, k, v, bias)


def _ssm_tables(lam_re, lam_im, log_dt, b_re, b_im, c_re, c_im, d_skip):
    L = SSM_CHUNK
    g = lam_re.shape[0]
    dt = jnp.exp(log_dt)[:, None]
    lr, li = lam_re, lam_im
    mag = jnp.exp(lr * dt)
    ab_re, ab_im = mag * jnp.cos(li * dt), mag * jnp.sin(li * dt)
    den = lr * lr + li * li
    nr, ni = ab_re - 1.0, ab_im
    f_re, f_im = (nr * lr + ni * li) / den, (ni * lr - nr * li) / den
    bb_re = f_re[..., None] * b_re - f_im[..., None] * b_im
    bb_im = f_re[..., None] * b_im + f_im[..., None] * b_re
    n = jnp.arange(L + 1, dtype=F32)[:, None, None]
    pw_mag = jnp.exp(n * (lr * dt)[None])
    pw_re, pw_im = pw_mag * jnp.cos(n * (li * dt)[None]), pw_mag * jnp.sin(n * (li * dt)[None])
    ca_re = c_re[None] * pw_re[:L, :, None, :] - c_im[None] * pw_im[:L, :, None, :]
    ca_im = c_re[None] * pw_im[:L, :, None, :] + c_im[None] * pw_re[:L, :, None, :]
    hp = lax.Precision.HIGHEST
    kern = (jnp.einsum("ngcp,gpd->gdcn", ca_re, bb_re, precision=hp)
            - jnp.einsum("ngcp,gpd->gdcn", ca_im, bb_im, precision=hp))
    skip = jnp.asarray(np.eye(SSM_GROUP, dtype=np.float32))[None, :, :, None] * d_skip.reshape(g, 1, SSM_GROUP, 1)
    kern = kern + skip * jnp.asarray((np.arange(L) == 0).astype(np.float32))
    kern = kern.reshape(g, SSM_GROUP * SSM_GROUP, L)
    rev_re, rev_im = pw_re[L - 1::-1][:L], pw_im[L - 1::-1][:L]
    win_re = rev_re[..., None] * bb_re[None] - rev_im[..., None] * bb_im[None]
    win_im = rev_re[..., None] * bb_im[None] + rev_im[..., None] * bb_re[None]
    w_in = jnp.concatenate([win_re, win_im], axis=2)
    w_in = w_in.transpose(1, 3, 0, 2).reshape(g, SSM_GROUP * L, 2 * SSM_STATE)
    fw_re, fw_im = pw_re[1:], pw_im[1:]
    wo_re = c_re[None] * fw_re[:, :, None, :] - c_im[None] * fw_im[:, :, None, :]
    wo_im = -(c_re[None] * fw_im[:, :, None, :] + c_im[None] * fw_re[:, :, None, :])
    w_out = jnp.concatenate([wo_re, wo_im], axis=3)
    w_out = w_out.transpose(1, 3, 2, 0).reshape(g, 2 * SSM_STATE, SSM_GROUP * L)
    a_chunk = jnp.stack([jnp.concatenate([pw_re[L], pw_re[L]], axis=-1),
                         jnp.concatenate([-pw_im[L], pw_im[L]], axis=-1)], axis=1)
    return kern, w_in.astype(BF16), w_out.astype(BF16), a_chunk


def _ssm_kernel(u_ref, kern_ref, win_ref, wout_ref, ac_ref, y_ref, toep_sc):
    bsz, ng, s = u_ref.shape
    L = SSM_CHUNK
    nc = s // L
    s_ix = lax.broadcasted_iota(I32, (L, L), 0)
    t_ix = lax.broadcasted_iota(I32, (L, L), 1)

    def build(cp, carry):
        r0 = pl.multiple_of(cp * L, L)
        for c in range(ng):
            vec = kern_ref[0, pl.ds(cp * ng + c, 1), :]
            lagged = pltpu.roll(jnp.broadcast_to(vec, (L, L)), 0, 1, stride=1, stride_axis=0)
            toep_sc[pl.ds(r0, L), c * L:(c + 1) * L] = jnp.where(t_ix >= s_ix, lagged, 0.0).astype(BF16)
        return carry

    lax.fori_loop(0, ng, build, 0)

    u = jnp.concatenate(
        [jnp.concatenate([u_ref[:, c, ch * L:(ch + 1) * L] for c in range(ng)], axis=1) for ch in range(nc)],
        axis=0).astype(BF16)
    st = jnp.dot(u, win_ref[0], preferred_element_type=F32)
    a1 = ac_ref[0, 0:1, :]
    a2 = ac_ref[0, 1:2, :]
    state = jnp.zeros((bsz, 2 * SSM_STATE), F32)
    prevs = []
    for ch in range(nc):
        prevs.append(state)
        state = a1 * state + a2 * pltpu.roll(state, SSM_STATE, 1) + st[ch * bsz:(ch + 1) * bsz]
    prev = jnp.concatenate(prevs, axis=0).astype(BF16)
    y = (jnp.dot(u, toep_sc[...], preferred_element_type=F32)
         + jnp.dot(prev, wout_ref[0], preferred_element_type=F32))
    for ch in range(nc):
        for c in range(ng):
            y_ref[:, c, ch * L:(ch + 1) * L] = y[ch * bsz:(ch + 1) * bsz, c * L:(c + 1) * L]


def _ssm(u_t, kern, w_in, w_out, a_chunk):
    bsz, sw, s = u_t.shape
    g = sw // SSM_GROUP
    blk = pl.BlockSpec((bsz, SSM_GROUP, s), lambda j: (0, j, 0))
    per_group = lambda a: pl.BlockSpec((1,) + a.shape[1:], lambda j: (j, 0, 0))
    return pl.pallas_call(
        _ssm_kernel,
        out_shape=jax.ShapeDtypeStruct(u_t.shape, F32),
        grid=(g,),
        in_specs=[blk, per_group(kern), per_group(w_in), per_group(w_out), per_group(a_chunk)],
        out_specs=blk,
        scratch_shapes=[pltpu.VMEM((SSM_GROUP * SSM_CHUNK, SSM_GROUP * SSM_CHUNK), BF16)],
        compiler_params=_params(("parallel",)),
        name="ssm",
    )(u_t, kern, w_in, w_out, a_chunk)


def _gelu_tanh(x):
    return 0.5 * x * (1.0 + jnp.tanh(math.sqrt(2.0 / math.pi) * (x + 0.044715 * (x * x * x))))


def _merge_kernel(x_ref, att_ref, yt_ref, sga_ref, sgs_ref, mod_ref, gpost_ref, gpre_ref,
                  wao_ref, wgv_ref, wgg_ref, wmo_ref, wrt_ref, brt_ref,
                  x1_ref, h2_ref, lt_ref):
    a_br = jnp.dot(att_ref[0], wao_ref[...], preferred_element_type=F32)
    z = _gelu_tanh(yt_ref[0]).T.astype(BF16)
    s_br = (jnp.dot(z, wgv_ref[...], preferred_element_type=F32)
            * jax.nn.sigmoid(jnp.dot(z, wgg_ref[...], preferred_element_type=F32)))
    merged = sga_ref[0].astype(F32) * a_br + sgs_ref[0].astype(F32) * s_br
    mix = jnp.dot(merged.astype(BF16), wmo_ref[...], preferred_element_type=F32)
    x1 = x_ref[0] + mod_ref[0, 2:3, :] * _rms(mix, gpost_ref[...])
    x1_ref[0] = x1
    h2 = _rms(x1, gpre_ref[...]) * (1.0 + mod_ref[0, 4:5, :]) + mod_ref[0, 3:4, :]
    h2_ref[0] = _pack_pairs(h2)
    lt_ref[...] = _dot3(wrt_ref[...], h2, _NT) + brt_ref[...]


def _merge(x, att, yt, sga, sgs, mod, g_post, g_pre, w_ao, w_gv, w_gg, w_mo, w_rt, b_rt, tm):
    bsz, s, d = x.shape
    aw = att.shape[2]
    sw = yt.shape[1]
    nt = s // tm
    tspec = lambda width: pl.BlockSpec((1, tm, width), lambda b, i: (b, i, 0))
    full = lambda a: pl.BlockSpec(a.shape, lambda b, i: (0,) * a.ndim)
    return pl.pallas_call(
        _merge_kernel,
        out_shape=(jax.ShapeDtypeStruct((bsz, s, d), F32), jax.ShapeDtypeStruct((bsz, s, d // 2), jnp.uint32),
                   jax.ShapeDtypeStruct((ROUTER_ROWS, bsz * s), F32)),
        grid=(bsz, nt),
        in_specs=[tspec(d), tspec(aw), pl.BlockSpec((1, sw, tm), lambda b, i: (b, 0, i)),
                  tspec(d), tspec(d),
                  pl.BlockSpec((1, mod.shape[1], d), lambda b, i: (b, 0, 0)),
                  full(g_post), full(g_pre), full(w_ao), full(w_gv), full(w_gg), full(w_mo),
                  full(w_rt), full(b_rt)],
        out_specs=(tspec(d), tspec(d // 2), pl.BlockSpec((ROUTER_ROWS, tm), lambda b, i: (0, b * nt + i))),
        compiler_params=_params(("parallel", "parallel")),
        name="merge",
    )(x, att, yt, sga, sgs, mod, g_post, g_pre, w_ao, w_gv, w_gg, w_mo, w_rt, b_rt)


def _route_kernel(lt_ref, eid_ref, wt_ref, dest_ref, cnt_ref, base_sc, pstart_sc):
    tn = lt_ref.shape[1]
    epg = EXPERTS_PER_GROUP
    ph = pl.program_id(0)
    step = pl.program_id(1)

    @pl.when((ph == 0) & (step == 0))
    def _():
        base_sc[...] = jnp.zeros_like(base_sc)

    @pl.when((ph == 1) & (step == 0))
    def _():
        total = base_sc[...]
        cnt_ref[...] = total.astype(I32)
        padded = jnp.floor((total + (MOE_BLOCK - 1.0)) * (1.0 / MOE_BLOCK)) * MOE_BLOCK
        r = lax.broadcasted_iota(I32, (N_EXPERTS, N_EXPERTS), 0)
        c = lax.broadcasted_iota(I32, (N_EXPERTS, N_EXPERTS), 1)
        before = jnp.where(c < r, 1.0, 0.0)
        pstart_sc[...] = _dot3(before, padded, _NN)
        base_sc[...] = jnp.zeros_like(base_sc)

    row8 = lax.broadcasted_iota(I32, (epg, tn), 0)
    gl = lt_ref[N_EXPERTS:N_EXPERTS + epg, :]
    gl = jnp.where(row8 < N_GROUPS, gl, -jnp.inf)
    gmax = jnp.max(gl, axis=0, keepdims=True)
    gidx = jnp.min(jnp.where(gl == gmax, row8, epg), axis=0, keepdims=True)

    el = jnp.zeros((epg, tn), F32)
    for g in range(N_GROUPS):
        el = jnp.where(gidx == g, lt_ref[g * epg:(g + 1) * epg, :], el)
    m1 = jnp.max(el, axis=0, keepdims=True)
    i1 = jnp.min(jnp.where(el == m1, row8, epg), axis=0, keepdims=True)
    el2 = jnp.where(row8 == i1, -jnp.inf, el)
    m2 = jnp.max(el2, axis=0, keepdims=True)
    i2 = jnp.min(jnp.where(el2 == m2, row8, epg), axis=0, keepdims=True)
    e1 = gidx * epg + i1
    e2 = gidx * epg + i2

    row32 = lax.broadcasted_iota(I32, (N_EXPERTS, tn), 0)
    hit1 = row32 == e1
    hit2 = row32 == e2
    onehot = jnp.where(hit1, 1.0, jnp.where(hit2, 1.0, 0.0))

    @pl.when(ph == 1)
    def _():
        g_p = 1.0 / jnp.sum(jnp.exp(gl - gmax), axis=0, keepdims=True)
        zsum = jnp.sum(jnp.exp(el - m1), axis=0, keepdims=True)
        p1 = 1.0 / zsum
        p2 = jnp.exp(m2 - m1) / zsum
        sr = lax.broadcasted_iota(I32, (tn, tn), 0)
        tc = lax.broadcasted_iota(I32, (tn, tn), 1)
        earlier = jnp.where(sr < tc, 1.0, 0.0).astype(BF16)
        place = (jnp.dot(onehot.astype(BF16), earlier, preferred_element_type=F32)
                 + base_sc[:, 0:1] + pstart_sc[:, 0:1])
        d1 = jnp.sum(jnp.where(hit1, place, 0.0), axis=0, keepdims=True)
        d2 = jnp.sum(jnp.where(hit2, place, 0.0), axis=0, keepdims=True)
        eid_ref[...] = jnp.concatenate([e1, e2], axis=0)
        wt_ref[...] = jnp.concatenate([g_p * p1 / (p1 + p2), g_p * p2 / (p1 + p2)], axis=0)
        dest_ref[...] = jnp.concatenate([d1, d2], axis=0).astype(I32)

    base_sc[...] = base_sc[...] + jnp.sum(onehot, axis=1, keepdims=True)


def _route(logits_t, tn):
    n = logits_t.shape[1]
    two = lambda dt: jax.ShapeDtypeStruct((2, n), dt)
    tspec = pl.BlockSpec((2, tn), lambda ph, i: (0, i * ph))
    return pl.pallas_call(
        _route_kernel,
        out_shape=(two(I32), two(F32), two(I32), jax.ShapeDtypeStruct((N_EXPERTS, LANES), I32)),
        grid=(2, n // tn),
        in_specs=[pl.BlockSpec((ROUTER_ROWS, tn), lambda ph, i: (0, i))],
        out_specs=(tspec, tspec, tspec, pl.BlockSpec((N_EXPERTS, LANES), lambda ph, i: (0, 0))),
        scratch_shapes=[pltpu.VMEM((N_EXPERTS, LANES), F32), pltpu.VMEM((N_EXPERTS, LANES), F32)],
        compiler_params=_params(("arbitrary", "arbitrary")),
        name="route",
    )(logits_t)


def _row_copy(src_hbm, idx_ref, buf, sem, slot, g, u):
    return pltpu.make_async_copy(src_hbm.at[pl.ds(idx_ref[0, 0, g * GATHER_UNROLL + u], 1), :],
                                 buf.at[slot, g, pl.ds(u, 1), :], sem.at[slot])


def _start_rows(src_hbm, idx_ref, buf, sem, slot):
    def body(g, c):
        for u in range(GATHER_UNROLL):
            _row_copy(src_hbm, idx_ref, buf, sem, slot, g, u).start(priority=u % 2)
        return c
    lax.fori_loop(0, buf.shape[1], body, 0)


def _wait_rows(src_hbm, idx_ref, buf, sem, slot):
    def body(g, c):
        for u in range(GATHER_UNROLL):
            _row_copy(src_hbm, idx_ref, buf, sem, slot, g, u).wait()
        return c
    lax.fori_loop(0, buf.shape[1], body, 0)


def _gathered_rows(src_hbm, idx_ref, idx_next_ref, buf, sem):
    i = pl.program_id(0)
    slot = i % 2

    @pl.when(i == 0)
    def _():
        _start_rows(src_hbm, idx_ref, buf, sem, 0)

    @pl.when(i + 1 < pl.num_programs(0))
    def _():
        _start_rows(src_hbm, idx_next_ref, buf, sem, 1 - slot)

    _wait_rows(src_hbm, idx_ref, buf, sem, slot)
    return buf[slot].reshape(buf.shape[1] * GATHER_UNROLL, buf.shape[3])


def _dispatch_copy(stage, x_hbm, idx_ref, sem, slot, g, u, k):
    tm = idx_ref.shape[2] // 2
    return pltpu.make_async_copy(stage.at[slot, g, pl.ds(u, 1), :],
                                 x_hbm.at[pl.ds(idx_ref[0, 0, k * tm + g * GATHER_UNROLL + u], 1), :],
                                 sem.at[slot])


def _dispatch_rows(stage, x_hbm, idx_ref, sem, slot, wait):
    def body(g, c):
        for u in range(GATHER_UNROLL):
            for k in range(2):
                cp = _dispatch_copy(stage, x_hbm, idx_ref, sem, slot, g, u, k)
                if wait:
                    cp.wait()
                else:
                    cp.start(priority=k)
        return c
    lax.fori_loop(0, stage.shape[1], body, 0)


def _dispatch_kernel(cnt_ref, idx_ref, idxp_ref, h_hbm, x_hbm, stage, zero_sc, lsem, sem, zsem):
    i = pl.program_id(0)
    n_steps = pl.num_programs(0)
    n_blocks = x_hbm.shape[0] // MOE_BLOCK
    slot = i % 2

    def load(step, into):
        return pltpu.make_async_copy(h_hbm.at[step], stage.at[into], lsem.at[into])

    def zero_block(blk_ix):
        return pltpu.make_async_copy(zero_sc, x_hbm.at[pl.ds(blk_ix * MOE_BLOCK, MOE_BLOCK), :], zsem.at[0])

    @pl.when(i == 0)
    def _():
        load(0, 0).start()
        zero_sc[...] = jnp.zeros_like(zero_sc)
        used = jnp.int32(0)
        for e in range(N_EXPERTS):
            cnt = cnt_ref[e]
            used = used + lax.shift_right_logical(cnt + (MOE_BLOCK - 1), 8)
            ragged = jnp.bitwise_and(cnt, MOE_BLOCK - 1) != 0

            @pl.when(ragged)
            def _():
                zero_block(used - 1).start()

            @pl.when(ragged)
            def _():
                zero_block(used - 1).wait()

        def tail(b, c):
            zero_block(b).start()
            zero_block(b).wait()
            return c
        lax.fori_loop(used, n_blocks, tail, 0)

    @pl.when(i >= 1)
    def _():
        _dispatch_rows(stage, x_hbm, idxp_ref, sem, 1 - slot, True)

    @pl.when(i + 1 < n_steps)
    def _():
        load(i + 1, 1 - slot).start()

    load(i, slot).wait()
    _dispatch_rows(stage, x_hbm, idx_ref, sem, slot, False)

    @pl.when(i == n_steps - 1)
    def _():
        _dispatch_rows(stage, x_hbm, idx_ref, sem, slot, True)


def _dispatch(h2_rows, dest_tiles, counts, n_blocks, tm):
    n_tok, w = h2_rows.shape
    nt = n_tok // tm
    idx = dest_tiles.reshape(nt, 1, 2 * tm)
    smem_blk = lambda f: pl.BlockSpec((1, 1, 2 * tm), f, memory_space=pltpu.SMEM)
    grid_spec = pltpu.PrefetchScalarGridSpec(
        num_scalar_prefetch=1,
        grid=(nt,),
        in_specs=[smem_blk(lambda i, c: (i, 0, 0)),
                  smem_blk(lambda i, c: (jnp.maximum(i - 1, 0), 0, 0)),
                  pl.BlockSpec(memory_space=pl.ANY)],
        out_specs=pl.BlockSpec(memory_space=pl.ANY),
        scratch_shapes=[pltpu.VMEM((2, tm // GATHER_UNROLL, GATHER_UNROLL, w), jnp.uint32),
                        pltpu.VMEM((MOE_BLOCK, w), jnp.uint32),
                        pltpu.SemaphoreType.DMA((2,)), pltpu.SemaphoreType.DMA((2,)),
                        pltpu.SemaphoreType.DMA((1,))],
    )
    return pl.pallas_call(
        _dispatch_kernel,
        out_shape=jax.ShapeDtypeStruct((n_blocks * MOE_BLOCK, w), jnp.uint32),
        grid_spec=grid_spec,
        compiler_params=_params(("arbitrary",)),
        name="dispatch",
    )(counts, idx, idx, h2_rows.reshape(nt, tm // GATHER_UNROLL, GATHER_UNROLL, w))


def _expert_kernel(be_ref, x_ref, wg_ref, wu_ref, wd_ref, y_ref):
    lo, hi = _unpack_pairs(x_ref[...])
    xb = jnp.concatenate([lo.astype(BF16), hi.astype(BF16)], axis=1)
    gate = jnp.dot(xb, wg_ref[0], preferred_element_type=F32)
    up = jnp.dot(xb, wu_ref[0], preferred_element_type=F32)
    hid = (gate * jax.nn.sigmoid(gate) * up).astype(BF16)
    y_ref[...] = _pack_pairs(jnp.dot(hid, wd_ref[0], preferred_element_type=F32))


def _experts(x_rows, block_e, w_gate, w_up, w_down):
    n_blocks = block_e.shape[0]
    d = w_gate.shape[1]
    de = w_gate.shape[2]
    grid_spec = pltpu.PrefetchScalarGridSpec(
        num_scalar_prefetch=1,
        grid=(n_blocks,),
        in_specs=[pl.BlockSpec((MOE_BLOCK, d // 2), lambda i, be: (i, 0)),
                  pl.BlockSpec((1, d, de), lambda i, be: (be[i], 0, 0)),
                  pl.BlockSpec((1, d, de), lambda i, be: (be[i], 0, 0)),
                  pl.BlockSpec((1, de, d), lambda i, be: (be[i], 0, 0))],
        out_specs=pl.BlockSpec((MOE_BLOCK, d // 2), lambda i, be: (i, 0)),
    )
    return pl.pallas_call(
        _expert_kernel,
        out_shape=jax.ShapeDtypeStruct((n_blocks * MOE_BLOCK, d // 2), jnp.uint32),
        grid_spec=grid_spec,
        compiler_params=_params(("arbitrary",)),
        name="experts",
    )(block_e, x_rows, w_gate, w_up, w_down)


def _combine_kernel(idx_ref, idxn_ref, y_hbm, x1_ref, wt_ref, g2_ref, gain_ref, o_ref, buf, sem):
    tm = x1_ref.shape[0]
    rows = _gathered_rows(y_hbm, idx_ref, idxn_ref, buf, sem)
    y0 = jnp.concatenate(_unpack_pairs(rows[0:tm]), axis=1)
    y1 = jnp.concatenate(_unpack_pairs(rows[tm:2 * tm]), axis=1)
    f = wt_ref[:, 0:1] * y0 + wt_ref[:, 1:2] * y1
    o_ref[...] = x1_ref[...] + g2_ref[0] * _rms(f, gain_ref[...])


def _combine(y_rows, dest, wts, x1, gate2, gain, tm):
    n, d = x1.shape
    s = n // gate2.shape[0]
    nt = n // tm
    idx = dest.reshape(nt, 1, 2 * tm)
    smem_blk = lambda f: pl.BlockSpec((1, 1, 2 * tm), f, memory_space=pltpu.SMEM)
    return pl.pallas_call(
        _combine_kernel,
        out_shape=jax.ShapeDtypeStruct((n, d), F32),
        grid=(nt,),
        in_specs=[smem_blk(lambda i: (i, 0, 0)),
                  smem_blk(lambda i: (jnp.minimum(i + 1, nt - 1), 0, 0)),
                  pl.BlockSpec(memory_space=pl.ANY),
                  pl.BlockSpec((tm, d), lambda i: (i, 0)),
                  pl.BlockSpec((tm, 2), lambda i: (i, 0)),
                  pl.BlockSpec((1, 1, d), lambda i: ((i * tm) // s, 0, 0)),
                  pl.BlockSpec((1, d), lambda i: (0, 0))],
        out_specs=pl.BlockSpec((tm, d), lambda i: (i, 0)),
        scratch_shapes=[pltpu.VMEM((2, 2 * tm // GATHER_UNROLL, GATHER_UNROLL, d // 2), jnp.uint32),
                        pltpu.SemaphoreType.DMA((2,))],
        compiler_params=_params(("arbitrary",)),
        name="combine",
    )(idx, idx, y_rows, x1, wts, gate2, gain)


def _layer(x, mod, bias, bfar, p):
    bsz, s, d = x.shape
    n_tok = bsz * s
    aw = ATT_HEADS * HEAD_DIM
    sw = p["ssm_d"].shape[0]

    w_in = p["w_in"]
    w_rest = jnp.concatenate([w_in[:, :3 * aw], w_in[:, 3 * aw + sw:]], axis=1).astype(BF16)
    w_ut = w_in[:, 3 * aw:3 * aw + sw].T.astype(BF16)
    q, k, v, u_t, sga, sgs = _inproj(x, mod, p["g_pre_mix"].reshape(1, d), w_rest, w_ut, tm=512)
    att = _moba(q, k, v, bias, bfar)
    tables = _ssm_tables(p["ssm_lambda_re"], p["ssm_lambda_im"], p["ssm_log_dt"], p["ssm_b_re"],
                         p["ssm_b_im"], p["ssm_c_re"], p["ssm_c_im"], p["ssm_d"])
    y_t = _ssm(u_t, *tables)

    w_rt = jnp.concatenate([p["w_router_expert"].T, p["w_router_group"].T,
                            jnp.zeros((ROUTER_ROWS - N_EXPERTS - N_GROUPS, d), F32)], axis=0)
    b_rt = jnp.concatenate([p["b_router_expert"], p["b_router_group"],
                            jnp.zeros((ROUTER_ROWS - N_EXPERTS - N_GROUPS,), F32)]).reshape(ROUTER_ROWS, 1)
    x1, h2, logits_t = _merge(
        x, att, y_t, sga, sgs, mod, p["g_post_mix"].reshape(1, d), p["g_pre_ffn"].reshape(1, d),
        p["w_att_out"].astype(BF16), p["w_glu_val"].astype(BF16), p["w_glu_gate"].astype(BF16),
        p["w_mix_out"].astype(BF16), w_rt, b_rt, tm=512)

    eid, wts, dest, counts = _route(logits_t, tn=1024)
    counts = counts[:, 0]
    pend = jnp.cumsum((counts + MOE_BLOCK - 1) // MOE_BLOCK * MOE_BLOCK)
    n_blocks = -(-(n_tok * 2) // MOE_BLOCK) + N_EXPERTS
    block_start = jnp.arange(n_blocks, dtype=I32) * MOE_BLOCK
    block_e = jnp.minimum(jnp.sum((pend[None, :] <= block_start[:, None]).astype(I32), axis=1), N_EXPERTS - 1)
    tm = 256
    dest_tiles = dest.reshape(2, n_tok // tm, tm).transpose(1, 0, 2).reshape(-1)
    x_rows = _dispatch(h2.reshape(n_tok, d // 2), dest_tiles, counts, n_blocks, tm)
    y_rows = _experts(x_rows, block_e, p["w_exp_gate"].astype(BF16), p["w_exp_up"].astype(BF16),
                      p["w_exp_down"].astype(BF16))
    out = _combine(y_rows, dest_tiles, wts.T, x1.reshape(n_tok, d), mod[:, 5:6, :],
                   p["g_post_ffn"].reshape(1, d), tm)
    return out.reshape(bsz, s, d)


def kernel(x, c, rel_bias, w_ada, b_ada, g_pre_mix, g_post_mix, w_in, w_att_out, ssm_lambda_re, ssm_lambda_im, ssm_log_dt, ssm_b_re, ssm_b_im, ssm_c_re, ssm_c_im, ssm_d, w_glu_val, w_glu_gate, w_mix_out, g_pre_ffn, g_post_ffn, w_router_group, b_router_group, w_router_expert, b_router_expert, w_exp_gate, w_exp_up, w_exp_down):
    layered = dict(
        w_ada=w_ada, b_ada=b_ada, g_pre_mix=g_pre_mix, g_post_mix=g_post_mix, w_in=w_in,
        w_att_out=w_att_out, ssm_lambda_re=ssm_lambda_re, ssm_lambda_im=ssm_lambda_im,
        ssm_log_dt=ssm_log_dt, ssm_b_re=ssm_b_re, ssm_b_im=ssm_b_im, ssm_c_re=ssm_c_re,
        ssm_c_im=ssm_c_im, ssm_d=ssm_d, w_glu_val=w_glu_val, w_glu_gate=w_glu_gate,
        w_mix_out=w_mix_out, g_pre_ffn=g_pre_ffn, g_post_ffn=g_post_ffn,
        w_router_group=w_router_group, b_router_group=b_router_group,
        w_router_expert=w_router_expert, b_router_expert=b_router_expert,
        w_exp_gate=w_exp_gate, w_exp_up=w_exp_up, w_exp_down=w_exp_down)
    depth = w_ada.shape[0]
    bsz, d = c.shape
    bias = _bias_tiles(rel_bias)
    far_bucket = np.unique(_t5_bucket_np(np.arange(MOBA_BLOCK + 1, max(x.shape[1], MOBA_BLOCK + 2))))
    assert far_bucket.size == 1
    bfar = rel_bias[int(far_bucket[0])] * LOG2E
    for l in range(depth):
        p = {name: a[l] for name, a in layered.items()}
        mod = _ada(c, p["w_ada"], p["b_ada"]).reshape(bsz, -1, d)
        x = _layer(x, mod, bias, bfar, p)
    return x
```

```python
import functools
import math

import numpy as np
import jax
import jax.numpy as jnp
from jax import lax
from jax.experimental import pallas as pl
from jax.experimental.pallas import tpu as pltpu
from jax.experimental.pallas import tpu_sc as plsc

F32 = jnp.float32
BF16 = jnp.bfloat16
I32 = jnp.int32

ATT_HEADS = 8
HEAD_DIM = 64
MOBA_BLOCK = 256
MOBA_TOPK = 3
NUM_BUCKETS = 32
MAX_DISTANCE = 128
SSM_GROUP = 16
SSM_STATE = 64
SSM_CHUNK = 128
N_GROUPS = 4
EXPERTS_PER_GROUP = 8
N_EXPERTS = N_GROUPS * EXPERTS_PER_GROUP
MOE_BLOCK = 256
RMS_EPS = 1e-6
NEG_INF = -1e30
LOG2E = math.log2(math.e)
LANES = 128
ROUTER_ROWS = 40
VMEM_LIMIT = 56 * 1024 * 1024
GATHER_UNROLL = 8
SC_CORES = 2
SC_SUBCORES = 16
SC_WINDOW = 128

_NT = (((1,), (1,)), ((), ()))
_NN = (((1,), (0,)), ((), ()))


def _params(sem, vmem=VMEM_LIMIT):
    return pltpu.CompilerParams(dimension_semantics=sem, vmem_limit_bytes=vmem)


def _split_bf16(a):
    hi = a.astype(BF16)
    lo = (a - hi.astype(F32)).astype(BF16)
    return hi, lo


def _dot3(a, b, dims):
    a_hi, a_lo = _split_bf16(a)
    b_hi, b_lo = _split_bf16(b)
    dg = functools.partial(lax.dot_general, dimension_numbers=dims, preferred_element_type=F32)
    return dg(a_hi, b_hi) + (dg(a_hi, b_lo) + dg(a_lo, b_hi))


def _pack_pairs(a):
    w = a.shape[1] // 2
    bits = pltpu.bitcast(a.astype(BF16).astype(F32), jnp.uint32)
    return (bits[:, :w] >> 16) | (bits[:, w:] & jnp.uint32(0xFFFF0000))


def _unpack_pairs(words):
    lo = pltpu.bitcast(words << 16, F32)
    hi = pltpu.bitcast(words & jnp.uint32(0xFFFF0000), F32)
    return lo, hi


def _ada_kernel(c_ref, w_ref, b_ref, o_ref):
    c = c_ref[...]
    ca = c * jax.nn.sigmoid(c)
    o_ref[...] = _dot3(ca, w_ref[...], _NN) + b_ref[...]


def _ada(c, w, b):
    bsz, d = c.shape
    n = w.shape[1]
    tn = 1536
    return pl.pallas_call(
        _ada_kernel,
        out_shape=jax.ShapeDtypeStruct((bsz, n), F32),
        grid=(n // tn,),
        in_specs=[pl.BlockSpec((bsz, d), lambda j: (0, 0)),
                  pl.BlockSpec((d, tn), lambda j: (0, j)),
                  pl.BlockSpec((1, tn), lambda j: (0, j))],
        out_specs=pl.BlockSpec((bsz, tn), lambda j: (0, j)),
        compiler_params=_params(("parallel",)),
        name="ada",
    )(c, w, b.reshape(1, n))


def _t5_bucket_np(dist):
    n = np.maximum(dist, 0)
    max_exact = NUM_BUCKETS // 2
    nf = np.maximum(n, 1).astype(np.float32)
    large = max_exact + (np.log(nf / np.float32(max_exact)) / np.float32(math.log(MAX_DISTANCE / max_exact))
                         * np.float32(NUM_BUCKETS - max_exact)).astype(np.int32)
    large = np.minimum(large, NUM_BUCKETS - 1)
    return np.where(n < max_exact, n, large).astype(np.int32)


def _bias_kernel(rb_ref, bk_ref, o_ref):
    h = pl.program_id(0)
    for t in range(2):
        bk = bk_ref[t]
        acc = jnp.where(bk < 0, NEG_INF, 0.0).astype(F32)
        for b in range(NUM_BUCKETS):
            acc = jnp.where(bk == b, rb_ref[b, h] * LOG2E, acc)
        o_ref[0, t] = acc


def _bias_tiles(rel_bias):
    qi = np.arange(MOBA_BLOCK)[:, None]
    kj = np.arange(MOBA_BLOCK)[None, :]
    own = np.where(qi >= kj, _t5_bucket_np(qi - kj), -1)
    adj = _t5_bucket_np(qi - kj + MOBA_BLOCK)
    buckets = jnp.asarray(np.stack([own, adj]).astype(np.int32))
    return pl.pallas_call(
        _bias_kernel,
        out_shape=jax.ShapeDtypeStruct((ATT_HEADS, 2, MOBA_BLOCK, MOBA_BLOCK), F32),
        grid=(ATT_HEADS,),
        in_specs=[pl.BlockSpec(memory_space=pltpu.SMEM),
                  pl.BlockSpec((2, MOBA_BLOCK, MOBA_BLOCK), lambda h: (0, 0, 0))],
        out_specs=pl.BlockSpec((1, 2, MOBA_BLOCK, MOBA_BLOCK), lambda h: (h, 0, 0, 0)),
        compiler_params=_params(("parallel",)),
        name="t5_bias",
    )(rel_bias, buckets)


def _rms(x, gain):
    ms = jnp.mean(x * x, axis=-1, keepdims=True)
    return x * lax.rsqrt(ms + RMS_EPS) * gain


def _inproj_kernel(x_ref, mod_ref, g_ref, w_ref, wut_ref,
                   q_ref, k_ref, v_ref, ut_ref, sga_ref, sgs_ref):
    aw = q_ref.shape[2]
    d = x_ref.shape[2]
    x = x_ref[0]
    h = _rms(x, g_ref[...]) * (1.0 + mod_ref[0, 1:2, :]) + mod_ref[0, 0:1, :]
    hb = h.astype(BF16)

    def proj(lo, width):
        return jnp.dot(hb, w_ref[:, lo:lo + width], preferred_element_type=F32)

    q_ref[0] = (proj(0, aw) * (HEAD_DIM ** -0.5 * LOG2E)).astype(BF16)
    k_ref[0] = proj(aw, aw).astype(BF16)
    v_ref[0] = proj(2 * aw, aw).astype(BF16)
    ut_ref[0] = lax.dot_general(wut_ref[...], hb, _NT, preferred_element_type=F32)
    sga_ref[0] = jax.nn.sigmoid(proj(3 * aw, d)).astype(BF16)
    sgs_ref[0] = jax.nn.sigmoid(proj(3 * aw + d, d)).astype(BF16)


def _inproj(x, mod, gain, w_rest, w_ut, tm):
    bsz, s, d = x.shape
    aw = ATT_HEADS * HEAD_DIM
    sw = w_ut.shape[0]
    tok = lambda width, dt: jax.ShapeDtypeStruct((bsz, s, width), dt)
    tspec = lambda width: pl.BlockSpec((1, tm, width), lambda b, i: (b, i, 0))
    return pl.pallas_call(
        _inproj_kernel,
        out_shape=(tok(aw, BF16), tok(aw, BF16), tok(aw, BF16),
                   jax.ShapeDtypeStruct((bsz, sw, s), F32), tok(d, BF16), tok(d, BF16)),
        grid=(bsz, s // tm),
        in_specs=[tspec(d),
                  pl.BlockSpec((1, mod.shape[1], d), lambda b, i: (b, 0, 0)),
                  pl.BlockSpec((1, d), lambda b, i: (0, 0)),
                  pl.BlockSpec(w_rest.shape, lambda b, i: (0, 0)),
                  pl.BlockSpec(w_ut.shape, lambda b, i: (0, 0))],
        out_specs=(tspec(aw), tspec(aw), tspec(aw),
                   pl.BlockSpec((1, sw, tm), lambda b, i: (b, 0, i)), tspec(d), tspec(d)),
        compiler_params=_params(("parallel", "parallel")),
        name="inproj",
    )(x, mod, gain, w_rest, w_ut)


def _moba_kernel(bfar_ref, q_ref, k_ref, v_ref, bias_ref, o_ref,
                 kmh_sc, kml_sc, qaug_sc, m_sc, acc_sc):
    blk = MOBA_BLOCK
    i = pl.program_id(1)
    s = k_ref.shape[1]
    nb = s // blk
    nbp = kmh_sc.shape[0]
    npair = q_ref.shape[2] // LANES
    lane = lax.broadcasted_iota(I32, (blk, LANES), 1)
    low_half = lane < HEAD_DIM

    @pl.when(i == 0)
    def _():
        r = lax.broadcasted_iota(I32, (nbp, s), 0)
        c = lax.broadcasted_iota(I32, (nbp, s), 1)
        avg = jnp.where((c >= r * blk) & (c < (r + 1) * blk), 1.0 / blk, 0.0).astype(BF16)
        km = jnp.dot(avg, k_ref[0], preferred_element_type=F32)
        hi, lo = _split_bf16(km)
        kmh_sc[...] = hi
        kml_sc[...] = lo

    row = lax.broadcasted_iota(I32, (nbp, blk), 0)
    half_k = lax.broadcasted_iota(I32, (nbp, LANES), 1) < HEAD_DIM
    for pr in range(npair):
        q2 = q_ref[0, :, pr * LANES:(pr + 1) * LANES]
        kmh = kmh_sc[:, pr * LANES:(pr + 1) * LANES]
        kml = kml_sc[:, pr * LANES:(pr + 1) * LANES]
        for hh in range(2):
            mine = half_k if hh == 0 else jnp.logical_not(half_k)
            gate = (lax.dot_general(jnp.where(mine, kmh, jnp.zeros_like(kmh)), q2, _NT,
                                    preferred_element_type=F32)
                    + lax.dot_general(jnp.where(mine, kml, jnp.zeros_like(kml)), q2, _NT,
                                      preferred_element_type=F32))
            gate = jnp.where(row < i, gate, NEG_INF)
            cnt = jnp.zeros((nbp, blk), F32)
            for m in range(nb):
                other = jnp.broadcast_to(gate[m:m + 1, :], (nbp, blk))
                tie = jnp.where(row > m, 1.0, 0.0)
                cnt = cnt + jnp.where(other > gate, 1.0, 0.0) + jnp.where(other == gate, tie, 0.0)
            chosen = jnp.where(row < i, cnt, float(MOBA_TOPK)) < float(MOBA_TOPK)
            keep_t = jnp.where(row == i, 0.0, jnp.where(chosen, 0.0, NEG_INF))
            keep_t = jnp.concatenate([keep_t, jnp.full((LANES - nbp, blk), NEG_INF, F32)], axis=0)
            qh = jnp.where(low_half if hh == 0 else jnp.logical_not(low_half), q2, jnp.zeros_like(q2))
            qaug_sc[2 * pr + hh] = jnp.concatenate([qh, keep_t.T.astype(BF16)], axis=1)

    one_hi = jnp.where(lane == HEAD_DIM, 1.0, 0.0).astype(BF16)
    one_lo = jnp.where(lane == 0, 1.0, 0.0).astype(BF16)

    def tile(j, bias_of_head, scalar_bias, first):
        start = pl.multiple_of(j * blk, blk)
        onehot = jnp.where(lane == j, 1.0, 0.0).astype(BF16)
        for pr in range(npair):
            kj = k_ref[0, pl.ds(start, blk), pr * LANES:(pr + 1) * LANES]
            vj = v_ref[0, pl.ds(start, blk), pr * LANES:(pr + 1) * LANES]
            kaug = jnp.concatenate([kj, onehot], axis=1)
            vaug = (jnp.where(low_half, vj, one_hi), jnp.where(low_half, one_lo, vj))
            for hh in range(2):
                h = 2 * pr + hh
                sc = lax.dot_general(qaug_sc[h], kaug, _NT, preferred_element_type=F32)
                bias = bias_of_head(h)
                if scalar_bias:
                    m_cur = jnp.max(sc, axis=1, keepdims=True) + bias
                else:
                    sc = sc + bias
                    m_cur = jnp.max(sc, axis=1, keepdims=True)
                if first:
                    m_new = jnp.broadcast_to(m_cur, (blk, LANES))
                else:
                    m_prev = m_sc[h]
                    m_new = jnp.maximum(m_prev, m_cur)
                shift = m_new - bias if scalar_bias else m_new
                pexp = jnp.exp2(sc - jnp.concatenate([shift, shift], axis=1))
                pv = jnp.dot(pexp.astype(BF16), vaug[hh], preferred_element_type=F32)
                if first:
                    acc_sc[h] = pv
                else:
                    acc_sc[h] = jnp.exp2(m_prev - m_new) * acc_sc[h] + pv
                m_sc[h] = m_new

    tile(i, lambda h: bias_ref[h, 0], False, True)

    @pl.when(i >= 1)
    def _():
        tile(i - 1, lambda h: bias_ref[h, 1], False, False)

    def far(j, carry):
        tile(j, lambda h: bfar_ref[h], True, False)
        return carry

    lax.fori_loop(0, jnp.maximum(i - 1, 0), far, 0)

    for pr in range(npair):
        acc_e = acc_sc[2 * pr]
        acc_o = acc_sc[2 * pr + 1]
        out = jnp.where(low_half, acc_e / acc_e[:, HEAD_DIM:HEAD_DIM + 1], acc_o / acc_o[:, 0:1])
        o_ref[0, :, pr * LANES:(pr + 1) * LANES] = out.astype(BF16)


def _moba(q, k, v, bias, bfar):
    bsz, s, aw = q.shape
    blk = MOBA_BLOCK
    nb = s // blk
    nbp = -(-nb // 16) * 16
    heads = bias.shape[0]
    grid_spec = pltpu.PrefetchScalarGridSpec(
        num_scalar_prefetch=1,
        grid=(bsz, nb),
        in_specs=[pl.BlockSpec((1, blk, aw), lambda b, i, _: (b, i, 0)),
                  pl.BlockSpec((1, s, aw), lambda b, i, _: (b, 0, 0)),
                  pl.BlockSpec((1, s, aw), lambda b, i, _: (b, 0, 0)),
                  pl.BlockSpec(bias.shape, lambda b, i, _: (0, 0, 0, 0))],
        out_specs=pl.BlockSpec((1, blk, aw), lambda b, i, _: (b, i, 0)),
        scratch_shapes=[pltpu.VMEM((nbp, aw), BF16), pltpu.VMEM((nbp, aw), BF16),
                        pltpu.VMEM((heads, blk, 2 * LANES), BF16),
                        pltpu.VMEM((heads, blk, LANES), F32), pltpu.VMEM((heads, blk, LANES), F32)],
    )
    return pl.pallas_call(
        _moba_kernel,
        out_shape=jax.ShapeDtypeStruct((bsz, s, aw), BF16),
        grid_spec=grid_spec,
        compiler_params=_params(("parallel", "arbitrary")),
        name="moba",
    )(bfar, q, k, v, bias)


def _ssm_tables(lam_re, lam_im, log_dt, b_re, b_im, c_re, c_im, d_skip):
    L = SSM_CHUNK
    g = lam_re.shape[0]
    dt = jnp.exp(log_dt)[:, None]
    lr, li = lam_re, lam_im
    mag = jnp.exp(lr * dt)
    ab_re, ab_im = mag * jnp.cos(li * dt), mag * jnp.sin(li * dt)
    den = lr * lr + li * li
    nr, ni = ab_re - 1.0, ab_im
    f_re, f_im = (nr * lr + ni * li) / den, (ni * lr - nr * li) / den
    bb_re = f_re[..., None] * b_re - f_im[..., None] * b_im
    bb_im = f_re[..., None] * b_im + f_im[..., None] * b_re
    n = jnp.arange(L + 1, dtype=F32)[:, None, None]
    pw_mag = jnp.exp(n * (lr * dt)[None])
    pw_re, pw_im = pw_mag * jnp.cos(n * (li * dt)[None]), pw_mag * jnp.sin(n * (li * dt)[None])
    ca_re = c_re[None] * pw_re[:L, :, None, :] - c_im[None] * pw_im[:L, :, None, :]
    ca_im = c_re[None] * pw_im[:L, :, None, :] + c_im[None] * pw_re[:L, :, None, :]
    hp = lax.Precision.HIGHEST
    kern = (jnp.einsum("ngcp,gpd->gdcn", ca_re, bb_re, precision=hp)
            - jnp.einsum("ngcp,gpd->gdcn", ca_im, bb_im, precision=hp))
    skip = jnp.asarray(np.eye(SSM_GROUP, dtype=np.float32))[None, :, :, None] * d_skip.reshape(g, 1, SSM_GROUP, 1)
    kern = kern + skip * jnp.asarray((np.arange(L) == 0).astype(np.float32))
    kern = kern.reshape(g, SSM_GROUP * SSM_GROUP, L)
    rev_re, rev_im = pw_re[L - 1::-1][:L], pw_im[L - 1::-1][:L]
    win_re = rev_re[..., None] * bb_re[None] - rev_im[..., None] * bb_im[None]
    win_im = rev_re[..., None] * bb_im[None] + rev_im[..., None] * bb_re[None]
    w_in = jnp.concatenate([win_re, win_im], axis=2)
    w_in = w_in.transpose(1, 3, 0, 2).reshape(g, SSM_GROUP * L, 2 * SSM_STATE)
    fw_re, fw_im = pw_re[1:], pw_im[1:]
    wo_re = c_re[None] * fw_re[:, :, None, :] - c_im[None] * fw_im[:, :, None, :]
    wo_im = -(c_re[None] * fw_im[:, :, None, :] + c_im[None] * fw_re[:, :, None, :])
    w_out = jnp.concatenate([wo_re, wo_im], axis=3)
    w_out = w_out.transpose(1, 3, 2, 0).reshape(g, 2 * SSM_STATE, SSM_GROUP * L)
    a_chunk = jnp.stack([jnp.concatenate([pw_re[L], pw_re[L]], axis=-1),
                         jnp.concatenate([-pw_im[L], pw_im[L]], axis=-1)], axis=1)
    return kern, w_in.astype(BF16), w_out.astype(BF16), a_chunk


def _ssm_kernel(u_ref, kern_ref, win_ref, wout_ref, ac_ref, y_ref, toep_sc):
    bsz, ng, s = u_ref.shape
    L = SSM_CHUNK
    nc = s // L
    s_ix = lax.broadcasted_iota(I32, (L, L), 0)
    t_ix = lax.broadcasted_iota(I32, (L, L), 1)

    def build(cp, carry):
        r0 = pl.multiple_of(cp * L, L)
        for c in range(ng):
            vec = kern_ref[0, pl.ds(cp * ng + c, 1), :]
            lagged = pltpu.roll(jnp.broadcast_to(vec, (L, L)), 0, 1, stride=1, stride_axis=0)
            toep_sc[pl.ds(r0, L), c * L:(c + 1) * L] = jnp.where(t_ix >= s_ix, lagged, 0.0).astype(BF16)
        return carry

    lax.fori_loop(0, ng, build, 0)

    u = jnp.concatenate(
        [jnp.concatenate([u_ref[:, c, ch * L:(ch + 1) * L] for c in range(ng)], axis=1) for ch in range(nc)],
        axis=0).astype(BF16)
    st = jnp.dot(u, win_ref[0], preferred_element_type=F32)
    a1 = ac_ref[0, 0:1, :]
    a2 = ac_ref[0, 1:2, :]
    state = jnp.zeros((bsz, 2 * SSM_STATE), F32)
    prevs = []
    for ch in range(nc):
        prevs.append(state)
        state = a1 * state + a2 * pltpu.roll(state, SSM_STATE, 1) + st[ch * bsz:(ch + 1) * bsz]
    prev = jnp.concatenate(prevs, axis=0).astype(BF16)
    y = (jnp.dot(u, toep_sc[...], preferred_element_type=F32)
         + jnp.dot(prev, wout_ref[0], preferred_element_type=F32))
    for ch in range(nc):
        for c in range(ng):
            y_ref[:, c, ch * L:(ch + 1) * L] = y[ch * bsz:(ch + 1) * bsz, c * L:(c + 1) * L]


def _ssm(u_t, kern, w_in, w_out, a_chunk):
    bsz, sw, s = u_t.shape
    g = sw // SSM_GROUP
    blk = pl.BlockSpec((bsz, SSM_GROUP, s), lambda j: (0, j, 0))
    per_group = lambda a: pl.BlockSpec((1,) + a.shape[1:], lambda j: (j, 0, 0))
    return pl.pallas_call(
        _ssm_kernel,
        out_shape=jax.ShapeDtypeStruct(u_t.shape, F32),
        grid=(g,),
        in_specs=[blk, per_group(kern), per_group(w_in), per_group(w_out), per_group(a_chunk)],
        out_specs=blk,
        scratch_shapes=[pltpu.VMEM((SSM_GROUP * SSM_CHUNK, SSM_GROUP * SSM_CHUNK), BF16)],
        compiler_params=_params(("parallel",)),
        name="ssm",
    )(u_t, kern, w_in, w_out, a_chunk)


def _gelu_tanh(x):
    return 0.5 * x * (1.0 + jnp.tanh(math.sqrt(2.0 / math.pi) * (x + 0.044715 * (x * x * x))))


def _merge_kernel(x_ref, att_ref, yt_ref, sga_ref, sgs_ref, mod_ref, gpost_ref, gpre_ref,
                  wao_ref, wgv_ref, wgg_ref, wmo_ref, wrt_ref, brt_ref,
                  x1_ref, h2_ref, lt_ref):
    a_br = jnp.dot(att_ref[0], wao_ref[...], preferred_element_type=F32)
    z = _gelu_tanh(yt_ref[0]).T.astype(BF16)
    s_br = (jnp.dot(z, wgv_ref[...], preferred_element_type=F32)
            * jax.nn.sigmoid(jnp.dot(z, wgg_ref[...], preferred_element_type=F32)))
    merged = sga_ref[0].astype(F32) * a_br + sgs_ref[0].astype(F32) * s_br
    mix = jnp.dot(merged.astype(BF16), wmo_ref[...], preferred_element_type=F32)
    x1 = x_ref[0] + mod_ref[0, 2:3, :] * _rms(mix, gpost_ref[...])
    x1_ref[0] = x1
    h2 = _rms(x1, gpre_ref[...]) * (1.0 + mod_ref[0, 4:5, :]) + mod_ref[0, 3:4, :]
    h2_ref[0] = _pack_pairs(h2)
    lt_ref[...] = _dot3(wrt_ref[...], h2, _NT) + brt_ref[...]


def _merge(x, att, yt, sga, sgs, mod, g_post, g_pre, w_ao, w_gv, w_gg, w_mo, w_rt, b_rt, tm):
    bsz, s, d = x.shape
    aw = att.shape[2]
    sw = yt.shape[1]
    nt = s // tm
    tspec = lambda width: pl.BlockSpec((1, tm, width), lambda b, i: (b, i, 0))
    full = lambda a: pl.BlockSpec(a.shape, lambda b, i: (0,) * a.ndim)
    return pl.pallas_call(
        _merge_kernel,
        out_shape=(jax.ShapeDtypeStruct((bsz, s, d), F32), jax.ShapeDtypeStruct((bsz, s, d // 2), jnp.uint32),
                   jax.ShapeDtypeStruct((ROUTER_ROWS, bsz * s), F32)),
        grid=(bsz, nt),
        in_specs=[tspec(d), tspec(aw), pl.BlockSpec((1, sw, tm), lambda b, i: (b, 0, i)),
                  tspec(d), tspec(d),
                  pl.BlockSpec((1, mod.shape[1], d), lambda b, i: (b, 0, 0)),
                  full(g_post), full(g_pre), full(w_ao), full(w_gv), full(w_gg), full(w_mo),
                  full(w_rt), full(b_rt)],
        out_specs=(tspec(d), tspec(d // 2), pl.BlockSpec((ROUTER_ROWS, tm), lambda b, i: (0, b * nt + i))),
        compiler_params=_params(("parallel", "parallel")),
        name="merge",
    )(x, att, yt, sga, sgs, mod, g_post, g_pre, w_ao, w_gv, w_gg, w_mo, w_rt, b_rt)


def _route_kernel(lt_ref, eid_ref, wt_ref, dest_ref, cnt_ref, base_sc, pstart_sc):
    tn = lt_ref.shape[1]
    epg = EXPERTS_PER_GROUP
    ph = pl.program_id(0)
    step = pl.program_id(1)

    @pl.when((ph == 0) & (step == 0))
    def _():
        base_sc[...] = jnp.zeros_like(base_sc)

    @pl.when((ph == 1) & (step == 0))
    def _():
        total = base_sc[...]
        cnt_ref[...] = total.astype(I32)
        padded = jnp.floor((total + (MOE_BLOCK - 1.0)) * (1.0 / MOE_BLOCK)) * MOE_BLOCK
        r = lax.broadcasted_iota(I32, (N_EXPERTS, N_EXPERTS), 0)
        c = lax.broadcasted_iota(I32, (N_EXPERTS, N_EXPERTS), 1)
        before = jnp.where(c < r, 1.0, 0.0)
        pstart_sc[...] = _dot3(before, padded, _NN)
        base_sc[...] = jnp.zeros_like(base_sc)

    row8 = lax.broadcasted_iota(I32, (epg, tn), 0)
    gl = lt_ref[N_EXPERTS:N_EXPERTS + epg, :]
    gl = jnp.where(row8 < N_GROUPS, gl, -jnp.inf)
    gmax = jnp.max(gl, axis=0, keepdims=True)
    gidx = jnp.min(jnp.where(gl == gmax, row8, epg), axis=0, keepdims=True)

    el = jnp.zeros((epg, tn), F32)
    for g in range(N_GROUPS):
        el = jnp.where(gidx == g, lt_ref[g * epg:(g + 1) * epg, :], el)
    m1 = jnp.max(el, axis=0, keepdims=True)
    i1 = jnp.min(jnp.where(el == m1, row8, epg), axis=0, keepdims=True)
    el2 = jnp.where(row8 == i1, -jnp.inf, el)
    m2 = jnp.max(el2, axis=0, keepdims=True)
    i2 = jnp.min(jnp.where(el2 == m2, row8, epg), axis=0, keepdims=True)
    e1 = gidx * epg + i1
    e2 = gidx * epg + i2

    row32 = lax.broadcasted_iota(I32, (N_EXPERTS, tn), 0)
    hit1 = row32 == e1
    hit2 = row32 == e2
    onehot = jnp.where(hit1, 1.0, jnp.where(hit2, 1.0, 0.0))

    @pl.when(ph == 1)
    def _():
        g_p = 1.0 / jnp.sum(jnp.exp(gl - gmax), axis=0, keepdims=True)
        zsum = jnp.sum(jnp.exp(el - m1), axis=0, keepdims=True)
        p1 = 1.0 / zsum
        p2 = jnp.exp(m2 - m1) / zsum
        sr = lax.broadcasted_iota(I32, (tn, tn), 0)
        tc = lax.broadcasted_iota(I32, (tn, tn), 1)
        earlier = jnp.where(sr < tc, 1.0, 0.0).astype(BF16)
        place = (jnp.dot(onehot.astype(BF16), earlier, preferred_element_type=F32)
                 + base_sc[:, 0:1] + pstart_sc[:, 0:1])
        d1 = jnp.sum(jnp.where(hit1, place, 0.0), axis=0, keepdims=True)
        d2 = jnp.sum(jnp.where(hit2, place, 0.0), axis=0, keepdims=True)
        eid_ref[...] = jnp.concatenate([e1, e2], axis=0)
        wt_ref[...] = jnp.concatenate([g_p * p1 / (p1 + p2), g_p * p2 / (p1 + p2)], axis=0)
        dest_ref[...] = jnp.concatenate([d1, d2], axis=0).astype(I32)

    base_sc[...] = base_sc[...] + jnp.sum(onehot, axis=1, keepdims=True)


def _route(logits_t, tn):
    n = logits_t.shape[1]
    two = lambda dt: jax.ShapeDtypeStruct((2, n), dt)
    tspec = pl.BlockSpec((2, tn), lambda ph, i: (0, i * ph))
    return pl.pallas_call(
        _route_kernel,
        out_shape=(two(I32), two(F32), two(I32), jax.ShapeDtypeStruct((N_EXPERTS, LANES), I32)),
        grid=(2, n // tn),
        in_specs=[pl.BlockSpec((ROUTER_ROWS, tn), lambda ph, i: (0, i))],
        out_specs=(tspec, tspec, tspec, pl.BlockSpec((N_EXPERTS, LANES), lambda ph, i: (0, 0))),
        scratch_shapes=[pltpu.VMEM((N_EXPERTS, LANES), F32), pltpu.VMEM((N_EXPERTS, LANES), F32)],
        compiler_params=_params(("arbitrary", "arbitrary")),
        name="route",
    )(logits_t)


def _row_copy(src_hbm, idx_ref, buf, sem, slot, g, u):
    return pltpu.make_async_copy(src_hbm.at[pl.ds(idx_ref[0, 0, g * GATHER_UNROLL + u], 1), :],
                                 buf.at[slot, g, pl.ds(u, 1), :], sem.at[slot])


def _start_rows(src_hbm, idx_ref, buf, sem, slot):
    def body(g, c):
        for u in range(GATHER_UNROLL):
            _row_copy(src_hbm, idx_ref, buf, sem, slot, g, u).start(priority=u % 2)
        return c
    lax.fori_loop(0, buf.shape[1], body, 0)


def _wait_rows(src_hbm, idx_ref, buf, sem, slot):
    def body(g, c):
        for u in range(GATHER_UNROLL):
            _row_copy(src_hbm, idx_ref, buf, sem, slot, g, u).wait()
        return c
    lax.fori_loop(0, buf.shape[1], body, 0)


def _gathered_rows(src_hbm, idx_ref, idx_next_ref, buf, sem):
    i = pl.program_id(0)
    slot = i % 2

    @pl.when(i == 0)
    def _():
        _start_rows(src_hbm, idx_ref, buf, sem, 0)

    @pl.when(i + 1 < pl.num_programs(0))
    def _():
        _start_rows(src_hbm, idx_next_ref, buf, sem, 1 - slot)

    _wait_rows(src_hbm, idx_ref, buf, sem, slot)
    return buf[slot].reshape(buf.shape[1] * GATHER_UNROLL, buf.shape[3])


def _dispatch_copy(stage, x_hbm, idx_ref, sem, slot, g, u, k):
    tm = idx_ref.shape[2] // 2
    return pltpu.make_async_copy(stage.at[slot, g, pl.ds(u, 1), :],
                                 x_hbm.at[pl.ds(idx_ref[0, 0, k * tm + g * GATHER_UNROLL + u], 1), :],
                                 sem.at[slot])


def _dispatch_rows(stage, x_hbm, idx_ref, sem, slot, wait):
    def body(g, c):
        for u in range(GATHER_UNROLL):
            for k in range(2):
                cp = _dispatch_copy(stage, x_hbm, idx_ref, sem, slot, g, u, k)
                if wait:
                    cp.wait()
                else:
                    cp.start(priority=k)
        return c
    lax.fori_loop(0, stage.shape[1], body, 0)


def _dispatch_kernel(cnt_ref, idx_ref, idxp_ref, h_hbm, x_hbm, stage, zero_sc, lsem, sem, zsem):
    i = pl.program_id(0)
    n_steps = pl.num_programs(0)
    n_blocks = x_hbm.shape[0] // MOE_BLOCK
    slot = i % 2

    def load(step, into):
        return pltpu.make_async_copy(h_hbm.at[step], stage.at[into], lsem.at[into])

    def zero_block(blk_ix):
        return pltpu.make_async_copy(zero_sc, x_hbm.at[pl.ds(blk_ix * MOE_BLOCK, MOE_BLOCK), :], zsem.at[0])

    @pl.when(i == 0)
    def _():
        load(0, 0).start()
        zero_sc[...] = jnp.zeros_like(zero_sc)
        used = jnp.int32(0)
        for e in range(N_EXPERTS):
            cnt = cnt_ref[e]
            used = used + lax.shift_right_logical(cnt + (MOE_BLOCK - 1), 8)
            ragged = jnp.bitwise_and(cnt, MOE_BLOCK - 1) != 0

            @pl.when(ragged)
            def _():
                zero_block(used - 1).start()

            @pl.when(ragged)
            def _():
                zero_block(used - 1).wait()

        def tail(b, c):
            zero_block(b).start()
            zero_block(b).wait()
            return c
        lax.fori_loop(used, n_blocks, tail, 0)

    @pl.when(i >= 1)
    def _():
        _dispatch_rows(stage, x_hbm, idxp_ref, sem, 1 - slot, True)

    @pl.when(i + 1 < n_steps)
    def _():
        load(i + 1, 1 - slot).start()

    load(i, slot).wait()
    _dispatch_rows(stage, x_hbm, idx_ref, sem, slot, False)

    @pl.when(i == n_steps - 1)
    def _():
        _dispatch_rows(stage, x_hbm, idx_ref, sem, slot, True)


def _dispatch(h2_rows, dest_tiles, counts, n_blocks, tm):
    n_tok, w = h2_rows.shape
    nt = n_tok // tm
    idx = dest_tiles.reshape(nt, 1, 2 * tm)
    smem_blk = lambda f: pl.BlockSpec((1, 1, 2 * tm), f, memory_space=pltpu.SMEM)
    grid_spec = pltpu.PrefetchScalarGridSpec(
        num_scalar_prefetch=1,
        grid=(nt,),
        in_specs=[smem_blk(lambda i, c: (i, 0, 0)),
                  smem_blk(lambda i, c: (jnp.maximum(i - 1, 0), 0, 0)),
                  pl.BlockSpec(memory_space=pl.ANY)],
        out_specs=pl.BlockSpec(memory_space=pl.ANY),
        scratch_shapes=[pltpu.VMEM((2, tm // GATHER_UNROLL, GATHER_UNROLL, w), jnp.uint32),
                        pltpu.VMEM((MOE_BLOCK, w), jnp.uint32),
                        pltpu.SemaphoreType.DMA((2,)), pltpu.SemaphoreType.DMA((2,)),
                        pltpu.SemaphoreType.DMA((1,))],
    )
    return pl.pallas_call(
        _dispatch_kernel,
        out_shape=jax.ShapeDtypeStruct((n_blocks * MOE_BLOCK, w), jnp.uint32),
        grid_spec=grid_spec,
        compiler_params=_params(("arbitrary",)),
        name="dispatch",
    )(counts, idx, idx, h2_rows.reshape(nt, tm // GATHER_UNROLL, GATHER_UNROLL, w))


def _expert_kernel(be_ref, x_ref, wg_ref, wu_ref, wd_ref, y_ref):
    lo, hi = _unpack_pairs(x_ref[...])
    xb = jnp.concatenate([lo.astype(BF16), hi.astype(BF16)], axis=1)
    gate = jnp.dot(xb, wg_ref[0], preferred_element_type=F32)
    up = jnp.dot(xb, wu_ref[0], preferred_element_type=F32)
    hid = (gate * jax.nn.sigmoid(gate) * up).astype(BF16)
    y_ref[...] = _pack_pairs(jnp.dot(hid, wd_ref[0], preferred_element_type=F32))


def _experts(x_rows, block_e, w_gate, w_up, w_down):
    n_blocks = block_e.shape[0]
    d = w_gate.shape[1]
    de = w_gate.shape[2]
    grid_spec = pltpu.PrefetchScalarGridSpec(
        num_scalar_prefetch=1,
        grid=(n_blocks,),
        in_specs=[pl.BlockSpec((MOE_BLOCK, d // 2), lambda i, be: (i, 0)),
                  pl.BlockSpec((1, d, de), lambda i, be: (be[i], 0, 0)),
                  pl.BlockSpec((1, d, de), lambda i, be: (be[i], 0, 0)),
                  pl.BlockSpec((1, de, d), lambda i, be: (be[i], 0, 0))],
        out_specs=pl.BlockSpec((MOE_BLOCK, d // 2), lambda i, be: (i, 0)),
    )
    return pl.pallas_call(
        _expert_kernel,
        out_shape=jax.ShapeDtypeStruct((n_blocks * MOE_BLOCK, d // 2), jnp.uint32),
        grid_spec=grid_spec,
        compiler_params=_params(("arbitrary",)),
        name="experts",
    )(block_e, x_rows, w_gate, w_up, w_down)


def _sc_gather(table, idx):
    n_idx = idx.shape[0]
    w = table.shape[1]
    per_worker = n_idx // (SC_CORES * SC_SUBCORES)
    mesh = plsc.VectorSubcoreMesh(core_axis_name="c", subcore_axis_name="s",
                                  num_cores=SC_CORES, num_subcores=SC_SUBCORES)

    @functools.partial(
        pl.kernel, mesh=mesh, out_type=jax.ShapeDtypeStruct((n_idx, w), table.dtype),
        scratch_types=[pltpu.VMEM((SC_WINDOW,), I32), pltpu.VMEM((SC_WINDOW, w), table.dtype),
                       pltpu.SemaphoreType.DMA],
        name="sc_gather")
    def gather(table_hbm, idx_hbm, out_hbm, idx_v, rows_v, sem):
        worker = lax.axis_index("s") * SC_CORES + lax.axis_index("c")

        @pl.loop(0, per_worker // SC_WINDOW)
        def _(chunk):
            off = worker * per_worker + chunk * SC_WINDOW
            pltpu.sync_copy(idx_hbm.at[pl.ds(off, SC_WINDOW)], idx_v)
            pltpu.async_copy(table_hbm.at[idx_v], rows_v, sem).wait()
            pltpu.sync_copy(rows_v, out_hbm.at[pl.ds(off, SC_WINDOW)])

    return gather(table, idx)


def _combine_kernel(y0_ref, y1_ref, x1_ref, wt_ref, g2_ref, gain_ref, o_ref):
    y0 = jnp.concatenate(_unpack_pairs(pltpu.bitcast(y0_ref[...], jnp.uint32)), axis=1)
    y1 = jnp.concatenate(_unpack_pairs(pltpu.bitcast(y1_ref[...], jnp.uint32)), axis=1)
    f = wt_ref[:, 0:1] * y0 + wt_ref[:, 1:2] * y1
    o_ref[...] = x1_ref[...] + g2_ref[0] * _rms(f, gain_ref[...])


def _combine(y2, wts, x1, gate2, gain, tm):
    n, d = x1.shape
    s = n // gate2.shape[0]
    nt = n // tm
    return pl.pallas_call(
        _combine_kernel,
        out_shape=jax.ShapeDtypeStruct((n, d), F32),
        grid=(nt,),
        in_specs=[pl.BlockSpec((tm, d // 2), lambda i: (i, 0)),
                  pl.BlockSpec((tm, d // 2), lambda i: (nt + i, 0)),
                  pl.BlockSpec((tm, d), lambda i: (i, 0)),
                  pl.BlockSpec((tm, 2), lambda i: (i, 0)),
                  pl.BlockSpec((1, 1, d), lambda i: ((i * tm) // s, 0, 0)),
                  pl.BlockSpec((1, d), lambda i: (0, 0))],
        out_specs=pl.BlockSpec((tm, d), lambda i: (i, 0)),
        compiler_params=_params(("parallel",)),
        name="combine",
    )(y2, y2, x1, wts, gate2, gain)


def _layer(x, mod, bias, bfar, p):
    bsz, s, d = x.shape
    n_tok = bsz * s
    aw = ATT_HEADS * HEAD_DIM
    sw = p["ssm_d"].shape[0]

    w_in = p["w_in"]
    w_rest = jnp.concatenate([w_in[:, :3 * aw], w_in[:, 3 * aw + sw:]], axis=1).astype(BF16)
    w_ut = w_in[:, 3 * aw:3 * aw + sw].T.astype(BF16)
    q, k, v, u_t, sga, sgs = _inproj(x, mod, p["g_pre_mix"].reshape(1, d), w_rest, w_ut, tm=512)
    att = _moba(q, k, v, bias, bfar)
    tables = _ssm_tables(p["ssm_lambda_re"], p["ssm_lambda_im"], p["ssm_log_dt"], p["ssm_b_re"],
                         p["ssm_b_im"], p["ssm_c_re"], p["ssm_c_im"], p["ssm_d"])
    y_t = _ssm(u_t, *tables)

    w_rt = jnp.concatenate([p["w_router_expert"].T, p["w_router_group"].T,
                            jnp.zeros((ROUTER_ROWS - N_EXPERTS - N_GROUPS, d), F32)], axis=0)
    b_rt = jnp.concatenate([p["b_router_expert"], p["b_router_group"],
                            jnp.zeros((ROUTER_ROWS - N_EXPERTS - N_GROUPS,), F32)]).reshape(ROUTER_ROWS, 1)
    x1, h2, logits_t = _merge(
        x, att, y_t, sga, sgs, mod, p["g_post_mix"].reshape(1, d), p["g_pre_ffn"].reshape(1, d),
        p["w_att_out"].astype(BF16), p["w_glu_val"].astype(BF16), p["w_glu_gate"].astype(BF16),
        p["w_mix_out"].astype(BF16), w_rt, b_rt, tm=512)

    eid, wts, dest, counts = _route(logits_t, tn=1024)
    counts = counts[:, 0]
    pend = jnp.cumsum((counts + MOE_BLOCK - 1) // MOE_BLOCK * MOE_BLOCK)
    n_blocks = -(-(n_tok * 2) // MOE_BLOCK) + N_EXPERTS
    block_start = jnp.arange(n_blocks, dtype=I32) * MOE_BLOCK
    block_e = jnp.minimum(jnp.sum((pend[None, :] <= block_start[:, None]).astype(I32), axis=1), N_EXPERTS - 1)
    tm = 256
    dest_tiles = dest.reshape(2, n_tok // tm, tm).transpose(1, 0, 2).reshape(-1)
    x_rows = _dispatch(h2.reshape(n_tok, d // 2), dest_tiles, counts, n_blocks, tm)
    y_rows = _experts(x_rows, block_e, p["w_exp_gate"].astype(BF16), p["w_exp_up"].astype(BF16),
                      p["w_exp_down"].astype(BF16))
    y2 = _sc_gather(lax.bitcast_convert_type(y_rows, I32), dest.reshape(-1))
    out = _combine(y2, wts.T, x1.reshape(n_tok, d), mod[:, 5:6, :], p["g_post_ffn"].reshape(1, d), tm=512)
    return out.reshape(bsz, s, d)


def kernel(x, c, rel_bias, w_ada, b_ada, g_pre_mix, g_post_mix, w_in, w_att_out, ssm_lambda_re, ssm_lambda_im, ssm_log_dt, ssm_b_re, ssm_b_im, ssm_c_re, ssm_c_im, ssm_d, w_glu_val, w_glu_gate, w_mix_out, g_pre_ffn, g_post_ffn, w_router_group, b_router_group, w_router_expert, b_router_expert, w_exp_gate, w_exp_up, w_exp_down):
    layered = dict(
        w_ada=w_ada, b_ada=b_ada, g_pre_mix=g_pre_mix, g_post_mix=g_post_mix, w_in=w_in,
        w_att_out=w_att_out, ssm_lambda_re=ssm_lambda_re, ssm_lambda_im=ssm_lambda_im,
        ssm_log_dt=ssm_log_dt, ssm_b_re=ssm_b_re, ssm_b_im=ssm_b_im, ssm_c_re=ssm_c_re,
        ssm_c_im=ssm_c_im, ssm_d=ssm_d, w_glu_val=w_glu_val, w_glu_gate=w_glu_gate,
        w_mix_out=w_mix_out, g_pre_ffn=g_pre_ffn, g_post_ffn=g_post_ffn,
        w_router_group=w_router_group, b_router_group=b_router_group,
        w_router_expert=w_router_expert, b_router_expert=b_router_expert,
        w_exp_gate=w_exp_gate, w_exp_up=w_exp_up, w_exp_down=w_exp_down)
    depth = w_ada.shape[0]
    bsz, d = c.shape
    bias = _bias_tiles(rel_bias)
    far_bucket = np.unique(_t5_bucket_np(np.arange(MOBA_BLOCK + 1, max(x.shape[1], MOBA_BLOCK + 2))))
    assert far_bucket.size == 1
    bfar = rel_bias[int(far_bucket[0])] * LOG2E
    for l in range(depth):
        p = {name: a[l] for name, a in layered.items()}
        mod = _ada(c, p["w_ada"], p["b_ada"]).reshape(bsz, -1, d)
        x = _layer(x, mod, bias, bfar, p)
    return x
```

```python
import functools
import math

import numpy as np
import jax
import jax.numpy as jnp
from jax import lax
from jax.experimental import pallas as pl
from jax.experimental.pallas import tpu as pltpu
from jax.experimental.pallas import tpu_sc as plsc

F32 = jnp.float32
BF16 = jnp.bfloat16
I32 = jnp.int32

ATT_HEADS = 8
HEAD_DIM = 64
MOBA_BLOCK = 256
MOBA_TOPK = 3
NUM_BUCKETS = 32
MAX_DISTANCE = 128
SSM_GROUP = 16
SSM_STATE = 64
SSM_CHUNK = 128
N_GROUPS = 4
EXPERTS_PER_GROUP = 8
N_EXPERTS = N_GROUPS * EXPERTS_PER_GROUP
MOE_BLOCK = 256
RMS_EPS = 1e-6
NEG_INF = -1e30
LOG2E = math.log2(math.e)
LANES = 128
ROUTER_ROWS = 40
VMEM_LIMIT = 56 * 1024 * 1024
SC_CORES = 2
SC_SUBCORES = 16
SC_WINDOW = 128

_NT = (((1,), (1,)), ((), ()))
_NN = (((1,), (0,)), ((), ()))


def _params(sem, vmem=VMEM_LIMIT):
    return pltpu.CompilerParams(dimension_semantics=sem, vmem_limit_bytes=vmem)


def _split_bf16(a):
    hi = a.astype(BF16)
    lo = (a - hi.astype(F32)).astype(BF16)
    return hi, lo


def _dot3(a, b, dims):
    a_hi, a_lo = _split_bf16(a)
    b_hi, b_lo = _split_bf16(b)
    dg = functools.partial(lax.dot_general, dimension_numbers=dims, preferred_element_type=F32)
    return dg(a_hi, b_hi) + (dg(a_hi, b_lo) + dg(a_lo, b_hi))


def _pack_pairs(a):
    w = a.shape[1] // 2
    bits = pltpu.bitcast(a.astype(BF16).astype(F32), jnp.uint32)
    return pltpu.bitcast((bits[:, :w] >> 16) | (bits[:, w:] & jnp.uint32(0xFFFF0000)), I32)


def _unpack_pairs(packed):
    words = pltpu.bitcast(packed, jnp.uint32)
    lo = pltpu.bitcast(words << 16, F32)
    hi = pltpu.bitcast(words & jnp.uint32(0xFFFF0000), F32)
    return lo, hi


def _ada_kernel(c_ref, w_ref, b_ref, o_ref):
    c = c_ref[...]
    ca = c * jax.nn.sigmoid(c)
    o_ref[...] = _dot3(ca, w_ref[...], _NN) + b_ref[...]


def _ada(c, w, b):
    bsz, d = c.shape
    n = w.shape[1]
    tn = 1536
    return pl.pallas_call(
        _ada_kernel,
        out_shape=jax.ShapeDtypeStruct((bsz, n), F32),
        grid=(n // tn,),
        in_specs=[pl.BlockSpec((bsz, d), lambda j: (0, 0)),
                  pl.BlockSpec((d, tn), lambda j: (0, j)),
                  pl.BlockSpec((1, tn), lambda j: (0, j))],
        out_specs=pl.BlockSpec((bsz, tn), lambda j: (0, j)),
        compiler_params=_params(("parallel",)),
        name="ada",
    )(c, w, b.reshape(1, n))


def _t5_bucket_np(dist):
    n = np.maximum(dist, 0)
    max_exact = NUM_BUCKETS // 2
    nf = np.maximum(n, 1).astype(np.float32)
    large = max_exact + (np.log(nf / np.float32(max_exact)) / np.float32(math.log(MAX_DISTANCE / max_exact))
                         * np.float32(NUM_BUCKETS - max_exact)).astype(np.int32)
    large = np.minimum(large, NUM_BUCKETS - 1)
    return np.where(n < max_exact, n, large).astype(np.int32)


def _bias_kernel(rb_ref, bk_ref, o_ref):
    h = pl.program_id(0)
    for t in range(2):
        bk = bk_ref[t]
        acc = jnp.where(bk < 0, NEG_INF, 0.0).astype(F32)
        for b in range(NUM_BUCKETS):
            acc = jnp.where(bk == b, rb_ref[b, h] * LOG2E, acc)
        o_ref[0, t] = acc


def _bias_tiles(rel_bias):
    qi = np.arange(MOBA_BLOCK)[:, None]
    kj = np.arange(MOBA_BLOCK)[None, :]
    own = np.where(qi >= kj, _t5_bucket_np(qi - kj), -1)
    adj = _t5_bucket_np(qi - kj + MOBA_BLOCK)
    buckets = jnp.asarray(np.stack([own, adj]).astype(np.int32))
    return pl.pallas_call(
        _bias_kernel,
        out_shape=jax.ShapeDtypeStruct((ATT_HEADS, 2, MOBA_BLOCK, MOBA_BLOCK), F32),
        grid=(ATT_HEADS,),
        in_specs=[pl.BlockSpec(memory_space=pltpu.SMEM),
                  pl.BlockSpec((2, MOBA_BLOCK, MOBA_BLOCK), lambda h: (0, 0, 0))],
        out_specs=pl.BlockSpec((1, 2, MOBA_BLOCK, MOBA_BLOCK), lambda h: (h, 0, 0, 0)),
        compiler_params=_params(("parallel",)),
        name="t5_bias",
    )(rel_bias, buckets)


def _rms(x, gain):
    ms = jnp.mean(x * x, axis=-1, keepdims=True)
    return x * lax.rsqrt(ms + RMS_EPS) * gain


def _inproj_kernel(x_ref, mod_ref, g_ref, w_ref, wut_ref,
                   q_ref, k_ref, v_ref, ut_ref, sga_ref, sgs_ref):
    aw = q_ref.shape[2]
    d = x_ref.shape[2]
    x = x_ref[0]
    h = _rms(x, g_ref[...]) * (1.0 + mod_ref[0, 1:2, :]) + mod_ref[0, 0:1, :]
    hb = h.astype(BF16)

    def proj(lo, width):
        return jnp.dot(hb, w_ref[:, lo:lo + width], preferred_element_type=F32)

    q_ref[0] = (proj(0, aw) * (HEAD_DIM ** -0.5 * LOG2E)).astype(BF16)
    k_ref[0] = proj(aw, aw).astype(BF16)
    v_ref[0] = proj(2 * aw, aw).astype(BF16)
    ut_ref[0] = lax.dot_general(wut_ref[...], hb, _NT, preferred_element_type=F32)
    sga_ref[0] = jax.nn.sigmoid(proj(3 * aw, d)).astype(BF16)
    sgs_ref[0] = jax.nn.sigmoid(proj(3 * aw + d, d)).astype(BF16)


def _inproj(x, mod, gain, w_rest, w_ut, tm):
    bsz, s, d = x.shape
    aw = ATT_HEADS * HEAD_DIM
    sw = w_ut.shape[0]
    tok = lambda width, dt: jax.ShapeDtypeStruct((bsz, s, width), dt)
    tspec = lambda width: pl.BlockSpec((1, tm, width), lambda b, i: (b, i, 0))
    return pl.pallas_call(
        _inproj_kernel,
        out_shape=(tok(aw, BF16), tok(aw, BF16), tok(aw, BF16),
                   jax.ShapeDtypeStruct((bsz, sw, s), F32), tok(d, BF16), tok(d, BF16)),
        grid=(bsz, s // tm),
        in_specs=[tspec(d),
                  pl.BlockSpec((1, mod.shape[1], d), lambda b, i: (b, 0, 0)),
                  pl.BlockSpec((1, d), lambda b, i: (0, 0)),
                  pl.BlockSpec(w_rest.shape, lambda b, i: (0, 0)),
                  pl.BlockSpec(w_ut.shape, lambda b, i: (0, 0))],
        out_specs=(tspec(aw), tspec(aw), tspec(aw),
                   pl.BlockSpec((1, sw, tm), lambda b, i: (b, 0, i)), tspec(d), tspec(d)),
        compiler_params=_params(("parallel", "parallel")),
        name="inproj",
    )(x, mod, gain, w_rest, w_ut)


def _moba_kernel(bfar_ref, q_ref, k_ref, v_ref, bias_ref, o_ref,
                 kmh_sc, kml_sc, qaug_sc, m_sc, acc_sc):
    blk = MOBA_BLOCK
    i = pl.program_id(1)
    s = k_ref.shape[1]
    nb = s // blk
    nbp = kmh_sc.shape[0]
    npair = q_ref.shape[2] // LANES
    lane = lax.broadcasted_iota(I32, (blk, LANES), 1)
    low_half = lane < HEAD_DIM

    @pl.when(i == 0)
    def _():
        r = lax.broadcasted_iota(I32, (nbp, s), 0)
        c = lax.broadcasted_iota(I32, (nbp, s), 1)
        avg = jnp.where((c >= r * blk) & (c < (r + 1) * blk), 1.0 / blk, 0.0).astype(BF16)
        km = jnp.dot(avg, k_ref[0], preferred_element_type=F32)
        hi, lo = _split_bf16(km)
        kmh_sc[...] = hi
        kml_sc[...] = lo

    row = lax.broadcasted_iota(I32, (nbp, blk), 0)
    half_k = lax.broadcasted_iota(I32, (nbp, LANES), 1) < HEAD_DIM
    for pr in range(npair):
        q2 = q_ref[0, :, pr * LANES:(pr + 1) * LANES]
        kmh = kmh_sc[:, pr * LANES:(pr + 1) * LANES]
        kml = kml_sc[:, pr * LANES:(pr + 1) * LANES]
        for hh in range(2):
            mine = half_k if hh == 0 else jnp.logical_not(half_k)
            gate = (lax.dot_general(jnp.where(mine, kmh, jnp.zeros_like(kmh)), q2, _NT,
                                    preferred_element_type=F32)
                    + lax.dot_general(jnp.where(mine, kml, jnp.zeros_like(kml)), q2, _NT,
                                      preferred_element_type=F32))
            gate = jnp.where(row < i, gate, NEG_INF)
            cnt = jnp.zeros((nbp, blk), F32)
            for m in range(nb):
                other = jnp.broadcast_to(gate[m:m + 1, :], (nbp, blk))
                tie = jnp.where(row > m, 1.0, 0.0)
                cnt = cnt + jnp.where(other > gate, 1.0, 0.0) + jnp.where(other == gate, tie, 0.0)
            chosen = jnp.where(row < i, cnt, float(MOBA_TOPK)) < float(MOBA_TOPK)
            keep_t = jnp.where(row == i, 0.0, jnp.where(chosen, 0.0, NEG_INF))
            keep_t = jnp.concatenate([keep_t, jnp.full((LANES - nbp, blk), NEG_INF, F32)], axis=0)
            qh = jnp.where(low_half if hh == 0 else jnp.logical_not(low_half), q2, jnp.zeros_like(q2))
            qaug_sc[2 * pr + hh] = jnp.concatenate([qh, keep_t.T.astype(BF16)], axis=1)

    one_hi = jnp.where(lane == HEAD_DIM, 1.0, 0.0).astype(BF16)
    one_lo = jnp.where(lane == 0, 1.0, 0.0).astype(BF16)

    def tile(j, bias_of_head, scalar_bias, first):
        start = pl.multiple_of(j * blk, blk)
        onehot = jnp.where(lane == j, 1.0, 0.0).astype(BF16)
        for pr in range(npair):
            kj = k_ref[0, pl.ds(start, blk), pr * LANES:(pr + 1) * LANES]
            vj = v_ref[0, pl.ds(start, blk), pr * LANES:(pr + 1) * LANES]
            kaug = jnp.concatenate([kj, onehot], axis=1)
            vaug = (jnp.where(low_half, vj, one_hi), jnp.where(low_half, one_lo, vj))
            for hh in range(2):
                h = 2 * pr + hh
                sc = lax.dot_general(qaug_sc[h], kaug, _NT, preferred_element_type=F32)
                bias = bias_of_head(h)
                if scalar_bias:
                    m_cur = jnp.max(sc, axis=1, keepdims=True) + bias
                else:
                    sc = sc + bias
                    m_cur = jnp.max(sc, axis=1, keepdims=True)
                if first:
                    m_new = jnp.broadcast_to(m_cur, (blk, LANES))
                else:
                    m_prev = m_sc[h]
                    m_new = jnp.maximum(m_prev, m_cur)
                shift = m_new - bias if scalar_bias else m_new
                pexp = jnp.exp2(sc - jnp.concatenate([shift, shift], axis=1))
                pv = jnp.dot(pexp.astype(BF16), vaug[hh], preferred_element_type=F32)
                if first:
                    acc_sc[h] = pv
                else:
                    acc_sc[h] = jnp.exp2(m_prev - m_new) * acc_sc[h] + pv
                m_sc[h] = m_new

    tile(i, lambda h: bias_ref[h, 0], False, True)

    @pl.when(i >= 1)
    def _():
        tile(i - 1, lambda h: bias_ref[h, 1], False, False)

    def far(j, carry):
        tile(j, lambda h: bfar_ref[h], True, False)
        return carry

    lax.fori_loop(0, jnp.maximum(i - 1, 0), far, 0)

    for pr in range(npair):
        acc_e = acc_sc[2 * pr]
        acc_o = acc_sc[2 * pr + 1]
        out = jnp.where(low_half, acc_e / acc_e[:, HEAD_DIM:HEAD_DIM + 1], acc_o / acc_o[:, 0:1])
        o_ref[0, :, pr * LANES:(pr + 1) * LANES] = out.astype(BF16)


def _moba(q, k, v, bias, bfar):
    bsz, s, aw = q.shape
    blk = MOBA_BLOCK
    nb = s // blk
    nbp = -(-nb // 16) * 16
    heads = bias.shape[0]
    grid_spec = pltpu.PrefetchScalarGridSpec(
        num_scalar_prefetch=1,
        grid=(bsz, nb),
        in_specs=[pl.BlockSpec((1, blk, aw), lambda b, i, _: (b, i, 0)),
                  pl.BlockSpec((1, s, aw), lambda b, i, _: (b, 0, 0)),
                  pl.BlockSpec((1, s, aw), lambda b, i, _: (b, 0, 0)),
                  pl.BlockSpec(bias.shape, lambda b, i, _: (0, 0, 0, 0))],
        out_specs=pl.BlockSpec((1, blk, aw), lambda b, i, _: (b, i, 0)),
        scratch_shapes=[pltpu.VMEM((nbp, aw), BF16), pltpu.VMEM((nbp, aw), BF16),
                        pltpu.VMEM((heads, blk, 2 * LANES), BF16),
                        pltpu.VMEM((heads, blk, LANES), F32), pltpu.VMEM((heads, blk, LANES), F32)],
    )
    return pl.pallas_call(
        _moba_kernel,
        out_shape=jax.ShapeDtypeStruct((bsz, s, aw), BF16),
        grid_spec=grid_spec,
        compiler_params=_params(("parallel", "arbitrary")),
        name="moba",
    )(bfar, q, k, v, bias)


def _ssm_tables(lam_re, lam_im, log_dt, b_re, b_im, c_re, c_im, d_skip):
    L = SSM_CHUNK
    g = lam_re.shape[0]
    dt = jnp.exp(log_dt)[:, None]
    lr, li = lam_re, lam_im
    mag = jnp.exp(lr * dt)
    ab_re, ab_im = mag * jnp.cos(li * dt), mag * jnp.sin(li * dt)
    den = lr * lr + li * li
    nr, ni = ab_re - 1.0, ab_im
    f_re, f_im = (nr * lr + ni * li) / den, (ni * lr - nr * li) / den
    bb_re = f_re[..., None] * b_re - f_im[..., None] * b_im
    bb_im = f_re[..., None] * b_im + f_im[..., None] * b_re
    n = jnp.arange(L + 1, dtype=F32)[:, None, None]
    pw_mag = jnp.exp(n * (lr * dt)[None])
    pw_re, pw_im = pw_mag * jnp.cos(n * (li * dt)[None]), pw_mag * jnp.sin(n * (li * dt)[None])
    ca_re = c_re[None] * pw_re[:L, :, None, :] - c_im[None] * pw_im[:L, :, None, :]
    ca_im = c_re[None] * pw_im[:L, :, None, :] + c_im[None] * pw_re[:L, :, None, :]
    hp = lax.Precision.HIGHEST
    kern = (jnp.einsum("ngcp,gpd->gdcn", ca_re, bb_re, precision=hp)
            - jnp.einsum("ngcp,gpd->gdcn", ca_im, bb_im, precision=hp))
    skip = jnp.asarray(np.eye(SSM_GROUP, dtype=np.float32))[None, :, :, None] * d_skip.reshape(g, 1, SSM_GROUP, 1)
    kern = kern + skip * jnp.asarray((np.arange(L) == 0).astype(np.float32))
    kern = kern.reshape(g, SSM_GROUP * SSM_GROUP, L)
    rev_re, rev_im = pw_re[L - 1::-1][:L], pw_im[L - 1::-1][:L]
    win_re = rev_re[..., None] * bb_re[None] - rev_im[..., None] * bb_im[None]
    win_im = rev_re[..., None] * bb_im[None] + rev_im[..., None] * bb_re[None]
    w_in = jnp.concatenate([win_re, win_im], axis=2)
    w_in = w_in.transpose(1, 3, 0, 2).reshape(g, SSM_GROUP * L, 2 * SSM_STATE)
    fw_re, fw_im = pw_re[1:], pw_im[1:]
    wo_re = c_re[None] * fw_re[:, :, None, :] - c_im[None] * fw_im[:, :, None, :]
    wo_im = -(c_re[None] * fw_im[:, :, None, :] + c_im[None] * fw_re[:, :, None, :])
    w_out = jnp.concatenate([wo_re, wo_im], axis=3)
    w_out = w_out.transpose(1, 3, 2, 0).reshape(g, 2 * SSM_STATE, SSM_GROUP * L)
    a_chunk = jnp.stack([jnp.concatenate([pw_re[L], pw_re[L]], axis=-1),
                         jnp.concatenate([-pw_im[L], pw_im[L]], axis=-1)], axis=1)
    return kern, w_in.astype(BF16), w_out.astype(BF16), a_chunk


def _ssm_kernel(u_ref, kern_ref, win_ref, wout_ref, ac_ref, y_ref, toep_sc):
    bsz, ng, s = u_ref.shape
    L = SSM_CHUNK
    nc = s // L
    s_ix = lax.broadcasted_iota(I32, (L, L), 0)
    t_ix = lax.broadcasted_iota(I32, (L, L), 1)

    def build(cp, carry):
        r0 = pl.multiple_of(cp * L, L)
        for c in range(ng):
            vec = kern_ref[0, pl.ds(cp * ng + c, 1), :]
            lagged = pltpu.roll(jnp.broadcast_to(vec, (L, L)), 0, 1, stride=1, stride_axis=0)
            toep_sc[pl.ds(r0, L), c * L:(c + 1) * L] = jnp.where(t_ix >= s_ix, lagged, 0.0).astype(BF16)
        return carry

    lax.fori_loop(0, ng, build, 0)

    u = jnp.concatenate(
        [jnp.concatenate([u_ref[:, c, ch * L:(ch + 1) * L] for c in range(ng)], axis=1) for ch in range(nc)],
        axis=0).astype(BF16)
    st = jnp.dot(u, win_ref[0], preferred_element_type=F32)
    a1 = ac_ref[0, 0:1, :]
    a2 = ac_ref[0, 1:2, :]
    state = jnp.zeros((bsz, 2 * SSM_STATE), F32)
    prevs = []
    for ch in range(nc):
        prevs.append(state)
        state = a1 * state + a2 * pltpu.roll(state, SSM_STATE, 1) + st[ch * bsz:(ch + 1) * bsz]
    prev = jnp.concatenate(prevs, axis=0).astype(BF16)
    y = (jnp.dot(u, toep_sc[...], preferred_element_type=F32)
         + jnp.dot(prev, wout_ref[0], preferred_element_type=F32))
    for ch in range(nc):
        for c in range(ng):
            y_ref[:, c, ch * L:(ch + 1) * L] = y[ch * bsz:(ch + 1) * bsz, c * L:(c + 1) * L]


def _ssm(u_t, kern, w_in, w_out, a_chunk):
    bsz, sw, s = u_t.shape
    g = sw // SSM_GROUP
    blk = pl.BlockSpec((bsz, SSM_GROUP, s), lambda j: (0, j, 0))
    per_group = lambda a: pl.BlockSpec((1,) + a.shape[1:], lambda j: (j, 0, 0))
    return pl.pallas_call(
        _ssm_kernel,
        out_shape=jax.ShapeDtypeStruct(u_t.shape, F32),
        grid=(g,),
        in_specs=[blk, per_group(kern), per_group(w_in), per_group(w_out), per_group(a_chunk)],
        out_specs=blk,
        scratch_shapes=[pltpu.VMEM((SSM_GROUP * SSM_CHUNK, SSM_GROUP * SSM_CHUNK), BF16)],
        compiler_params=_params(("parallel",)),
        name="ssm",
    )(u_t, kern, w_in, w_out, a_chunk)


def _gelu_tanh(x):
    return 0.5 * x * (1.0 + jnp.tanh(math.sqrt(2.0 / math.pi) * (x + 0.044715 * (x * x * x))))


def _merge_kernel(x_ref, att_ref, yt_ref, sga_ref, sgs_ref, mod_ref, gpost_ref, gpre_ref,
                  wao_ref, wgv_ref, wgg_ref, wmo_ref, wrt_ref, brt_ref,
                  x1_ref, h2_ref, lt_ref):
    a_br = jnp.dot(att_ref[0], wao_ref[...], preferred_element_type=F32)
    z = _gelu_tanh(yt_ref[0]).T.astype(BF16)
    s_br = (jnp.dot(z, wgv_ref[...], preferred_element_type=F32)
            * jax.nn.sigmoid(jnp.dot(z, wgg_ref[...], preferred_element_type=F32)))
    merged = sga_ref[0].astype(F32) * a_br + sgs_ref[0].astype(F32) * s_br
    mix = jnp.dot(merged.astype(BF16), wmo_ref[...], preferred_element_type=F32)
    x1 = x_ref[0] + mod_ref[0, 2:3, :] * _rms(mix, gpost_ref[...])
    x1_ref[0] = x1
    h2 = _rms(x1, gpre_ref[...]) * (1.0 + mod_ref[0, 4:5, :]) + mod_ref[0, 3:4, :]
    h2_ref[0] = _pack_pairs(h2)
    lt_ref[...] = _dot3(wrt_ref[...], h2, _NT) + brt_ref[...]


def _merge(x, att, yt, sga, sgs, mod, g_post, g_pre, w_ao, w_gv, w_gg, w_mo, w_rt, b_rt, tm):
    bsz, s, d = x.shape
    aw = att.shape[2]
    sw = yt.shape[1]
    nt = s // tm
    tspec = lambda width: pl.BlockSpec((1, tm, width), lambda b, i: (b, i, 0))
    full = lambda a: pl.BlockSpec(a.shape, lambda b, i: (0,) * a.ndim)
    return pl.pallas_call(
        _merge_kernel,
        out_shape=(jax.ShapeDtypeStruct((bsz, s, d), F32), jax.ShapeDtypeStruct((bsz, s, d // 2), I32),
                   jax.ShapeDtypeStruct((ROUTER_ROWS, bsz * s), F32)),
        grid=(bsz, nt),
        in_specs=[tspec(d), tspec(aw), pl.BlockSpec((1, sw, tm), lambda b, i: (b, 0, i)),
                  tspec(d), tspec(d),
                  pl.BlockSpec((1, mod.shape[1], d), lambda b, i: (b, 0, 0)),
                  full(g_post), full(g_pre), full(w_ao), full(w_gv), full(w_gg), full(w_mo),
                  full(w_rt), full(b_rt)],
        out_specs=(tspec(d), tspec(d // 2), pl.BlockSpec((ROUTER_ROWS, tm), lambda b, i: (0, b * nt + i))),
        compiler_params=_params(("parallel", "parallel")),
        name="merge",
    )(x, att, yt, sga, sgs, mod, g_post, g_pre, w_ao, w_gv, w_gg, w_mo, w_rt, b_rt)


def _route_kernel(lt_ref, eid_ref, wt_ref, dest_ref, cnt_ref, base_sc, pstart_sc):
    tn = lt_ref.shape[1]
    epg = EXPERTS_PER_GROUP
    ph = pl.program_id(0)
    step = pl.program_id(1)

    @pl.when((ph == 0) & (step == 0))
    def _():
        base_sc[...] = jnp.zeros_like(base_sc)

    @pl.when((ph == 1) & (step == 0))
    def _():
        total = base_sc[...]
        cnt_ref[...] = total.astype(I32)
        padded = jnp.floor((total + (MOE_BLOCK - 1.0)) * (1.0 / MOE_BLOCK)) * MOE_BLOCK
        r = lax.broadcasted_iota(I32, (N_EXPERTS, N_EXPERTS), 0)
        c = lax.broadcasted_iota(I32, (N_EXPERTS, N_EXPERTS), 1)
        before = jnp.where(c < r, 1.0, 0.0)
        pstart_sc[...] = _dot3(before, padded, _NN)
        base_sc[...] = jnp.zeros_like(base_sc)

    row8 = lax.broadcasted_iota(I32, (epg, tn), 0)
    gl = lt_ref[N_EXPERTS:N_EXPERTS + epg, :]
    gl = jnp.where(row8 < N_GROUPS, gl, -jnp.inf)
    gmax = jnp.max(gl, axis=0, keepdims=True)
    gidx = jnp.min(jnp.where(gl == gmax, row8, epg), axis=0, keepdims=True)

    el = jnp.zeros((epg, tn), F32)
    for g in range(N_GROUPS):
        el = jnp.where(gidx == g, lt_ref[g * epg:(g + 1) * epg, :], el)
    m1 = jnp.max(el, axis=0, keepdims=True)
    i1 = jnp.min(jnp.where(el == m1, row8, epg), axis=0, keepdims=True)
    el2 = jnp.where(row8 == i1, -jnp.inf, el)
    m2 = jnp.max(el2, axis=0, keepdims=True)
    i2 = jnp.min(jnp.where(el2 == m2, row8, epg), axis=0, keepdims=True)
    e1 = gidx * epg + i1
    e2 = gidx * epg + i2

    row32 = lax.broadcasted_iota(I32, (N_EXPERTS, tn), 0)
    hit1 = row32 == e1
    hit2 = row32 == e2
    onehot = jnp.where(hit1, 1.0, jnp.where(hit2, 1.0, 0.0))

    @pl.when(ph == 1)
    def _():
        g_p = 1.0 / jnp.sum(jnp.exp(gl - gmax), axis=0, keepdims=True)
        zsum = jnp.sum(jnp.exp(el - m1), axis=0, keepdims=True)
        p1 = 1.0 / zsum
        p2 = jnp.exp(m2 - m1) / zsum
        sr = lax.broadcasted_iota(I32, (tn, tn), 0)
        tc = lax.broadcasted_iota(I32, (tn, tn), 1)
        earlier = jnp.where(sr < tc, 1.0, 0.0).astype(BF16)
        place = (jnp.dot(onehot.astype(BF16), earlier, preferred_element_type=F32)
                 + base_sc[:, 0:1] + pstart_sc[:, 0:1])
        d1 = jnp.sum(jnp.where(hit1, place, 0.0), axis=0, keepdims=True)
        d2 = jnp.sum(jnp.where(hit2, place, 0.0), axis=0, keepdims=True)
        eid_ref[...] = jnp.concatenate([e1, e2], axis=0)
        wt_ref[...] = jnp.concatenate([g_p * p1 / (p1 + p2), g_p * p2 / (p1 + p2)], axis=0)
        dest_ref[...] = jnp.concatenate([d1, d2], axis=0).astype(I32)

    base_sc[...] = base_sc[...] + jnp.sum(onehot, axis=1, keepdims=True)


def _route(logits_t, tn):
    n = logits_t.shape[1]
    two = lambda dt: jax.ShapeDtypeStruct((2, n), dt)
    tspec = pl.BlockSpec((2, tn), lambda ph, i: (0, i * ph))
    return pl.pallas_call(
        _route_kernel,
        out_shape=(two(I32), two(F32), two(I32), jax.ShapeDtypeStruct((N_EXPERTS, LANES), I32)),
        grid=(2, n // tn),
        in_specs=[pl.BlockSpec((ROUTER_ROWS, tn), lambda ph, i: (0, i))],
        out_specs=(tspec, tspec, tspec, pl.BlockSpec((N_EXPERTS, LANES), lambda ph, i: (0, 0))),
        scratch_shapes=[pltpu.VMEM((N_EXPERTS, LANES), F32), pltpu.VMEM((N_EXPERTS, LANES), F32)],
        compiler_params=_params(("arbitrary", "arbitrary")),
        name="route",
    )(logits_t)


def _sc_mesh():
    return plsc.VectorSubcoreMesh(core_axis_name="c", subcore_axis_name="s",
                                  num_cores=SC_CORES, num_subcores=SC_SUBCORES)


def _sc_worker_offset(per_worker):
    return (lax.axis_index("s") * SC_CORES + lax.axis_index("c")) * per_worker


def _sc_dispatch(rows, dest0, dest1, n_rows):
    n_tok, w = rows.shape
    per_worker = n_tok // (SC_CORES * SC_SUBCORES)
    x_ref = jax.new_ref(jnp.zeros((n_rows, w), rows.dtype))

    @functools.partial(
        pl.kernel, mesh=_sc_mesh(), out_type=(),
        scratch_types=[pltpu.VMEM((SC_WINDOW,), I32), pltpu.VMEM((SC_WINDOW,), I32),
                       pltpu.VMEM((SC_WINDOW, w), rows.dtype), pltpu.SemaphoreType.DMA],
        name="sc_dispatch")
    def scatter(rows_hbm, dest0_hbm, dest1_hbm, x_hbm, idx0_v, idx1_v, rows_v, sem):
        base = _sc_worker_offset(per_worker)

        @pl.loop(0, per_worker // SC_WINDOW)
        def _(chunk):
            off = base + chunk * SC_WINDOW
            pltpu.sync_copy(rows_hbm.at[pl.ds(off, SC_WINDOW)], rows_v)
            pltpu.sync_copy(dest0_hbm.at[pl.ds(off, SC_WINDOW)], idx0_v)
            pltpu.sync_copy(dest1_hbm.at[pl.ds(off, SC_WINDOW)], idx1_v)
            pltpu.async_copy(rows_v, x_hbm.at[idx0_v], sem).wait()
            pltpu.async_copy(rows_v, x_hbm.at[idx1_v], sem).wait()

    scatter(rows, dest0, dest1, x_ref)
    return jax.freeze(x_ref)


def _sc_gather(table, idx):
    n_idx = idx.shape[0]
    w = table.shape[1]
    per_worker = n_idx // (SC_CORES * SC_SUBCORES)

    @functools.partial(
        pl.kernel, mesh=_sc_mesh(), out_type=jax.ShapeDtypeStruct((n_idx, w), table.dtype),
        scratch_types=[pltpu.VMEM((SC_WINDOW,), I32), pltpu.VMEM((SC_WINDOW, w), table.dtype),
                       pltpu.SemaphoreType.DMA],
        name="sc_gather")
    def gather(table_hbm, idx_hbm, out_hbm, idx_v, rows_v, sem):
        base = _sc_worker_offset(per_worker)

        @pl.loop(0, per_worker // SC_WINDOW)
        def _(chunk):
            off = base + chunk * SC_WINDOW
            pltpu.sync_copy(idx_hbm.at[pl.ds(off, SC_WINDOW)], idx_v)
            pltpu.async_copy(table_hbm.at[idx_v], rows_v, sem).wait()
            pltpu.sync_copy(rows_v, out_hbm.at[pl.ds(off, SC_WINDOW)])

    return gather(table, idx)


def _expert_kernel(be_ref, x_ref, wg_ref, wu_ref, wd_ref, y_ref):
    lo, hi = _unpack_pairs(x_ref[...])
    xb = jnp.concatenate([lo.astype(BF16), hi.astype(BF16)], axis=1)
    gate = jnp.dot(xb, wg_ref[0], preferred_element_type=F32)
    up = jnp.dot(xb, wu_ref[0], preferred_element_type=F32)
    hid = (gate * jax.nn.sigmoid(gate) * up).astype(BF16)
    y_ref[...] = _pack_pairs(jnp.dot(hid, wd_ref[0], preferred_element_type=F32))


def _experts(x_rows, block_e, w_gate, w_up, w_down):
    n_blocks = block_e.shape[0]
    d = w_gate.shape[1]
    de = w_gate.shape[2]
    grid_spec = pltpu.PrefetchScalarGridSpec(
        num_scalar_prefetch=1,
        grid=(n_blocks,),
        in_specs=[pl.BlockSpec((MOE_BLOCK, d // 2), lambda i, be: (i, 0)),
                  pl.BlockSpec((1, d, de), lambda i, be: (be[i], 0, 0)),
                  pl.BlockSpec((1, d, de), lambda i, be: (be[i], 0, 0)),
                  pl.BlockSpec((1, de, d), lambda i, be: (be[i], 0, 0))],
        out_specs=pl.BlockSpec((MOE_BLOCK, d // 2), lambda i, be: (i, 0)),
    )
    return pl.pallas_call(
        _expert_kernel,
        out_shape=jax.ShapeDtypeStruct((n_blocks * MOE_BLOCK, d // 2), I32),
        grid_spec=grid_spec,
        compiler_params=_params(("arbitrary",)),
        name="experts",
    )(block_e, x_rows, w_gate, w_up, w_down)


def _combine_kernel(y0_ref, y1_ref, x1_ref, wt_ref, g2_ref, gain_ref, o_ref):
    y0 = jnp.concatenate(_unpack_pairs(y0_ref[...]), axis=1)
    y1 = jnp.concatenate(_unpack_pairs(y1_ref[...]), axis=1)
    f = wt_ref[:, 0:1] * y0 + wt_ref[:, 1:2] * y1
    o_ref[...] = x1_ref[...] + g2_ref[0] * _rms(f, gain_ref[...])


def _combine(y2, wts, x1, gate2, gain, tm):
    n, d = x1.shape
    s = n // gate2.shape[0]
    nt = n // tm
    return pl.pallas_call(
        _combine_kernel,
        out_shape=jax.ShapeDtypeStruct((n, d), F32),
        grid=(nt,),
        in_specs=[pl.BlockSpec((tm, d // 2), lambda i: (i, 0)),
                  pl.BlockSpec((tm, d // 2), lambda i: (nt + i, 0)),
                  pl.BlockSpec((tm, d), lambda i: (i, 0)),
                  pl.BlockSpec((tm, 2), lambda i: (i, 0)),
                  pl.BlockSpec((1, 1, d), lambda i: ((i * tm) // s, 0, 0)),
                  pl.BlockSpec((1, d), lambda i: (0, 0))],
        out_specs=pl.BlockSpec((tm, d), lambda i: (i, 0)),
        compiler_params=_params(("parallel",)),
        name="combine",
    )(y2, y2, x1, wts, gate2, gain)


def _layer(x, mod, bias, bfar, p):
    bsz, s, d = x.shape
    n_tok = bsz * s
    aw = ATT_HEADS * HEAD_DIM
    sw = p["ssm_d"].shape[0]

    w_in = p["w_in"]
    w_rest = jnp.concatenate([w_in[:, :3 * aw], w_in[:, 3 * aw + sw:]], axis=1).astype(BF16)
    w_ut = w_in[:, 3 * aw:3 * aw + sw].T.astype(BF16)
    q, k, v, u_t, sga, sgs = _inproj(x, mod, p["g_pre_mix"].reshape(1, d), w_rest, w_ut, tm=512)
    att = _moba(q, k, v, bias, bfar)
    tables = _ssm_tables(p["ssm_lambda_re"], p["ssm_lambda_im"], p["ssm_log_dt"], p["ssm_b_re"],
                         p["ssm_b_im"], p["ssm_c_re"], p["ssm_c_im"], p["ssm_d"])
    y_t = _ssm(u_t, *tables)

    w_rt = jnp.concatenate([p["w_router_expert"].T, p["w_router_group"].T,
                            jnp.zeros((ROUTER_ROWS - N_EXPERTS - N_GROUPS, d), F32)], axis=0)
    b_rt = jnp.concatenate([p["b_router_expert"], p["b_router_group"],
                            jnp.zeros((ROUTER_ROWS - N_EXPERTS - N_GROUPS,), F32)]).reshape(ROUTER_ROWS, 1)
    x1, h2, logits_t = _merge(
        x, att, y_t, sga, sgs, mod, p["g_post_mix"].reshape(1, d), p["g_pre_ffn"].reshape(1, d),
        p["w_att_out"].astype(BF16), p["w_glu_val"].astype(BF16), p["w_glu_gate"].astype(BF16),
        p["w_mix_out"].astype(BF16), w_rt, b_rt, tm=512)

    eid, wts, dest, counts = _route(logits_t, tn=1024)
    counts = counts[:, 0]
    pend = jnp.cumsum((counts + MOE_BLOCK - 1) // MOE_BLOCK * MOE_BLOCK)
    n_blocks = -(-(n_tok * 2) // MOE_BLOCK) + N_EXPERTS
    block_start = jnp.arange(n_blocks, dtype=I32) * MOE_BLOCK
    block_e = jnp.minimum(jnp.sum((pend[None, :] <= block_start[:, None]).astype(I32), axis=1), N_EXPERTS - 1)
    x_rows = _sc_dispatch(h2.reshape(n_tok, d // 2), dest[0], dest[1], n_blocks * MOE_BLOCK)
    y_rows = _experts(x_rows, block_e, p["w_exp_gate"].astype(BF16), p["w_exp_up"].astype(BF16),
                      p["w_exp_down"].astype(BF16))
    y2 = _sc_gather(y_rows, dest.reshape(-1))
    out = _combine(y2, wts.T, x1.reshape(n_tok, d), mod[:, 5:6, :], p["g_post_ffn"].reshape(1, d), tm=512)
    return out.reshape(bsz, s, d)


def kernel(x, c, rel_bias, w_ada, b_ada, g_pre_mix, g_post_mix, w_in, w_att_out, ssm_lambda_re, ssm_lambda_im, ssm_log_dt, ssm_b_re, ssm_b_im, ssm_c_re, ssm_c_im, ssm_d, w_glu_val, w_glu_gate, w_mix_out, g_pre_ffn, g_post_ffn, w_router_group, b_router_group, w_router_expert, b_router_expert, w_exp_gate, w_exp_up, w_exp_down):
    layered = dict(
        w_ada=w_ada, b_ada=b_ada, g_pre_mix=g_pre_mix, g_post_mix=g_post_mix, w_in=w_in,
        w_att_out=w_att_out, ssm_lambda_re=ssm_lambda_re, ssm_lambda_im=ssm_lambda_im,
        ssm_log_dt=ssm_log_dt, ssm_b_re=ssm_b_re, ssm_b_im=ssm_b_im, ssm_c_re=ssm_c_re,
        ssm_c_im=ssm_c_im, ssm_d=ssm_d, w_glu_val=w_glu_val, w_glu_gate=w_glu_gate,
        w_mix_out=w_mix_out, g_pre_ffn=g_pre_ffn, g_post_ffn=g_post_ffn,
        w_router_group=w_router_group, b_router_group=b_router_group,
        w_router_expert=w_router_expert, b_router_expert=b_router_expert,
        w_exp_gate=w_exp_gate, w_exp_up=w_exp_up, w_exp_down=w_exp_down)
    depth = w_ada.shape[0]
    bsz, d = c.shape
    bias = _bias_tiles(rel_bias)
    far_bucket = np.unique(_t5_bucket_np(np.arange(MOBA_BLOCK + 1, max(x.shape[1], MOBA_BLOCK + 2))))
    assert far_bucket.size == 1
    bfar = rel_bias[int(far_bucket[0])] * LOG2E
    for l in range(depth):
        p = {name: a[l] for name, a in layered.items()}
        mod = _ada(c, p["w_ada"], p["b_ada"]).reshape(bsz, -1, d)
        x = _layer(x, mod, bias, bfar, p)
    return x
```

```python
import functools
import math

import numpy as np
import jax
import jax.numpy as jnp
from jax import lax
from jax.experimental import pallas as pl
from jax.experimental.pallas import tpu as pltpu
from jax.experimental.pallas import tpu_sc as plsc

F32 = jnp.float32
BF16 = jnp.bfloat16
I32 = jnp.int32

ATT_HEADS = 8
HEAD_DIM = 64
MOBA_BLOCK = 256
MOBA_TOPK = 3
NUM_BUCKETS = 32
MAX_DISTANCE = 128
SSM_GROUP = 16
SSM_STATE = 64
SSM_CHUNK = 128
N_GROUPS = 4
EXPERTS_PER_GROUP = 8
N_EXPERTS = N_GROUPS * EXPERTS_PER_GROUP
MOE_BLOCK = 512
RMS_EPS = 1e-6
NEG_INF = -1e30
LOG2E = math.log2(math.e)
LANES = 128
ROUTER_ROWS = 40
VMEM_LIMIT = 56 * 1024 * 1024
SC_CORES = 2
SC_SUBCORES = 16
SC_WINDOW = 128

_NT = (((1,), (1,)), ((), ()))
_NN = (((1,), (0,)), ((), ()))


def _params(sem, vmem=VMEM_LIMIT):
    return pltpu.CompilerParams(dimension_semantics=sem, vmem_limit_bytes=vmem)


def _split_bf16(a):
    hi = a.astype(BF16)
    lo = (a - hi.astype(F32)).astype(BF16)
    return hi, lo


def _dot3(a, b, dims):
    a_hi, a_lo = _split_bf16(a)
    b_hi, b_lo = _split_bf16(b)
    dg = functools.partial(lax.dot_general, dimension_numbers=dims, preferred_element_type=F32)
    return dg(a_hi, b_hi) + (dg(a_hi, b_lo) + dg(a_lo, b_hi))


def _pack_pairs(a):
    w = a.shape[1] // 2
    bits = pltpu.bitcast(a.astype(BF16).astype(F32), jnp.uint32)
    return pltpu.bitcast((bits[:, :w] >> 16) | (bits[:, w:] & jnp.uint32(0xFFFF0000)), I32)


def _unpack_pairs(packed):
    words = pltpu.bitcast(packed, jnp.uint32)
    lo = pltpu.bitcast(words << 16, F32)
    hi = pltpu.bitcast(words & jnp.uint32(0xFFFF0000), F32)
    return lo, hi


def _ada_kernel(c_ref, w_ref, b_ref, o_ref):
    c = c_ref[...]
    ca = c * jax.nn.sigmoid(c)
    o_ref[...] = _dot3(ca, w_ref[...], _NN) + b_ref[...]


def _ada(c, w, b):
    bsz, d = c.shape
    n = w.shape[1]
    tn = 1536
    return pl.pallas_call(
        _ada_kernel,
        out_shape=jax.ShapeDtypeStruct((bsz, n), F32),
        grid=(n // tn,),
        in_specs=[pl.BlockSpec((bsz, d), lambda j: (0, 0)),
                  pl.BlockSpec((d, tn), lambda j: (0, j)),
                  pl.BlockSpec((1, tn), lambda j: (0, j))],
        out_specs=pl.BlockSpec((bsz, tn), lambda j: (0, j)),
        compiler_params=_params(("parallel",)),
        name="ada",
    )(c, w, b.reshape(1, n))


def _t5_bucket_np(dist):
    n = np.maximum(dist, 0)
    max_exact = NUM_BUCKETS // 2
    nf = np.maximum(n, 1).astype(np.float32)
    large = max_exact + (np.log(nf / np.float32(max_exact)) / np.float32(math.log(MAX_DISTANCE / max_exact))
                         * np.float32(NUM_BUCKETS - max_exact)).astype(np.int32)
    large = np.minimum(large, NUM_BUCKETS - 1)
    return np.where(n < max_exact, n, large).astype(np.int32)


def _bias_kernel(rb_ref, bk_ref, o_ref):
    h = pl.program_id(0)
    for t in range(2):
        bk = bk_ref[t]
        acc = jnp.where(bk < 0, NEG_INF, 0.0).astype(F32)
        for b in range(NUM_BUCKETS):
            acc = jnp.where(bk == b, rb_ref[b, h] * LOG2E, acc)
        o_ref[0, t] = acc


def _bias_tiles(rel_bias):
    qi = np.arange(MOBA_BLOCK)[:, None]
    kj = np.arange(MOBA_BLOCK)[None, :]
    own = np.where(qi >= kj, _t5_bucket_np(qi - kj), -1)
    adj = _t5_bucket_np(qi - kj + MOBA_BLOCK)
    buckets = jnp.asarray(np.stack([own, adj]).astype(np.int32))
    return pl.pallas_call(
        _bias_kernel,
        out_shape=jax.ShapeDtypeStruct((ATT_HEADS, 2, MOBA_BLOCK, MOBA_BLOCK), F32),
        grid=(ATT_HEADS,),
        in_specs=[pl.BlockSpec(memory_space=pltpu.SMEM),
                  pl.BlockSpec((2, MOBA_BLOCK, MOBA_BLOCK), lambda h: (0, 0, 0))],
        out_specs=pl.BlockSpec((1, 2, MOBA_BLOCK, MOBA_BLOCK), lambda h: (h, 0, 0, 0)),
        compiler_params=_params(("parallel",)),
        name="t5_bias",
    )(rel_bias, buckets)


def _rms(x, gain):
    ms = jnp.mean(x * x, axis=-1, keepdims=True)
    return x * lax.rsqrt(ms + RMS_EPS) * gain


def _inproj_kernel(x_ref, mod_ref, g_ref, w_ref, wut_ref,
                   q_ref, k_ref, v_ref, ut_ref, sga_ref, sgs_ref):
    aw = q_ref.shape[2]
    d = x_ref.shape[2]
    x = x_ref[0]
    h = _rms(x, g_ref[...]) * (1.0 + mod_ref[0, 1:2, :]) + mod_ref[0, 0:1, :]
    hb = h.astype(BF16)

    def proj(lo, width):
        return jnp.dot(hb, w_ref[:, lo:lo + width], preferred_element_type=F32)

    q_ref[0] = (proj(0, aw) * (HEAD_DIM ** -0.5 * LOG2E)).astype(BF16)
    k_ref[0] = proj(aw, aw).astype(BF16)
    v_ref[0] = proj(2 * aw, aw).astype(BF16)
    ut_ref[0] = lax.dot_general(wut_ref[...], hb, _NT, preferred_element_type=F32)
    sga_ref[0] = jax.nn.sigmoid(proj(3 * aw, d)).astype(BF16)
    sgs_ref[0] = jax.nn.sigmoid(proj(3 * aw + d, d)).astype(BF16)


def _inproj(x, mod, gain, w_rest, w_ut, tm):
    bsz, s, d = x.shape
    aw = ATT_HEADS * HEAD_DIM
    sw = w_ut.shape[0]
    tok = lambda width, dt: jax.ShapeDtypeStruct((bsz, s, width), dt)
    tspec = lambda width: pl.BlockSpec((1, tm, width), lambda b, i: (b, i, 0))
    return pl.pallas_call(
        _inproj_kernel,
        out_shape=(tok(aw, BF16), tok(aw, BF16), tok(aw, BF16),
                   jax.ShapeDtypeStruct((bsz, sw, s), F32), tok(d, BF16), tok(d, BF16)),
        grid=(bsz, s // tm),
        in_specs=[tspec(d),
                  pl.BlockSpec((1, mod.shape[1], d), lambda b, i: (b, 0, 0)),
                  pl.BlockSpec((1, d), lambda b, i: (0, 0)),
                  pl.BlockSpec(w_rest.shape, lambda b, i: (0, 0)),
                  pl.BlockSpec(w_ut.shape, lambda b, i: (0, 0))],
        out_specs=(tspec(aw), tspec(aw), tspec(aw),
                   pl.BlockSpec((1, sw, tm), lambda b, i: (b, 0, i)), tspec(d), tspec(d)),
        compiler_params=_params(("parallel", "parallel")),
        name="inproj",
    )(x, mod, gain, w_rest, w_ut)


def _moba_kernel(bfar_ref, q_ref, k_ref, v_ref, bias_ref, o_ref,
                 kmh_sc, kml_sc, qaug_sc, m_sc, acc_sc):
    blk = MOBA_BLOCK
    i = pl.program_id(1)
    s = k_ref.shape[1]
    nb = s // blk
    nbp = kmh_sc.shape[0]
    npair = q_ref.shape[2] // LANES
    lane = lax.broadcasted_iota(I32, (blk, LANES), 1)
    low_half = lane < HEAD_DIM

    @pl.when(i == 0)
    def _():
        r = lax.broadcasted_iota(I32, (nbp, s), 0)
        c = lax.broadcasted_iota(I32, (nbp, s), 1)
        avg = jnp.where((c >= r * blk) & (c < (r + 1) * blk), 1.0 / blk, 0.0).astype(BF16)
        km = jnp.dot(avg, k_ref[0], preferred_element_type=F32)
        hi, lo = _split_bf16(km)
        kmh_sc[...] = hi
        kml_sc[...] = lo

    row = lax.broadcasted_iota(I32, (nbp, blk), 0)
    half_k = lax.broadcasted_iota(I32, (nbp, LANES), 1) < HEAD_DIM
    for pr in range(npair):
        q2 = q_ref[0, :, pr * LANES:(pr + 1) * LANES]
        kmh = kmh_sc[:, pr * LANES:(pr + 1) * LANES]
        kml = kml_sc[:, pr * LANES:(pr + 1) * LANES]
        for hh in range(2):
            mine = half_k if hh == 0 else jnp.logical_not(half_k)
            gate = (lax.dot_general(jnp.where(mine, kmh, jnp.zeros_like(kmh)), q2, _NT,
                                    preferred_element_type=F32)
                    + lax.dot_general(jnp.where(mine, kml, jnp.zeros_like(kml)), q2, _NT,
                                      preferred_element_type=F32))
            gate = jnp.where(row < i, gate, NEG_INF)
            cnt = jnp.zeros((nbp, blk), F32)
            for m in range(nb):
                other = jnp.broadcast_to(gate[m:m + 1, :], (nbp, blk))
                tie = jnp.where(row > m, 1.0, 0.0)
                cnt = cnt + jnp.where(other > gate, 1.0, 0.0) + jnp.where(other == gate, tie, 0.0)
            chosen = jnp.where(row < i, cnt, float(MOBA_TOPK)) < float(MOBA_TOPK)
            keep_t = jnp.where(row == i, 0.0, jnp.where(chosen, 0.0, NEG_INF))
            keep_t = jnp.concatenate([keep_t, jnp.full((LANES - nbp, blk), NEG_INF, F32)], axis=0)
            qh = jnp.where(low_half if hh == 0 else jnp.logical_not(low_half), q2, jnp.zeros_like(q2))
            qaug_sc[pr, hh * blk:(hh + 1) * blk, :] = jnp.concatenate([qh, keep_t.T.astype(BF16)], axis=1)

    one_hi = jnp.where(lane == HEAD_DIM, 1.0, 0.0).astype(BF16)
    one_lo = jnp.where(lane == 0, 1.0, 0.0).astype(BF16)

    def tile(j, bias_of_head, scalar_bias, first):
        start = pl.multiple_of(j * blk, blk)
        onehot = jnp.where(lane == j, 1.0, 0.0).astype(BF16)
        for pr in range(npair):
            kj = k_ref[0, pl.ds(start, blk), pr * LANES:(pr + 1) * LANES]
            vj = v_ref[0, pl.ds(start, blk), pr * LANES:(pr + 1) * LANES]
            kaug = jnp.concatenate([kj, onehot], axis=1)
            vaug = (jnp.where(low_half, vj, one_hi), jnp.where(low_half, one_lo, vj))
            sc_pair = lax.dot_general(qaug_sc[pr], kaug, _NT, preferred_element_type=F32)
            for hh in range(2):
                h = 2 * pr + hh
                sc = sc_pair[hh * blk:(hh + 1) * blk]
                bias = bias_of_head(h)
                if scalar_bias:
                    m_cur = jnp.max(sc, axis=1, keepdims=True) + bias
                else:
                    sc = sc + bias
                    m_cur = jnp.max(sc, axis=1, keepdims=True)
                if first:
                    m_new = jnp.broadcast_to(m_cur, (blk, LANES))
                else:
                    m_prev = m_sc[h]
                    m_new = jnp.maximum(m_prev, m_cur)
                shift = m_new - bias if scalar_bias else m_new
                pexp = jnp.exp2(sc - jnp.concatenate([shift, shift], axis=1))
                pv = jnp.dot(pexp.astype(BF16), vaug[hh], preferred_element_type=F32)
                if first:
                    acc_sc[h] = pv
                else:
                    acc_sc[h] = jnp.exp2(m_prev - m_new) * acc_sc[h] + pv
                m_sc[h] = m_new

    tile(i, lambda h: bias_ref[h, 0], False, True)

    @pl.when(i >= 1)
    def _():
        tile(i - 1, lambda h: bias_ref[h, 1], False, False)

    def far(j, carry):
        tile(j, lambda h: bfar_ref[h], True, False)
        return carry

    lax.fori_loop(0, jnp.maximum(i - 1, 0), far, 0)

    for pr in range(npair):
        acc_e = acc_sc[2 * pr]
        acc_o = acc_sc[2 * pr + 1]
        out = jnp.where(low_half, acc_e / acc_e[:, HEAD_DIM:HEAD_DIM + 1], acc_o / acc_o[:, 0:1])
        o_ref[0, :, pr * LANES:(pr + 1) * LANES] = out.astype(BF16)


def _moba(q, k, v, bias, bfar):
    bsz, s, aw = q.shape
    blk = MOBA_BLOCK
    nb = s // blk
    nbp = -(-nb // 16) * 16
    heads = bias.shape[0]
    grid_spec = pltpu.PrefetchScalarGridSpec(
        num_scalar_prefetch=1,
        grid=(bsz, nb),
        in_specs=[pl.BlockSpec((1, blk, aw), lambda b, i, _: (b, i, 0)),
                  pl.BlockSpec((1, s, aw), lambda b, i, _: (b, 0, 0)),
                  pl.BlockSpec((1, s, aw), lambda b, i, _: (b, 0, 0)),
                  pl.BlockSpec(bias.shape, lambda b, i, _: (0, 0, 0, 0))],
        out_specs=pl.BlockSpec((1, blk, aw), lambda b, i, _: (b, i, 0)),
        scratch_shapes=[pltpu.VMEM((nbp, aw), BF16), pltpu.VMEM((nbp, aw), BF16),
                        pltpu.VMEM((heads // 2, 2 * blk, 2 * LANES), BF16),
                        pltpu.VMEM((heads, blk, LANES), F32), pltpu.VMEM((heads, blk, LANES), F32)],
    )
    return pl.pallas_call(
        _moba_kernel,
        out_shape=jax.ShapeDtypeStruct((bsz, s, aw), BF16),
        grid_spec=grid_spec,
        compiler_params=_params(("parallel", "arbitrary")),
        name="moba",
    )(bfar, q, k, v, bias)


def _ssm_tables(lam_re, lam_im, log_dt, b_re, b_im, c_re, c_im, d_skip):
    L = SSM_CHUNK
    g = lam_re.shape[0]
    dt = jnp.exp(log_dt)[:, None]
    lr, li = lam_re, lam_im
    mag = jnp.exp(lr * dt)
    ab_re, ab_im = mag * jnp.cos(li * dt), mag * jnp.sin(li * dt)
    den = lr * lr + li * li
    nr, ni = ab_re - 1.0, ab_im
    f_re, f_im = (nr * lr + ni * li) / den, (ni * lr - nr * li) / den
    bb_re = f_re[..., None] * b_re - f_im[..., None] * b_im
    bb_im = f_re[..., None] * b_im + f_im[..., None] * b_re
    n = jnp.arange(L + 1, dtype=F32)[:, None, None]
    pw_mag = jnp.exp(n * (lr * dt)[None])
    pw_re, pw_im = pw_mag * jnp.cos(n * (li * dt)[None]), pw_mag * jnp.sin(n * (li * dt)[None])
    ca_re = c_re[None] * pw_re[:L, :, None, :] - c_im[None] * pw_im[:L, :, None, :]
    ca_im = c_re[None] * pw_im[:L, :, None, :] + c_im[None] * pw_re[:L, :, None, :]
    hp = lax.Precision.HIGHEST
    kern = (jnp.einsum("ngcp,gpd->gdcn", ca_re, bb_re, precision=hp)
            - jnp.einsum("ngcp,gpd->gdcn", ca_im, bb_im, precision=hp))
    skip = jnp.asarray(np.eye(SSM_GROUP, dtype=np.float32))[None, :, :, None] * d_skip.reshape(g, 1, SSM_GROUP, 1)
    kern = kern + skip * jnp.asarray((np.arange(L) == 0).astype(np.float32))
    kern = kern.reshape(g, SSM_GROUP * SSM_GROUP, L)
    rev_re, rev_im = pw_re[L - 1::-1][:L], pw_im[L - 1::-1][:L]
    win_re = rev_re[..., None] * bb_re[None] - rev_im[..., None] * bb_im[None]
    win_im = rev_re[..., None] * bb_im[None] + rev_im[..., None] * bb_re[None]
    w_in = jnp.concatenate([win_re, win_im], axis=2)
    w_in = w_in.transpose(1, 3, 0, 2).reshape(g, SSM_GROUP * L, 2 * SSM_STATE)
    fw_re, fw_im = pw_re[1:], pw_im[1:]
    wo_re = c_re[None] * fw_re[:, :, None, :] - c_im[None] * fw_im[:, :, None, :]
    wo_im = -(c_re[None] * fw_im[:, :, None, :] + c_im[None] * fw_re[:, :, None, :])
    w_out = jnp.concatenate([wo_re, wo_im], axis=3)
    w_out = w_out.transpose(1, 3, 2, 0).reshape(g, 2 * SSM_STATE, SSM_GROUP * L)
    a_chunk = jnp.stack([jnp.concatenate([pw_re[L], pw_re[L]], axis=-1),
                         jnp.concatenate([-pw_im[L], pw_im[L]], axis=-1)], axis=1)
    return kern, w_in.astype(BF16), w_out.astype(BF16), a_chunk


def _ssm_kernel(u_ref, kern_ref, win_ref, wout_ref, ac_ref, y_ref, toep_sc):
    bsz, ng, s = u_ref.shape
    L = SSM_CHUNK
    nc = s // L
    s_ix = lax.broadcasted_iota(I32, (L, L), 0)
    t_ix = lax.broadcasted_iota(I32, (L, L), 1)

    def build(cp, carry):
        r0 = pl.multiple_of(cp * L, L)
        for c in range(ng):
            vec = kern_ref[0, pl.ds(cp * ng + c, 1), :]
            lagged = pltpu.roll(jnp.broadcast_to(vec, (L, L)), 0, 1, stride=1, stride_axis=0)
            toep_sc[pl.ds(r0, L), c * L:(c + 1) * L] = jnp.where(t_ix >= s_ix, lagged, 0.0).astype(BF16)
        return carry

    lax.fori_loop(0, ng, build, 0)

    u = jnp.concatenate(
        [jnp.concatenate([u_ref[:, c, ch * L:(ch + 1) * L] for c in range(ng)], axis=1) for ch in range(nc)],
        axis=0).astype(BF16)
    st = jnp.dot(u, win_ref[0], preferred_element_type=F32)
    a1 = ac_ref[0, 0:1, :]
    a2 = ac_ref[0, 1:2, :]
    state = jnp.zeros((bsz, 2 * SSM_STATE), F32)
    prevs = []
    for ch in range(nc):
        prevs.append(state)
        state = a1 * state + a2 * pltpu.roll(state, SSM_STATE, 1) + st[ch * bsz:(ch + 1) * bsz]
    prev = jnp.concatenate(prevs, axis=0).astype(BF16)
    y = (jnp.dot(u, toep_sc[...], preferred_element_type=F32)
         + jnp.dot(prev, wout_ref[0], preferred_element_type=F32))
    for ch in range(nc):
        for c in range(ng):
            y_ref[:, c, ch * L:(ch + 1) * L] = y[ch * bsz:(ch + 1) * bsz, c * L:(c + 1) * L]


def _ssm(u_t, kern, w_in, w_out, a_chunk):
    bsz, sw, s = u_t.shape
    g = sw // SSM_GROUP
    blk = pl.BlockSpec((bsz, SSM_GROUP, s), lambda j: (0, j, 0))
    per_group = lambda a: pl.BlockSpec((1,) + a.shape[1:], lambda j: (j, 0, 0))
    return pl.pallas_call(
        _ssm_kernel,
        out_shape=jax.ShapeDtypeStruct(u_t.shape, F32),
        grid=(g,),
        in_specs=[blk, per_group(kern), per_group(w_in), per_group(w_out), per_group(a_chunk)],
        out_specs=blk,
        scratch_shapes=[pltpu.VMEM((SSM_GROUP * SSM_CHUNK, SSM_GROUP * SSM_CHUNK), BF16)],
        compiler_params=_params(("parallel",)),
        name="ssm",
    )(u_t, kern, w_in, w_out, a_chunk)


def _gelu_tanh(x):
    return 0.5 * x * (1.0 + jnp.tanh(math.sqrt(2.0 / math.pi) * (x + 0.044715 * (x * x * x))))


def _merge_kernel(x_ref, att_ref, yt_ref, sga_ref, sgs_ref, mod_ref, gpost_ref, gpre_ref,
                  wao_ref, wgv_ref, wgg_ref, wmo_ref, wrt_ref, brt_ref,
                  x1_ref, h2_ref, lt_ref):
    a_br = jnp.dot(att_ref[0], wao_ref[...], preferred_element_type=F32)
    z = _gelu_tanh(yt_ref[0]).T.astype(BF16)
    s_br = (jnp.dot(z, wgv_ref[...], preferred_element_type=F32)
            * jax.nn.sigmoid(jnp.dot(z, wgg_ref[...], preferred_element_type=F32)))
    merged = sga_ref[0].astype(F32) * a_br + sgs_ref[0].astype(F32) * s_br
    mix = jnp.dot(merged.astype(BF16), wmo_ref[...], preferred_element_type=F32)
    x1 = x_ref[0] + mod_ref[0, 2:3, :] * _rms(mix, gpost_ref[...])
    x1_ref[0] = x1
    h2 = _rms(x1, gpre_ref[...]) * (1.0 + mod_ref[0, 4:5, :]) + mod_ref[0, 3:4, :]
    h2_ref[0] = _pack_pairs(h2)
    lt_ref[...] = _dot3(wrt_ref[...], h2, _NT) + brt_ref[...]


def _merge(x, att, yt, sga, sgs, mod, g_post, g_pre, w_ao, w_gv, w_gg, w_mo, w_rt, b_rt, tm):
    bsz, s, d = x.shape
    aw = att.shape[2]
    sw = yt.shape[1]
    nt = s // tm
    tspec = lambda width: pl.BlockSpec((1, tm, width), lambda b, i: (b, i, 0))
    full = lambda a: pl.BlockSpec(a.shape, lambda b, i: (0,) * a.ndim)
    return pl.pallas_call(
        _merge_kernel,
        out_shape=(jax.ShapeDtypeStruct((bsz, s, d), F32), jax.ShapeDtypeStruct((bsz, s, d // 2), I32),
                   jax.ShapeDtypeStruct((ROUTER_ROWS, bsz * s), F32)),
        grid=(bsz, nt),
        in_specs=[tspec(d), tspec(aw), pl.BlockSpec((1, sw, tm), lambda b, i: (b, 0, i)),
                  tspec(d), tspec(d),
                  pl.BlockSpec((1, mod.shape[1], d), lambda b, i: (b, 0, 0)),
                  full(g_post), full(g_pre), full(w_ao), full(w_gv), full(w_gg), full(w_mo),
                  full(w_rt), full(b_rt)],
        out_specs=(tspec(d), tspec(d // 2), pl.BlockSpec((ROUTER_ROWS, tm), lambda b, i: (0, b * nt + i))),
        compiler_params=_params(("parallel", "parallel")),
        name="merge",
    )(x, att, yt, sga, sgs, mod, g_post, g_pre, w_ao, w_gv, w_gg, w_mo, w_rt, b_rt)


def _route_kernel(lt_ref, eid_ref, wt_ref, dest_ref, cnt_ref, base_sc, pstart_sc):
    tn = lt_ref.shape[1]
    epg = EXPERTS_PER_GROUP
    ph = pl.program_id(0)
    step = pl.program_id(1)

    @pl.when((ph == 0) & (step == 0))
    def _():
        base_sc[...] = jnp.zeros_like(base_sc)

    @pl.when((ph == 1) & (step == 0))
    def _():
        total = base_sc[...]
        cnt_ref[...] = total.astype(I32)
        padded = jnp.floor((total + (MOE_BLOCK - 1.0)) * (1.0 / MOE_BLOCK)) * MOE_BLOCK
        r = lax.broadcasted_iota(I32, (N_EXPERTS, N_EXPERTS), 0)
        c = lax.broadcasted_iota(I32, (N_EXPERTS, N_EXPERTS), 1)
        before = jnp.where(c < r, 1.0, 0.0)
        pstart_sc[...] = _dot3(before, padded, _NN)
        base_sc[...] = jnp.zeros_like(base_sc)

    row8 = lax.broadcasted_iota(I32, (epg, tn), 0)
    gl = lt_ref[N_EXPERTS:N_EXPERTS + epg, :]
    gl = jnp.where(row8 < N_GROUPS, gl, -jnp.inf)
    gmax = jnp.max(gl, axis=0, keepdims=True)
    gidx = jnp.min(jnp.where(gl == gmax, row8, epg), axis=0, keepdims=True)

    el = jnp.zeros((epg, tn), F32)
    for g in range(N_GROUPS):
        el = jnp.where(gidx == g, lt_ref[g * epg:(g + 1) * epg, :], el)
    m1 = jnp.max(el, axis=0, keepdims=True)
    i1 = jnp.min(jnp.where(el == m1, row8, epg), axis=0, keepdims=True)
    el2 = jnp.where(row8 == i1, -jnp.inf, el)
    m2 = jnp.max(el2, axis=0, keepdims=True)
    i2 = jnp.min(jnp.where(el2 == m2, row8, epg), axis=0, keepdims=True)
    e1 = gidx * epg + i1
    e2 = gidx * epg + i2

    row32 = lax.broadcasted_iota(I32, (N_EXPERTS, tn), 0)
    hit1 = row32 == e1
    hit2 = row32 == e2
    onehot = jnp.where(hit1, 1.0, jnp.where(hit2, 1.0, 0.0))

    @pl.when(ph == 1)
    def _():
        g_p = 1.0 / jnp.sum(jnp.exp(gl - gmax), axis=0, keepdims=True)
        zsum = jnp.sum(jnp.exp(el - m1), axis=0, keepdims=True)
        p1 = 1.0 / zsum
        p2 = jnp.exp(m2 - m1) / zsum
        sr = lax.broadcasted_iota(I32, (tn, tn), 0)
        tc = lax.broadcasted_iota(I32, (tn, tn), 1)
        earlier = jnp.where(sr < tc, 1.0, 0.0).astype(BF16)
        place = (jnp.dot(onehot.astype(BF16), earlier, preferred_element_type=F32)
                 + base_sc[:, 0:1] + pstart_sc[:, 0:1])
        d1 = jnp.sum(jnp.where(hit1, place, 0.0), axis=0, keepdims=True)
        d2 = jnp.sum(jnp.where(hit2, place, 0.0), axis=0, keepdims=True)
        eid_ref[...] = jnp.concatenate([e1, e2], axis=0)
        wt_ref[...] = jnp.concatenate([g_p * p1 / (p1 + p2), g_p * p2 / (p1 + p2)], axis=0)
        dest_ref[...] = jnp.concatenate([d1, d2], axis=0).astype(I32)

    base_sc[...] = base_sc[...] + jnp.sum(onehot, axis=1, keepdims=True)


def _route(logits_t, tn):
    n = logits_t.shape[1]
    two = lambda dt: jax.ShapeDtypeStruct((2, n), dt)
    tspec = pl.BlockSpec((2, tn), lambda ph, i: (0, i * ph))
    return pl.pallas_call(
        _route_kernel,
        out_shape=(two(I32), two(F32), two(I32), jax.ShapeDtypeStruct((N_EXPERTS, LANES), I32)),
        grid=(2, n // tn),
        in_specs=[pl.BlockSpec((ROUTER_ROWS, tn), lambda ph, i: (0, i))],
        out_specs=(tspec, tspec, tspec, pl.BlockSpec((N_EXPERTS, LANES), lambda ph, i: (0, 0))),
        scratch_shapes=[pltpu.VMEM((N_EXPERTS, LANES), F32), pltpu.VMEM((N_EXPERTS, LANES), F32)],
        compiler_params=_params(("arbitrary", "arbitrary")),
        name="route",
    )(logits_t)


def _sc_mesh():
    return plsc.VectorSubcoreMesh(core_axis_name="c", subcore_axis_name="s",
                                  num_cores=SC_CORES, num_subcores=SC_SUBCORES)


def _sc_worker_offset(per_worker):
    return (lax.axis_index("s") * SC_CORES + lax.axis_index("c")) * per_worker


def _sc_dispatch(rows, dest0, dest1, n_rows):
    n_tok, w = rows.shape
    per_worker = n_tok // (SC_CORES * SC_SUBCORES)
    x_ref = jax.new_ref(jnp.zeros((n_rows, w), rows.dtype))

    @functools.partial(
        pl.kernel, mesh=_sc_mesh(), out_type=(),
        scratch_types=[pltpu.VMEM((SC_WINDOW,), I32), pltpu.VMEM((SC_WINDOW,), I32),
                       pltpu.VMEM((SC_WINDOW, w), rows.dtype), pltpu.SemaphoreType.DMA],
        name="sc_dispatch")
    def scatter(rows_hbm, dest0_hbm, dest1_hbm, x_hbm, idx0_v, idx1_v, rows_v, sem):
        base = _sc_worker_offset(per_worker)

        @pl.loop(0, per_worker // SC_WINDOW)
        def _(chunk):
            off = base + chunk * SC_WINDOW
            pltpu.sync_copy(rows_hbm.at[pl.ds(off, SC_WINDOW)], rows_v)
            pltpu.sync_copy(dest0_hbm.at[pl.ds(off, SC_WINDOW)], idx0_v)
            pltpu.sync_copy(dest1_hbm.at[pl.ds(off, SC_WINDOW)], idx1_v)
            pltpu.async_copy(rows_v, x_hbm.at[idx0_v], sem).wait()
            pltpu.async_copy(rows_v, x_hbm.at[idx1_v], sem).wait()

    scatter(rows, dest0, dest1, x_ref)
    return jax.freeze(x_ref)


def _sc_gather(table, idx):
    n_idx = idx.shape[0]
    w = table.shape[1]
    per_worker = n_idx // (SC_CORES * SC_SUBCORES)

    @functools.partial(
        pl.kernel, mesh=_sc_mesh(), out_type=jax.ShapeDtypeStruct((n_idx, w), table.dtype),
        scratch_types=[pltpu.VMEM((SC_WINDOW,), I32), pltpu.VMEM((SC_WINDOW, w), table.dtype),
                       pltpu.SemaphoreType.DMA],
        name="sc_gather")
    def gather(table_hbm, idx_hbm, out_hbm, idx_v, rows_v, sem):
        base = _sc_worker_offset(per_worker)

        @pl.loop(0, per_worker // SC_WINDOW)
        def _(chunk):
            off = base + chunk * SC_WINDOW
            pltpu.sync_copy(idx_hbm.at[pl.ds(off, SC_WINDOW)], idx_v)
            pltpu.async_copy(table_hbm.at[idx_v], rows_v, sem).wait()
            pltpu.sync_copy(rows_v, out_hbm.at[pl.ds(off, SC_WINDOW)])

    return gather(table, idx)


def _expert_kernel(be_ref, x_ref, wg_ref, wu_ref, wd_ref, y_ref):
    lo, hi = _unpack_pairs(x_ref[...])
    xb = jnp.concatenate([lo.astype(BF16), hi.astype(BF16)], axis=1)
    gate = jnp.dot(xb, wg_ref[0], preferred_element_type=F32)
    up = jnp.dot(xb, wu_ref[0], preferred_element_type=F32)
    hid = (gate * jax.nn.sigmoid(gate) * up).astype(BF16)
    y_ref[...] = _pack_pairs(jnp.dot(hid, wd_ref[0], preferred_element_type=F32))


def _experts(x_rows, block_e, w_gate, w_up, w_down):
    n_blocks = block_e.shape[0]
    d = w_gate.shape[1]
    de = w_gate.shape[2]
    grid_spec = pltpu.PrefetchScalarGridSpec(
        num_scalar_prefetch=1,
        grid=(n_blocks,),
        in_specs=[pl.BlockSpec((MOE_BLOCK, d // 2), lambda i, be: (i, 0)),
                  pl.BlockSpec((1, d, de), lambda i, be: (be[i], 0, 0)),
                  pl.BlockSpec((1, d, de), lambda i, be: (be[i], 0, 0)),
                  pl.BlockSpec((1, de, d), lambda i, be: (be[i], 0, 0))],
        out_specs=pl.BlockSpec((MOE_BLOCK, d // 2), lambda i, be: (i, 0)),
    )
    return pl.pallas_call(
        _expert_kernel,
        out_shape=jax.ShapeDtypeStruct((n_blocks * MOE_BLOCK, d // 2), I32),
        grid_spec=grid_spec,
        compiler_params=_params(("arbitrary",)),
        name="experts",
    )(block_e, x_rows, w_gate, w_up, w_down)


def _combine_kernel(y0_ref, y1_ref, x1_ref, wt_ref, g2_ref, gain_ref, o_ref):
    y0 = jnp.concatenate(_unpack_pairs(y0_ref[...]), axis=1)
    y1 = jnp.concatenate(_unpack_pairs(y1_ref[...]), axis=1)
    f = wt_ref[:, 0:1] * y0 + wt_ref[:, 1:2] * y1
    o_ref[...] = x1_ref[...] + g2_ref[0] * _rms(f, gain_ref[...])


def _combine(y2, wts, x1, gate2, gain, tm):
    n, d = x1.shape
    s = n // gate2.shape[0]
    nt = n // tm
    return pl.pallas_call(
        _combine_kernel,
        out_shape=jax.ShapeDtypeStruct((n, d), F32),
        grid=(nt,),
        in_specs=[pl.BlockSpec((tm, d // 2), lambda i: (i, 0)),
                  pl.BlockSpec((tm, d // 2), lambda i: (nt + i, 0)),
                  pl.BlockSpec((tm, d), lambda i: (i, 0)),
                  pl.BlockSpec((tm, 2), lambda i: (i, 0)),
                  pl.BlockSpec((1, 1, d), lambda i: ((i * tm) // s, 0, 0)),
                  pl.BlockSpec((1, d), lambda i: (0, 0))],
        out_specs=pl.BlockSpec((tm, d), lambda i: (i, 0)),
        compiler_params=_params(("parallel",)),
        name="combine",
    )(y2, y2, x1, wts, gate2, gain)


def _layer(x, mod, bias, bfar, p):
    bsz, s, d = x.shape
    n_tok = bsz * s
    aw = ATT_HEADS * HEAD_DIM
    sw = p["ssm_d"].shape[0]

    w_in = p["w_in"]
    w_rest = jnp.concatenate([w_in[:, :3 * aw], w_in[:, 3 * aw + sw:]], axis=1).astype(BF16)
    w_ut = w_in[:, 3 * aw:3 * aw + sw].T.astype(BF16)
    q, k, v, u_t, sga, sgs = _inproj(x, mod, p["g_pre_mix"].reshape(1, d), w_rest, w_ut, tm=512)
    att = _moba(q, k, v, bias, bfar)
    tables = _ssm_tables(p["ssm_lambda_re"], p["ssm_lambda_im"], p["ssm_log_dt"], p["ssm_b_re"],
                         p["ssm_b_im"], p["ssm_c_re"], p["ssm_c_im"], p["ssm_d"])
    y_t = _ssm(u_t, *tables)

    w_rt = jnp.concatenate([p["w_router_expert"].T, p["w_router_group"].T,
                            jnp.zeros((ROUTER_ROWS - N_EXPERTS - N_GROUPS, d), F32)], axis=0)
    b_rt = jnp.concatenate([p["b_router_expert"], p["b_router_group"],
                            jnp.zeros((ROUTER_ROWS - N_EXPERTS - N_GROUPS,), F32)]).reshape(ROUTER_ROWS, 1)
    x1, h2, logits_t = _merge(
        x, att, y_t, sga, sgs, mod, p["g_post_mix"].reshape(1, d), p["g_pre_ffn"].reshape(1, d),
        p["w_att_out"].astype(BF16), p["w_glu_val"].astype(BF16), p["w_glu_gate"].astype(BF16),
        p["w_mix_out"].astype(BF16), w_rt, b_rt, tm=512)

    eid, wts, dest, counts = _route(logits_t, tn=1024)
    counts = counts[:, 0]
    pend = jnp.cumsum((counts + MOE_BLOCK - 1) // MOE_BLOCK * MOE_BLOCK)
    n_blocks = -(-(n_tok * 2) // MOE_BLOCK) + N_EXPERTS
    block_start = jnp.arange(n_blocks, dtype=I32) * MOE_BLOCK
    block_e = jnp.minimum(jnp.sum((pend[None, :] <= block_start[:, None]).astype(I32), axis=1), N_EXPERTS - 1)
    x_rows = _sc_dispatch(h2.reshape(n_tok, d // 2), dest[0], dest[1], n_blocks * MOE_BLOCK)
    y_rows = _experts(x_rows, block_e, p["w_exp_gate"].astype(BF16), p["w_exp_up"].astype(BF16),
                      p["w_exp_down"].astype(BF16))
    y2 = _sc_gather(y_rows, dest.reshape(-1))
    out = _combine(y2, wts.T, x1.reshape(n_tok, d), mod[:, 5:6, :], p["g_post_ffn"].reshape(1, d), tm=512)
    return out.reshape(bsz, s, d)


def kernel(x, c, rel_bias, w_ada, b_ada, g_pre_mix, g_post_mix, w_in, w_att_out, ssm_lambda_re, ssm_lambda_im, ssm_log_dt, ssm_b_re, ssm_b_im, ssm_c_re, ssm_c_im, ssm_d, w_glu_val, w_glu_gate, w_mix_out, g_pre_ffn, g_post_ffn, w_router_group, b_router_group, w_router_expert, b_router_expert, w_exp_gate, w_exp_up, w_exp_down):
    layered = dict(
        w_ada=w_ada, b_ada=b_ada, g_pre_mix=g_pre_mix, g_post_mix=g_post_mix, w_in=w_in,
        w_att_out=w_att_out, ssm_lambda_re=ssm_lambda_re, ssm_lambda_im=ssm_lambda_im,
        ssm_log_dt=ssm_log_dt, ssm_b_re=ssm_b_re, ssm_b_im=ssm_b_im, ssm_c_re=ssm_c_re,
        ssm_c_im=ssm_c_im, ssm_d=ssm_d, w_glu_val=w_glu_val, w_glu_gate=w_glu_gate,
        w_mix_out=w_mix_out, g_pre_ffn=g_pre_ffn, g_post_ffn=g_post_ffn,
        w_router_group=w_router_group, b_router_group=b_router_group,
        w_router_expert=w_router_expert, b_router_expert=b_router_expert,
        w_exp_gate=w_exp_gate, w_exp_up=w_exp_up, w_exp_down=w_exp_down)
    depth = w_ada.shape[0]
    bsz, d = c.shape
    bias = _bias_tiles(rel_bias)
    far_bucket = np.unique(_t5_bucket_np(np.arange(MOBA_BLOCK + 1, max(x.shape[1], MOBA_BLOCK + 2))))
    assert far_bucket.size == 1
    bfar = rel_bias[int(far_bucket[0])] * LOG2E
    for l in range(depth):
        p = {name: a[l] for name, a in layered.items()}
        mod = _ada(c, p["w_ada"], p["b_ada"]).reshape(bsz, -1, d)
        x = _layer(x, mod, bias, bfar, p)
    return x
```

```python
import functools
import math

import numpy as np
import jax
import jax.numpy as jnp
from jax import lax
from jax.experimental import pallas as pl
from jax.experimental.pallas import tpu as pltpu
from jax.experimental.pallas import tpu_sc as plsc

F32 = jnp.float32
BF16 = jnp.bfloat16
I32 = jnp.int32

ATT_HEADS = 8
HEAD_DIM = 64
MOBA_BLOCK = 256
MOBA_TOPK = 3
NUM_BUCKETS = 32
MAX_DISTANCE = 128
SSM_GROUP = 16
SSM_STATE = 64
SSM_CHUNK = 128
N_GROUPS = 4
EXPERTS_PER_GROUP = 8
N_EXPERTS = N_GROUPS * EXPERTS_PER_GROUP
MOE_BLOCK = 512
RMS_EPS = 1e-6
NEG_INF = -1e30
LOG2E = math.log2(math.e)
LANES = 128
ROUTER_ROWS = 40
VMEM_LIMIT = 56 * 1024 * 1024
SC_CORES = 2
SC_SUBCORES = 16
SC_WINDOW = 64

_NT = (((1,), (1,)), ((), ()))
_NN = (((1,), (0,)), ((), ()))


def _params(sem, vmem=VMEM_LIMIT):
    return pltpu.CompilerParams(dimension_semantics=sem, vmem_limit_bytes=vmem)


def _split_bf16(a):
    hi = a.astype(BF16)
    lo = (a - hi.astype(F32)).astype(BF16)
    return hi, lo


def _dot3(a, b, dims):
    a_hi, a_lo = _split_bf16(a)
    b_hi, b_lo = _split_bf16(b)
    dg = functools.partial(lax.dot_general, dimension_numbers=dims, preferred_element_type=F32)
    return dg(a_hi, b_hi) + (dg(a_hi, b_lo) + dg(a_lo, b_hi))


def _pack_pairs(a):
    w = a.shape[1] // 2
    bits = pltpu.bitcast(a.astype(BF16).astype(F32), jnp.uint32)
    return pltpu.bitcast((bits[:, :w] >> 16) | (bits[:, w:] & jnp.uint32(0xFFFF0000)), I32)


def _unpack_pairs(packed):
    words = pltpu.bitcast(packed, jnp.uint32)
    lo = pltpu.bitcast(words << 16, F32)
    hi = pltpu.bitcast(words & jnp.uint32(0xFFFF0000), F32)
    return lo, hi


def _ada_kernel(c_ref, w_ref, b_ref, o_ref):
    c = c_ref[...]
    ca = c * jax.nn.sigmoid(c)
    o_ref[...] = _dot3(ca, w_ref[...], _NN) + b_ref[...]


def _ada(c, w, b):
    bsz, d = c.shape
    n = w.shape[1]
    tn = 1536
    return pl.pallas_call(
        _ada_kernel,
        out_shape=jax.ShapeDtypeStruct((bsz, n), F32),
        grid=(n // tn,),
        in_specs=[pl.BlockSpec((bsz, d), lambda j: (0, 0)),
                  pl.BlockSpec((d, tn), lambda j: (0, j)),
                  pl.BlockSpec((1, tn), lambda j: (0, j))],
        out_specs=pl.BlockSpec((bsz, tn), lambda j: (0, j)),
        compiler_params=_params(("parallel",)),
        name="ada",
    )(c, w, b.reshape(1, n))


def _t5_bucket_np(dist):
    n = np.maximum(dist, 0)
    max_exact = NUM_BUCKETS // 2
    nf = np.maximum(n, 1).astype(np.float32)
    large = max_exact + (np.log(nf / np.float32(max_exact)) / np.float32(math.log(MAX_DISTANCE / max_exact))
                         * np.float32(NUM_BUCKETS - max_exact)).astype(np.int32)
    large = np.minimum(large, NUM_BUCKETS - 1)
    return np.where(n < max_exact, n, large).astype(np.int32)


def _bias_kernel(rb_ref, bk_ref, o_ref):
    h = pl.program_id(0)
    for t in range(2):
        bk = bk_ref[t]
        acc = jnp.where(bk < 0, NEG_INF, 0.0).astype(F32)
        for b in range(NUM_BUCKETS):
            acc = jnp.where(bk == b, rb_ref[b, h] * LOG2E, acc)
        o_ref[0, t] = acc


def _bias_tiles(rel_bias):
    qi = np.arange(MOBA_BLOCK)[:, None]
    kj = np.arange(MOBA_BLOCK)[None, :]
    own = np.where(qi >= kj, _t5_bucket_np(qi - kj), -1)
    adj = _t5_bucket_np(qi - kj + MOBA_BLOCK)
    buckets = jnp.asarray(np.stack([own, adj]).astype(np.int32))
    return pl.pallas_call(
        _bias_kernel,
        out_shape=jax.ShapeDtypeStruct((ATT_HEADS, 2, MOBA_BLOCK, MOBA_BLOCK), F32),
        grid=(ATT_HEADS,),
        in_specs=[pl.BlockSpec(memory_space=pltpu.SMEM),
                  pl.BlockSpec((2, MOBA_BLOCK, MOBA_BLOCK), lambda h: (0, 0, 0))],
        out_specs=pl.BlockSpec((1, 2, MOBA_BLOCK, MOBA_BLOCK), lambda h: (h, 0, 0, 0)),
        compiler_params=_params(("parallel",)),
        name="t5_bias",
    )(rel_bias, buckets)


def _rms(x, gain):
    ms = jnp.mean(x * x, axis=-1, keepdims=True)
    return x * lax.rsqrt(ms + RMS_EPS) * gain


def _inproj_kernel(x_ref, mod_ref, g_ref, w_ref, wut_ref,
                   q_ref, k_ref, v_ref, ut_ref, sga_ref, sgs_ref):
    aw = q_ref.shape[2]
    d = x_ref.shape[2]
    x = x_ref[0]
    h = _rms(x, g_ref[...]) * (1.0 + mod_ref[0, 1:2, :]) + mod_ref[0, 0:1, :]
    hb = h.astype(BF16)

    def proj(lo, width):
        return jnp.dot(hb, w_ref[:, lo:lo + width], preferred_element_type=F32)

    q_ref[0] = (proj(0, aw) * (HEAD_DIM ** -0.5 * LOG2E)).astype(BF16)
    k_ref[0] = proj(aw, aw).astype(BF16)
    v_ref[0] = proj(2 * aw, aw).astype(BF16)
    ut_ref[0] = lax.dot_general(wut_ref[...], hb, _NT, preferred_element_type=F32)
    sga_ref[0] = jax.nn.sigmoid(proj(3 * aw, d)).astype(BF16)
    sgs_ref[0] = jax.nn.sigmoid(proj(3 * aw + d, d)).astype(BF16)


def _inproj(x, mod, gain, w_rest, w_ut, tm):
    bsz, s, d = x.shape
    aw = ATT_HEADS * HEAD_DIM
    sw = w_ut.shape[0]
    tok = lambda width, dt: jax.ShapeDtypeStruct((bsz, s, width), dt)
    tspec = lambda width: pl.BlockSpec((1, tm, width), lambda b, i: (b, i, 0))
    return pl.pallas_call(
        _inproj_kernel,
        out_shape=(tok(aw, BF16), tok(aw, BF16), tok(aw, BF16),
                   jax.ShapeDtypeStruct((bsz, sw, s), F32), tok(d, BF16), tok(d, BF16)),
        grid=(bsz, s // tm),
        in_specs=[tspec(d),
                  pl.BlockSpec((1, mod.shape[1], d), lambda b, i: (b, 0, 0)),
                  pl.BlockSpec((1, d), lambda b, i: (0, 0)),
                  pl.BlockSpec(w_rest.shape, lambda b, i: (0, 0)),
                  pl.BlockSpec(w_ut.shape, lambda b, i: (0, 0))],
        out_specs=(tspec(aw), tspec(aw), tspec(aw),
                   pl.BlockSpec((1, sw, tm), lambda b, i: (b, 0, i)), tspec(d), tspec(d)),
        compiler_params=_params(("parallel", "parallel")),
        name="inproj",
    )(x, mod, gain, w_rest, w_ut)


def _moba_kernel(bfar_ref, q_ref, k_ref, v_ref, bias_ref, o_ref,
                 kmh_sc, kml_sc, qaug_sc, m_sc, acc_sc):
    blk = MOBA_BLOCK
    i = pl.program_id(1)
    s = k_ref.shape[1]
    nb = s // blk
    nbp = kmh_sc.shape[0]
    npair = q_ref.shape[2] // LANES
    lane = lax.broadcasted_iota(I32, (blk, LANES), 1)
    low_half = lane < HEAD_DIM

    @pl.when(i == 0)
    def _():
        r = lax.broadcasted_iota(I32, (nbp, s), 0)
        c = lax.broadcasted_iota(I32, (nbp, s), 1)
        avg = jnp.where((c >= r * blk) & (c < (r + 1) * blk), 1.0 / blk, 0.0).astype(BF16)
        km = jnp.dot(avg, k_ref[0], preferred_element_type=F32)
        hi, lo = _split_bf16(km)
        kmh_sc[...] = hi
        kml_sc[...] = lo

    row = lax.broadcasted_iota(I32, (nbp, blk), 0)
    half_k = lax.broadcasted_iota(I32, (nbp, LANES), 1) < HEAD_DIM
    for pr in range(npair):
        q2 = q_ref[0, :, pr * LANES:(pr + 1) * LANES]
        kmh = kmh_sc[:, pr * LANES:(pr + 1) * LANES]
        kml = kml_sc[:, pr * LANES:(pr + 1) * LANES]
        for hh in range(2):
            mine = half_k if hh == 0 else jnp.logical_not(half_k)
            gate = (lax.dot_general(jnp.where(mine, kmh, jnp.zeros_like(kmh)), q2, _NT,
                                    preferred_element_type=F32)
                    + lax.dot_general(jnp.where(mine, kml, jnp.zeros_like(kml)), q2, _NT,
                                      preferred_element_type=F32))
            gate = jnp.where(row < i, gate, NEG_INF)
            cnt = jnp.zeros((nbp, blk), F32)
            for m in range(nb):
                other = jnp.broadcast_to(gate[m:m + 1, :], (nbp, blk))
                tie = jnp.where(row > m, 1.0, 0.0)
                cnt = cnt + jnp.where(other > gate, 1.0, 0.0) + jnp.where(other == gate, tie, 0.0)
            chosen = jnp.where(row < i, cnt, float(MOBA_TOPK)) < float(MOBA_TOPK)
            keep_t = jnp.where(row == i, 0.0, jnp.where(chosen, 0.0, NEG_INF))
            keep_t = jnp.concatenate([keep_t, jnp.full((LANES - nbp, blk), NEG_INF, F32)], axis=0)
            qh = jnp.where(low_half if hh == 0 else jnp.logical_not(low_half), q2, jnp.zeros_like(q2))
            qaug_sc[pr, hh * blk:(hh + 1) * blk, :] = jnp.concatenate([qh, keep_t.T.astype(BF16)], axis=1)

    one_hi = jnp.where(lane == HEAD_DIM, 1.0, 0.0).astype(BF16)
    one_lo = jnp.where(lane == 0, 1.0, 0.0).astype(BF16)

    def tile(j, bias_of_head, scalar_bias, first):
        start = pl.multiple_of(j * blk, blk)
        onehot = jnp.where(lane == j, 1.0, 0.0).astype(BF16)
        for pr in range(npair):
            kj = k_ref[0, pl.ds(start, blk), pr * LANES:(pr + 1) * LANES]
            vj = v_ref[0, pl.ds(start, blk), pr * LANES:(pr + 1) * LANES]
            kaug = jnp.concatenate([kj, onehot], axis=1)
            vaug = (jnp.where(low_half, vj, one_hi), jnp.where(low_half, one_lo, vj))
            sc_pair = lax.dot_general(qaug_sc[pr], kaug, _NT, preferred_element_type=F32)
            for hh in range(2):
                h = 2 * pr + hh
                sc = sc_pair[hh * blk:(hh + 1) * blk]
                bias = bias_of_head(h)
                if scalar_bias:
                    m_cur = jnp.max(sc, axis=1, keepdims=True) + bias
                else:
                    sc = sc + bias
                    m_cur = jnp.max(sc, axis=1, keepdims=True)
                if first:
                    m_new = jnp.broadcast_to(m_cur, (blk, LANES))
                else:
                    m_prev = m_sc[h]
                    m_new = jnp.maximum(m_prev, m_cur)
                shift = m_new - bias if scalar_bias else m_new
                pexp = jnp.exp2(sc - jnp.concatenate([shift, shift], axis=1))
                pv = jnp.dot(pexp.astype(BF16), vaug[hh], preferred_element_type=F32)
                if first:
                    acc_sc[h] = pv
                else:
                    acc_sc[h] = jnp.exp2(m_prev - m_new) * acc_sc[h] + pv
                m_sc[h] = m_new

    tile(i, lambda h: bias_ref[h, 0], False, True)

    @pl.when(i >= 1)
    def _():
        tile(i - 1, lambda h: bias_ref[h, 1], False, False)

    def far(j, carry):
        tile(j, lambda h: bfar_ref[h], True, False)
        return carry

    lax.fori_loop(0, jnp.maximum(i - 1, 0), far, 0)

    for pr in range(npair):
        acc_e = acc_sc[2 * pr]
        acc_o = acc_sc[2 * pr + 1]
        out = jnp.where(low_half, acc_e / acc_e[:, HEAD_DIM:HEAD_DIM + 1], acc_o / acc_o[:, 0:1])
        o_ref[0, :, pr * LANES:(pr + 1) * LANES] = out.astype(BF16)


def _moba(q, k, v, bias, bfar):
    bsz, s, aw = q.shape
    blk = MOBA_BLOCK
    nb = s // blk
    nbp = -(-nb // 16) * 16
    heads = bias.shape[0]
    grid_spec = pltpu.PrefetchScalarGridSpec(
        num_scalar_prefetch=1,
        grid=(bsz, nb),
        in_specs=[pl.BlockSpec((1, blk, aw), lambda b, i, _: (b, i, 0)),
                  pl.BlockSpec((1, s, aw), lambda b, i, _: (b, 0, 0)),
                  pl.BlockSpec((1, s, aw), lambda b, i, _: (b, 0, 0)),
                  pl.BlockSpec(bias.shape, lambda b, i, _: (0, 0, 0, 0))],
        out_specs=pl.BlockSpec((1, blk, aw), lambda b, i, _: (b, i, 0)),
        scratch_shapes=[pltpu.VMEM((nbp, aw), BF16), pltpu.VMEM((nbp, aw), BF16),
                        pltpu.VMEM((heads // 2, 2 * blk, 2 * LANES), BF16),
                        pltpu.VMEM((heads, blk, LANES), F32), pltpu.VMEM((heads, blk, LANES), F32)],
    )
    return pl.pallas_call(
        _moba_kernel,
        out_shape=jax.ShapeDtypeStruct((bsz, s, aw), BF16),
        grid_spec=grid_spec,
        compiler_params=_params(("parallel", "arbitrary")),
        name="moba",
    )(bfar, q, k, v, bias)


def _ssm_tables(lam_re, lam_im, log_dt, b_re, b_im, c_re, c_im, d_skip):
    L = SSM_CHUNK
    g = lam_re.shape[0]
    dt = jnp.exp(log_dt)[:, None]
    lr, li = lam_re, lam_im
    mag = jnp.exp(lr * dt)
    ab_re, ab_im = mag * jnp.cos(li * dt), mag * jnp.sin(li * dt)
    den = lr * lr + li * li
    nr, ni = ab_re - 1.0, ab_im
    f_re, f_im = (nr * lr + ni * li) / den, (ni * lr - nr * li) / den
    bb_re = f_re[..., None] * b_re - f_im[..., None] * b_im
    bb_im = f_re[..., None] * b_im + f_im[..., None] * b_re
    n = jnp.arange(L + 1, dtype=F32)[:, None, None]
    pw_mag = jnp.exp(n * (lr * dt)[None])
    pw_re, pw_im = pw_mag * jnp.cos(n * (li * dt)[None]), pw_mag * jnp.sin(n * (li * dt)[None])
    ca_re = c_re[None] * pw_re[:L, :, None, :] - c_im[None] * pw_im[:L, :, None, :]
    ca_im = c_re[None] * pw_im[:L, :, None, :] + c_im[None] * pw_re[:L, :, None, :]
    hp = lax.Precision.HIGHEST
    kern = (jnp.einsum("ngcp,gpd->gdcn", ca_re, bb_re, precision=hp)
            - jnp.einsum("ngcp,gpd->gdcn", ca_im, bb_im, precision=hp))
    skip = jnp.asarray(np.eye(SSM_GROUP, dtype=np.float32))[None, :, :, None] * d_skip.reshape(g, 1, SSM_GROUP, 1)
    kern = kern + skip * jnp.asarray((np.arange(L) == 0).astype(np.float32))
    kern = kern.reshape(g, SSM_GROUP * SSM_GROUP, L)
    rev_re, rev_im = pw_re[L - 1::-1][:L], pw_im[L - 1::-1][:L]
    win_re = rev_re[..., None] * bb_re[None] - rev_im[..., None] * bb_im[None]
    win_im = rev_re[..., None] * bb_im[None] + rev_im[..., None] * bb_re[None]
    w_in = jnp.concatenate([win_re, win_im], axis=2)
    w_in = w_in.transpose(1, 3, 0, 2).reshape(g, SSM_GROUP * L, 2 * SSM_STATE)
    fw_re, fw_im = pw_re[1:], pw_im[1:]
    wo_re = c_re[None] * fw_re[:, :, None, :] - c_im[None] * fw_im[:, :, None, :]
    wo_im = -(c_re[None] * fw_im[:, :, None, :] + c_im[None] * fw_re[:, :, None, :])
    w_out = jnp.concatenate([wo_re, wo_im], axis=3)
    w_out = w_out.transpose(1, 3, 2, 0).reshape(g, 2 * SSM_STATE, SSM_GROUP * L)
    a_chunk = jnp.stack([jnp.concatenate([pw_re[L], pw_re[L]], axis=-1),
                         jnp.concatenate([-pw_im[L], pw_im[L]], axis=-1)], axis=1)
    return kern, w_in.astype(BF16), w_out.astype(BF16), a_chunk


def _ssm_kernel(u_ref, kern_ref, win_ref, wout_ref, ac_ref, y_ref, toep_sc):
    bsz, ng, s = u_ref.shape
    L = SSM_CHUNK
    nc = s // L
    s_ix = lax.broadcasted_iota(I32, (L, L), 0)
    t_ix = lax.broadcasted_iota(I32, (L, L), 1)

    def build(cp, carry):
        r0 = pl.multiple_of(cp * L, L)
        for c in range(ng):
            vec = kern_ref[0, pl.ds(cp * ng + c, 1), :]
            lagged = pltpu.roll(jnp.broadcast_to(vec, (L, L)), 0, 1, stride=1, stride_axis=0)
            toep_sc[pl.ds(r0, L), c * L:(c + 1) * L] = jnp.where(t_ix >= s_ix, lagged, 0.0).astype(BF16)
        return carry

    lax.fori_loop(0, ng, build, 0)

    u = jnp.concatenate(
        [jnp.concatenate([u_ref[:, c, ch * L:(ch + 1) * L] for c in range(ng)], axis=1) for ch in range(nc)],
        axis=0).astype(BF16)
    st = jnp.dot(u, win_ref[0], preferred_element_type=F32)
    a1 = ac_ref[0, 0:1, :]
    a2 = ac_ref[0, 1:2, :]
    state = jnp.zeros((bsz, 2 * SSM_STATE), F32)
    prevs = []
    for ch in range(nc):
        prevs.append(state)
        state = a1 * state + a2 * pltpu.roll(state, SSM_STATE, 1) + st[ch * bsz:(ch + 1) * bsz]
    prev = jnp.concatenate(prevs, axis=0).astype(BF16)
    y = (jnp.dot(u, toep_sc[...], preferred_element_type=F32)
         + jnp.dot(prev, wout_ref[0], preferred_element_type=F32))
    for ch in range(nc):
        for c in range(ng):
            y_ref[:, c, ch * L:(ch + 1) * L] = y[ch * bsz:(ch + 1) * bsz, c * L:(c + 1) * L]


def _ssm(u_t, kern, w_in, w_out, a_chunk):
    bsz, sw, s = u_t.shape
    g = sw // SSM_GROUP
    blk = pl.BlockSpec((bsz, SSM_GROUP, s), lambda j: (0, j, 0))
    per_group = lambda a: pl.BlockSpec((1,) + a.shape[1:], lambda j: (j, 0, 0))
    return pl.pallas_call(
        _ssm_kernel,
        out_shape=jax.ShapeDtypeStruct(u_t.shape, F32),
        grid=(g,),
        in_specs=[blk, per_group(kern), per_group(w_in), per_group(w_out), per_group(a_chunk)],
        out_specs=blk,
        scratch_shapes=[pltpu.VMEM((SSM_GROUP * SSM_CHUNK, SSM_GROUP * SSM_CHUNK), BF16)],
        compiler_params=_params(("parallel",)),
        name="ssm",
    )(u_t, kern, w_in, w_out, a_chunk)


def _gelu_tanh(x):
    return 0.5 * x * (1.0 + jnp.tanh(math.sqrt(2.0 / math.pi) * (x + 0.044715 * (x * x * x))))


def _merge_kernel(x_ref, att_ref, yt_ref, sga_ref, sgs_ref, mod_ref, gpost_ref, gpre_ref,
                  wao_ref, wgv_ref, wgg_ref, wmo_ref, wrt_ref, brt_ref,
                  x1_ref, h2_ref, lt_ref):
    a_br = jnp.dot(att_ref[0], wao_ref[...], preferred_element_type=F32)
    z = _gelu_tanh(yt_ref[0]).T.astype(BF16)
    s_br = (jnp.dot(z, wgv_ref[...], preferred_element_type=F32)
            * jax.nn.sigmoid(jnp.dot(z, wgg_ref[...], preferred_element_type=F32)))
    merged = sga_ref[0].astype(F32) * a_br + sgs_ref[0].astype(F32) * s_br
    mix = jnp.dot(merged.astype(BF16), wmo_ref[...], preferred_element_type=F32)
    x1 = x_ref[0] + mod_ref[0, 2:3, :] * _rms(mix, gpost_ref[...])
    x1_ref[0] = x1
    h2 = _rms(x1, gpre_ref[...]) * (1.0 + mod_ref[0, 4:5, :]) + mod_ref[0, 3:4, :]
    h2_ref[0] = _pack_pairs(h2)
    lt_ref[...] = _dot3(wrt_ref[...], h2, _NT) + brt_ref[...]


def _merge(x, att, yt, sga, sgs, mod, g_post, g_pre, w_ao, w_gv, w_gg, w_mo, w_rt, b_rt, tm):
    bsz, s, d = x.shape
    aw = att.shape[2]
    sw = yt.shape[1]
    nt = s // tm
    tspec = lambda width: pl.BlockSpec((1, tm, width), lambda b, i: (b, i, 0))
    full = lambda a: pl.BlockSpec(a.shape, lambda b, i: (0,) * a.ndim)
    return pl.pallas_call(
        _merge_kernel,
        out_shape=(jax.ShapeDtypeStruct((bsz, s, d), F32), jax.ShapeDtypeStruct((bsz, s, d // 2), I32),
                   jax.ShapeDtypeStruct((ROUTER_ROWS, bsz * s), F32)),
        grid=(bsz, nt),
        in_specs=[tspec(d), tspec(aw), pl.BlockSpec((1, sw, tm), lambda b, i: (b, 0, i)),
                  tspec(d), tspec(d),
                  pl.BlockSpec((1, mod.shape[1], d), lambda b, i: (b, 0, 0)),
                  full(g_post), full(g_pre), full(w_ao), full(w_gv), full(w_gg), full(w_mo),
                  full(w_rt), full(b_rt)],
        out_specs=(tspec(d), tspec(d // 2), pl.BlockSpec((ROUTER_ROWS, tm), lambda b, i: (0, b * nt + i))),
        compiler_params=_params(("parallel", "parallel")),
        name="merge",
    )(x, att, yt, sga, sgs, mod, g_post, g_pre, w_ao, w_gv, w_gg, w_mo, w_rt, b_rt)


def _route_kernel(lt_ref, eid_ref, wt_ref, dest_ref, cnt_ref, base_sc, pstart_sc):
    tn = lt_ref.shape[1]
    epg = EXPERTS_PER_GROUP
    ph = pl.program_id(0)
    step = pl.program_id(1)

    @pl.when((ph == 0) & (step == 0))
    def _():
        base_sc[...] = jnp.zeros_like(base_sc)

    @pl.when((ph == 1) & (step == 0))
    def _():
        total = base_sc[...]
        cnt_ref[...] = total.astype(I32)
        padded = jnp.floor((total + (MOE_BLOCK - 1.0)) * (1.0 / MOE_BLOCK)) * MOE_BLOCK
        r = lax.broadcasted_iota(I32, (N_EXPERTS, N_EXPERTS), 0)
        c = lax.broadcasted_iota(I32, (N_EXPERTS, N_EXPERTS), 1)
        before = jnp.where(c < r, 1.0, 0.0)
        pstart_sc[...] = _dot3(before, padded, _NN)
        base_sc[...] = jnp.zeros_like(base_sc)

    row8 = lax.broadcasted_iota(I32, (epg, tn), 0)
    gl = lt_ref[N_EXPERTS:N_EXPERTS + epg, :]
    gl = jnp.where(row8 < N_GROUPS, gl, -jnp.inf)
    gmax = jnp.max(gl, axis=0, keepdims=True)
    gidx = jnp.min(jnp.where(gl == gmax, row8, epg), axis=0, keepdims=True)

    el = jnp.zeros((epg, tn), F32)
    for g in range(N_GROUPS):
        el = jnp.where(gidx == g, lt_ref[g * epg:(g + 1) * epg, :], el)
    m1 = jnp.max(el, axis=0, keepdims=True)
    i1 = jnp.min(jnp.where(el == m1, row8, epg), axis=0, keepdims=True)
    el2 = jnp.where(row8 == i1, -jnp.inf, el)
    m2 = jnp.max(el2, axis=0, keepdims=True)
    i2 = jnp.min(jnp.where(el2 == m2, row8, epg), axis=0, keepdims=True)
    e1 = gidx * epg + i1
    e2 = gidx * epg + i2

    row32 = lax.broadcasted_iota(I32, (N_EXPERTS, tn), 0)
    hit1 = row32 == e1
    hit2 = row32 == e2
    onehot = jnp.where(hit1, 1.0, jnp.where(hit2, 1.0, 0.0))

    @pl.when(ph == 1)
    def _():
        g_p = 1.0 / jnp.sum(jnp.exp(gl - gmax), axis=0, keepdims=True)
        zsum = jnp.sum(jnp.exp(el - m1), axis=0, keepdims=True)
        p1 = 1.0 / zsum
        p2 = jnp.exp(m2 - m1) / zsum
        sr = lax.broadcasted_iota(I32, (tn, tn), 0)
        tc = lax.broadcasted_iota(I32, (tn, tn), 1)
        earlier = jnp.where(sr < tc, 1.0, 0.0).astype(BF16)
        place = (jnp.dot(onehot.astype(BF16), earlier, preferred_element_type=F32)
                 + base_sc[:, 0:1] + pstart_sc[:, 0:1])
        d1 = jnp.sum(jnp.where(hit1, place, 0.0), axis=0, keepdims=True)
        d2 = jnp.sum(jnp.where(hit2, place, 0.0), axis=0, keepdims=True)
        eid_ref[...] = jnp.concatenate([e1, e2], axis=0)
        wt_ref[...] = jnp.concatenate([g_p * p1 / (p1 + p2), g_p * p2 / (p1 + p2)], axis=0)
        dest_ref[...] = jnp.concatenate([d1, d2], axis=0).astype(I32)

    base_sc[...] = base_sc[...] + jnp.sum(onehot, axis=1, keepdims=True)


def _route(logits_t, tn):
    n = logits_t.shape[1]
    two = lambda dt: jax.ShapeDtypeStruct((2, n), dt)
    tspec = pl.BlockSpec((2, tn), lambda ph, i: (0, i * ph))
    return pl.pallas_call(
        _route_kernel,
        out_shape=(two(I32), two(F32), two(I32), jax.ShapeDtypeStruct((N_EXPERTS, LANES), I32)),
        grid=(2, n // tn),
        in_specs=[pl.BlockSpec((ROUTER_ROWS, tn), lambda ph, i: (0, i))],
        out_specs=(tspec, tspec, tspec, pl.BlockSpec((N_EXPERTS, LANES), lambda ph, i: (0, 0))),
        scratch_shapes=[pltpu.VMEM((N_EXPERTS, LANES), F32), pltpu.VMEM((N_EXPERTS, LANES), F32)],
        compiler_params=_params(("arbitrary", "arbitrary")),
        name="route",
    )(logits_t)


def _sc_mesh():
    return plsc.VectorSubcoreMesh(core_axis_name="c", subcore_axis_name="s",
                                  num_cores=SC_CORES, num_subcores=SC_SUBCORES)


def _sc_worker_offset(per_worker):
    return (lax.axis_index("s") * SC_CORES + lax.axis_index("c")) * per_worker


def _sc_dispatch(rows, dest0, dest1, n_rows):
    n_tok, w = rows.shape
    per_worker = n_tok // (SC_CORES * SC_SUBCORES)
    n_chunks = per_worker // SC_WINDOW

    @functools.partial(
        pl.kernel, mesh=_sc_mesh(), out_type=jax.ShapeDtypeStruct((n_rows, w), rows.dtype),
        scratch_types=[pltpu.VMEM((SC_WINDOW,), I32)] * 4 + [pltpu.VMEM((SC_WINDOW, w), rows.dtype)] * 2
        + [pltpu.SemaphoreType.DMA] * 4,
        name="sc_dispatch")
    def scatter(rows_hbm, dest0_hbm, dest1_hbm, x_hbm, i0a, i1a, i0b, i1b, rows_a, rows_b,
                lsem_a, lsem_b, ssem_a, ssem_b):
        base = _sc_worker_offset(per_worker)
        bufs = ((i0a, i1a, rows_a, lsem_a, ssem_a), (i0b, i1b, rows_b, lsem_b, ssem_b))

        def load(chunk, b):
            i0, i1, rv, lsem, _ = bufs[b]
            off = base + chunk * SC_WINDOW
            pltpu.sync_copy(dest0_hbm.at[pl.ds(off, SC_WINDOW)], i0)
            pltpu.sync_copy(dest1_hbm.at[pl.ds(off, SC_WINDOW)], i1)
            return pltpu.make_async_copy(rows_hbm.at[pl.ds(off, SC_WINDOW)], rv, lsem)

        def stores(b):
            i0, i1, rv, _, ssem = bufs[b]
            return (pltpu.make_async_copy(rv, x_hbm.at[i0], ssem),
                    pltpu.make_async_copy(rv, x_hbm.at[i1], ssem))

        load(0, 0).start()

        @pl.loop(0, n_chunks, step=2)
        def _(chunk):
            for b in range(2):
                cur = chunk + b

                @pl.when(cur >= 1)
                def _():
                    for cp in stores(1 - b):
                        cp.wait()

                @pl.when(cur + 1 < n_chunks)
                def _():
                    load(cur + 1, 1 - b).start()

                pltpu.make_async_copy(rows_hbm.at[pl.ds(base, SC_WINDOW)], bufs[b][2], bufs[b][3]).wait()
                for cp in stores(b):
                    cp.start()

        for cp in stores((n_chunks - 1) % 2):
            cp.wait()

    return scatter(rows, dest0, dest1)


def _sc_gather(table, idx):
    n_idx = idx.shape[0]
    w = table.shape[1]
    per_worker = n_idx // (SC_CORES * SC_SUBCORES)
    n_chunks = per_worker // SC_WINDOW

    @functools.partial(
        pl.kernel, mesh=_sc_mesh(), out_type=jax.ShapeDtypeStruct((n_idx, w), table.dtype),
        scratch_types=[pltpu.VMEM((SC_WINDOW,), I32)] * 2 + [pltpu.VMEM((SC_WINDOW, w), table.dtype)] * 2
        + [pltpu.SemaphoreType.DMA] * 4,
        name="sc_gather")
    def gather(table_hbm, idx_hbm, out_hbm, idx_a, idx_b, rows_a, rows_b, gsem_a, gsem_b, wsem_a, wsem_b):
        base = _sc_worker_offset(per_worker)
        bufs = ((idx_a, rows_a, gsem_a, wsem_a), (idx_b, rows_b, gsem_b, wsem_b))

        def fetch(chunk, b):
            iv, rv, gsem, _ = bufs[b]
            pltpu.sync_copy(idx_hbm.at[pl.ds(base + chunk * SC_WINDOW, SC_WINDOW)], iv)
            return pltpu.make_async_copy(table_hbm.at[iv], rv, gsem)

        def write(chunk, b):
            _, rv, _, wsem = bufs[b]
            return pltpu.make_async_copy(rv, out_hbm.at[pl.ds(base + chunk * SC_WINDOW, SC_WINDOW)], wsem)

        fetch(0, 0).start()

        @pl.loop(0, n_chunks, step=2)
        def _(chunk):
            for b in range(2):
                cur = chunk + b

                @pl.when(cur >= 1)
                def _():
                    write(cur - 1, 1 - b).wait()

                @pl.when(cur + 1 < n_chunks)
                def _():
                    fetch(cur + 1, 1 - b).start()

                pltpu.make_async_copy(table_hbm.at[bufs[b][0]], bufs[b][1], bufs[b][2]).wait()
                write(cur, b).start()

        write(n_chunks - 1, (n_chunks - 1) % 2).wait()

    return gather(table, idx)


def _expert_kernel(be_ref, nv_ref, x_ref, wg_ref, wu_ref, wd_ref, y_ref):
    rowi = lax.broadcasted_iota(I32, x_ref.shape, 0)
    lo, hi = _unpack_pairs(jnp.where(rowi < nv_ref[pl.program_id(0)], x_ref[...], 0))
    xb = jnp.concatenate([lo.astype(BF16), hi.astype(BF16)], axis=1)
    gate = jnp.dot(xb, wg_ref[0], preferred_element_type=F32)
    up = jnp.dot(xb, wu_ref[0], preferred_element_type=F32)
    hid = (gate * jax.nn.sigmoid(gate) * up).astype(BF16)
    y_ref[...] = _pack_pairs(jnp.dot(hid, wd_ref[0], preferred_element_type=F32))


def _experts(x_rows, block_e, block_valid, w_gate, w_up, w_down):
    n_blocks = block_e.shape[0]
    d = w_gate.shape[1]
    de = w_gate.shape[2]
    grid_spec = pltpu.PrefetchScalarGridSpec(
        num_scalar_prefetch=2,
        grid=(n_blocks,),
        in_specs=[pl.BlockSpec((MOE_BLOCK, d // 2), lambda i, be, nv: (i, 0)),
                  pl.BlockSpec((1, d, de), lambda i, be, nv: (be[i], 0, 0)),
                  pl.BlockSpec((1, d, de), lambda i, be, nv: (be[i], 0, 0)),
                  pl.BlockSpec((1, de, d), lambda i, be, nv: (be[i], 0, 0))],
        out_specs=pl.BlockSpec((MOE_BLOCK, d // 2), lambda i, be, nv: (i, 0)),
    )
    return pl.pallas_call(
        _expert_kernel,
        out_shape=jax.ShapeDtypeStruct((n_blocks * MOE_BLOCK, d // 2), I32),
        grid_spec=grid_spec,
        compiler_params=_params(("arbitrary",)),
        name="experts",
    )(block_e, block_valid, x_rows, w_gate, w_up, w_down)


def _combine_kernel(y0_ref, y1_ref, x1_ref, wt_ref, g2_ref, gain_ref, o_ref):
    y0 = jnp.concatenate(_unpack_pairs(y0_ref[...]), axis=1)
    y1 = jnp.concatenate(_unpack_pairs(y1_ref[...]), axis=1)
    f = wt_ref[:, 0:1] * y0 + wt_ref[:, 1:2] * y1
    o_ref[...] = x1_ref[...] + g2_ref[0] * _rms(f, gain_ref[...])


def _combine(y2, wts, x1, gate2, gain, tm):
    n, d = x1.shape
    s = n // gate2.shape[0]
    nt = n // tm
    return pl.pallas_call(
        _combine_kernel,
        out_shape=jax.ShapeDtypeStruct((n, d), F32),
        grid=(nt,),
        in_specs=[pl.BlockSpec((tm, d // 2), lambda i: (i, 0)),
                  pl.BlockSpec((tm, d // 2), lambda i: (nt + i, 0)),
                  pl.BlockSpec((tm, d), lambda i: (i, 0)),
                  pl.BlockSpec((tm, 2), lambda i: (i, 0)),
                  pl.BlockSpec((1, 1, d), lambda i: ((i * tm) // s, 0, 0)),
                  pl.BlockSpec((1, d), lambda i: (0, 0))],
        out_specs=pl.BlockSpec((tm, d), lambda i: (i, 0)),
        compiler_params=_params(("parallel",)),
        name="combine",
    )(y2, y2, x1, wts, gate2, gain)


def _layer(x, mod, bias, bfar, p):
    bsz, s, d = x.shape
    n_tok = bsz * s
    aw = ATT_HEADS * HEAD_DIM
    sw = p["ssm_d"].shape[0]

    w_in = p["w_in"]
    w_rest = jnp.concatenate([w_in[:, :3 * aw], w_in[:, 3 * aw + sw:]], axis=1).astype(BF16)
    w_ut = w_in[:, 3 * aw:3 * aw + sw].T.astype(BF16)
    q, k, v, u_t, sga, sgs = _inproj(x, mod, p["g_pre_mix"].reshape(1, d), w_rest, w_ut, tm=512)
    att = _moba(q, k, v, bias, bfar)
    tables = _ssm_tables(p["ssm_lambda_re"], p["ssm_lambda_im"], p["ssm_log_dt"], p["ssm_b_re"],
                         p["ssm_b_im"], p["ssm_c_re"], p["ssm_c_im"], p["ssm_d"])
    y_t = _ssm(u_t, *tables)

    w_rt = jnp.concatenate([p["w_router_expert"].T, p["w_router_group"].T,
                            jnp.zeros((ROUTER_ROWS - N_EXPERTS - N_GROUPS, d), F32)], axis=0)
    b_rt = jnp.concatenate([p["b_router_expert"], p["b_router_group"],
                            jnp.zeros((ROUTER_ROWS - N_EXPERTS - N_GROUPS,), F32)]).reshape(ROUTER_ROWS, 1)
    x1, h2, logits_t = _merge(
        x, att, y_t, sga, sgs, mod, p["g_post_mix"].reshape(1, d), p["g_pre_ffn"].reshape(1, d),
        p["w_att_out"].astype(BF16), p["w_glu_val"].astype(BF16), p["w_glu_gate"].astype(BF16),
        p["w_mix_out"].astype(BF16), w_rt, b_rt, tm=512)

    eid, wts, dest, counts = _route(logits_t, tn=1024)
    counts = counts[:, 0]
    padded = (counts + MOE_BLOCK - 1) // MOE_BLOCK * MOE_BLOCK
    pend = jnp.cumsum(padded)
    n_blocks = -(-(n_tok * 2) // MOE_BLOCK) + N_EXPERTS
    block_start = jnp.arange(n_blocks, dtype=I32) * MOE_BLOCK
    block_e = jnp.minimum(jnp.sum((pend[None, :] <= block_start[:, None]).astype(I32), axis=1), N_EXPERTS - 1)
    block_valid = jnp.clip(counts[block_e] - (block_start - (pend - padded)[block_e]), 0, MOE_BLOCK)
    x_rows = _sc_dispatch(h2.reshape(n_tok, d // 2), dest[0], dest[1], n_blocks * MOE_BLOCK)
    y_rows = _experts(x_rows, block_e, block_valid, p["w_exp_gate"].astype(BF16), p["w_exp_up"].astype(BF16),
                      p["w_exp_down"].astype(BF16))
    y2 = _sc_gather(y_rows, dest.reshape(-1))
    out = _combine(y2, wts.T, x1.reshape(n_tok, d), mod[:, 5:6, :], p["g_post_ffn"].reshape(1, d), tm=512)
    return out.reshape(bsz, s, d)


def kernel(x, c, rel_bias, w_ada, b_ada, g_pre_mix, g_post_mix, w_in, w_att_out, ssm_lambda_re, ssm_lambda_im, ssm_log_dt, ssm_b_re, ssm_b_im, ssm_c_re, ssm_c_im, ssm_d, w_glu_val, w_glu_gate, w_mix_out, g_pre_ffn, g_post_ffn, w_router_group, b_router_group, w_router_expert, b_router_expert, w_exp_gate, w_exp_up, w_exp_down):
    layered = dict(
        w_ada=w_ada, b_ada=b_ada, g_pre_mix=g_pre_mix, g_post_mix=g_post_mix, w_in=w_in,
        w_att_out=w_att_out, ssm_lambda_re=ssm_lambda_re, ssm_lambda_im=ssm_lambda_im,
        ssm_log_dt=ssm_log_dt, ssm_b_re=ssm_b_re, ssm_b_im=ssm_b_im, ssm_c_re=ssm_c_re,
        ssm_c_im=ssm_c_im, ssm_d=ssm_d, w_glu_val=w_glu_val, w_glu_gate=w_glu_gate,
        w_mix_out=w_mix_out, g_pre_ffn=g_pre_ffn, g_post_ffn=g_post_ffn,
        w_router_group=w_router_group, b_router_group=b_router_group,
        w_router_expert=w_router_expert, b_router_expert=b_router_expert,
        w_exp_gate=w_exp_gate, w_exp_up=w_exp_up, w_exp_down=w_exp_down)
    depth = w_ada.shape[0]
    bsz, d = c.shape
    bias = _bias_tiles(rel_bias)
    far_bucket = np.unique(_t5_bucket_np(np.arange(MOBA_BLOCK + 1, max(x.shape[1], MOBA_BLOCK + 2))))
    assert far_bucket.size == 1
    bfar = rel_bias[int(far_bucket[0])] * LOG2E
    for l in range(depth):
        p = {name: a[l] for name, a in layered.items()}
        mod = _ada(c, p["w_ada"], p["b_ada"]).reshape(bsz, -1, d)
        x = _layer(x, mod, bias, bfar, p)
    return x
```

```python
import functools
import math

import numpy as np
import jax
import jax.numpy as jnp
from jax import lax
from jax.experimental import pallas as pl
from jax.experimental.pallas import tpu as pltpu
from jax.experimental.pallas import tpu_sc as plsc

F32 = jnp.float32
BF16 = jnp.bfloat16
I32 = jnp.int32

ATT_HEADS = 8
HEAD_DIM = 64
MOBA_BLOCK = 256
MOBA_TOPK = 3
NUM_BUCKETS = 32
MAX_DISTANCE = 128
SSM_GROUP = 16
SSM_STATE = 64
SSM_CHUNK = 128
N_GROUPS = 4
EXPERTS_PER_GROUP = 8
N_EXPERTS = N_GROUPS * EXPERTS_PER_GROUP
MOE_BLOCK = 512
RMS_EPS = 1e-6
NEG_INF = -1e30
LOG2E = math.log2(math.e)
LANES = 128
ROUTER_ROWS = 40
VMEM_LIMIT = 56 * 1024 * 1024
SC_CORES = 2
SC_SUBCORES = 16
SC_WINDOW = 64
COMBINE_CHUNKS = 4

_NT = (((1,), (1,)), ((), ()))
_NN = (((1,), (0,)), ((), ()))


def _params(sem, vmem=VMEM_LIMIT):
    return pltpu.CompilerParams(dimension_semantics=sem, vmem_limit_bytes=vmem)


def _split_bf16(a):
    hi = a.astype(BF16)
    lo = (a - hi.astype(F32)).astype(BF16)
    return hi, lo


def _dot3(a, b, dims):
    a_hi, a_lo = _split_bf16(a)
    b_hi, b_lo = _split_bf16(b)
    dg = functools.partial(lax.dot_general, dimension_numbers=dims, preferred_element_type=F32)
    return dg(a_hi, b_hi) + (dg(a_hi, b_lo) + dg(a_lo, b_hi))


def _pack_pairs(a):
    w = a.shape[1] // 2
    bits = pltpu.bitcast(a.astype(BF16).astype(F32), jnp.uint32)
    return pltpu.bitcast((bits[:, :w] >> 16) | (bits[:, w:] & jnp.uint32(0xFFFF0000)), I32)


def _unpack_pairs(packed):
    words = pltpu.bitcast(packed, jnp.uint32)
    lo = pltpu.bitcast(words << 16, F32)
    hi = pltpu.bitcast(words & jnp.uint32(0xFFFF0000), F32)
    return lo, hi


def _ada_kernel(c_ref, w_ref, b_ref, o_ref):
    c = c_ref[...]
    ca = c * jax.nn.sigmoid(c)
    o_ref[...] = _dot3(ca, w_ref[...], _NN) + b_ref[...]


def _ada(c, w, b):
    bsz, d = c.shape
    n = w.shape[1]
    tn = 1536
    return pl.pallas_call(
        _ada_kernel,
        out_shape=jax.ShapeDtypeStruct((bsz, n), F32),
        grid=(n // tn,),
        in_specs=[pl.BlockSpec((bsz, d), lambda j: (0, 0)),
                  pl.BlockSpec((d, tn), lambda j: (0, j)),
                  pl.BlockSpec((1, tn), lambda j: (0, j))],
        out_specs=pl.BlockSpec((bsz, tn), lambda j: (0, j)),
        compiler_params=_params(("parallel",)),
        name="ada",
    )(c, w, b.reshape(1, n))


def _t5_bucket_np(dist):
    n = np.maximum(dist, 0)
    max_exact = NUM_BUCKETS // 2
    nf = np.maximum(n, 1).astype(np.float32)
    large = max_exact + (np.log(nf / np.float32(max_exact)) / np.float32(math.log(MAX_DISTANCE / max_exact))
                         * np.float32(NUM_BUCKETS - max_exact)).astype(np.int32)
    large = np.minimum(large, NUM_BUCKETS - 1)
    return np.where(n < max_exact, n, large).astype(np.int32)


def _bias_kernel(rb_ref, bk_ref, o_ref):
    h = pl.program_id(0)
    for t in range(2):
        bk = bk_ref[t]
        acc = jnp.where(bk < 0, NEG_INF, 0.0).astype(F32)
        for b in range(NUM_BUCKETS):
            acc = jnp.where(bk == b, rb_ref[b, h] * LOG2E, acc)
        o_ref[0, t] = acc


def _bias_tiles(rel_bias):
    qi = np.arange(MOBA_BLOCK)[:, None]
    kj = np.arange(MOBA_BLOCK)[None, :]
    own = np.where(qi >= kj, _t5_bucket_np(qi - kj), -1)
    adj = _t5_bucket_np(qi - kj + MOBA_BLOCK)
    buckets = jnp.asarray(np.stack([own, adj]).astype(np.int32))
    return pl.pallas_call(
        _bias_kernel,
        out_shape=jax.ShapeDtypeStruct((ATT_HEADS, 2, MOBA_BLOCK, MOBA_BLOCK), F32),
        grid=(ATT_HEADS,),
        in_specs=[pl.BlockSpec(memory_space=pltpu.SMEM),
                  pl.BlockSpec((2, MOBA_BLOCK, MOBA_BLOCK), lambda h: (0, 0, 0))],
        out_specs=pl.BlockSpec((1, 2, MOBA_BLOCK, MOBA_BLOCK), lambda h: (h, 0, 0, 0)),
        compiler_params=_params(("parallel",)),
        name="t5_bias",
    )(rel_bias, buckets)


def _rms(x, gain):
    ms = jnp.mean(x * x, axis=-1, keepdims=True)
    return x * lax.rsqrt(ms + RMS_EPS) * gain


def _inproj_kernel(x_ref, mod_ref, g_ref, w_ref, wut_ref,
                   q_ref, k_ref, v_ref, ut_ref, sga_ref, sgs_ref):
    aw = q_ref.shape[2]
    d = x_ref.shape[2]
    x = x_ref[0]
    h = _rms(x, g_ref[...]) * (1.0 + mod_ref[0, 1:2, :]) + mod_ref[0, 0:1, :]
    hb = h.astype(BF16)

    def proj(lo, width):
        return jnp.dot(hb, w_ref[:, lo:lo + width], preferred_element_type=F32)

    q_ref[0] = (proj(0, aw) * (HEAD_DIM ** -0.5 * LOG2E)).astype(BF16)
    k_ref[0] = proj(aw, aw).astype(BF16)
    v_ref[0] = proj(2 * aw, aw).astype(BF16)
    ut_ref[0] = lax.dot_general(wut_ref[...], hb, _NT, preferred_element_type=F32)
    sga_ref[0] = jax.nn.sigmoid(proj(3 * aw, d)).astype(BF16)
    sgs_ref[0] = jax.nn.sigmoid(proj(3 * aw + d, d)).astype(BF16)


def _inproj(x, mod, gain, w_rest, w_ut, tm):
    bsz, s, d = x.shape
    aw = ATT_HEADS * HEAD_DIM
    sw = w_ut.shape[0]
    tok = lambda width, dt: jax.ShapeDtypeStruct((bsz, s, width), dt)
    tspec = lambda width: pl.BlockSpec((1, tm, width), lambda b, i: (b, i, 0))
    return pl.pallas_call(
        _inproj_kernel,
        out_shape=(tok(aw, BF16), tok(aw, BF16), tok(aw, BF16),
                   jax.ShapeDtypeStruct((bsz, sw, s), F32), tok(d, BF16), tok(d, BF16)),
        grid=(bsz, s // tm),
        in_specs=[tspec(d),
                  pl.BlockSpec((1, mod.shape[1], d), lambda b, i: (b, 0, 0)),
                  pl.BlockSpec((1, d), lambda b, i: (0, 0)),
                  pl.BlockSpec(w_rest.shape, lambda b, i: (0, 0)),
                  pl.BlockSpec(w_ut.shape, lambda b, i: (0, 0))],
        out_specs=(tspec(aw), tspec(aw), tspec(aw),
                   pl.BlockSpec((1, sw, tm), lambda b, i: (b, 0, i)), tspec(d), tspec(d)),
        compiler_params=_params(("parallel", "parallel")),
        name="inproj",
    )(x, mod, gain, w_rest, w_ut)


def _moba_kernel(bfar_ref, q_ref, k_ref, v_ref, bias_ref, o_ref,
                 kmh_sc, kml_sc, qaug_sc, m_sc, acc_sc):
    blk = MOBA_BLOCK
    i = pl.program_id(1)
    s = k_ref.shape[1]
    nb = s // blk
    nbp = kmh_sc.shape[0]
    npair = q_ref.shape[2] // LANES
    lane = lax.broadcasted_iota(I32, (blk, LANES), 1)
    low_half = lane < HEAD_DIM

    @pl.when(i == 0)
    def _():
        r = lax.broadcasted_iota(I32, (nbp, s), 0)
        c = lax.broadcasted_iota(I32, (nbp, s), 1)
        avg = jnp.where((c >= r * blk) & (c < (r + 1) * blk), 1.0 / blk, 0.0).astype(BF16)
        km = jnp.dot(avg, k_ref[0], preferred_element_type=F32)
        hi, lo = _split_bf16(km)
        kmh_sc[...] = hi
        kml_sc[...] = lo

    row = lax.broadcasted_iota(I32, (nbp, blk), 0)
    half_k = lax.broadcasted_iota(I32, (nbp, LANES), 1) < HEAD_DIM
    for pr in range(npair):
        q2 = q_ref[0, :, pr * LANES:(pr + 1) * LANES]
        kmh = kmh_sc[:, pr * LANES:(pr + 1) * LANES]
        kml = kml_sc[:, pr * LANES:(pr + 1) * LANES]
        for hh in range(2):
            mine = half_k if hh == 0 else jnp.logical_not(half_k)
            gate = (lax.dot_general(jnp.where(mine, kmh, jnp.zeros_like(kmh)), q2, _NT,
                                    preferred_element_type=F32)
                    + lax.dot_general(jnp.where(mine, kml, jnp.zeros_like(kml)), q2, _NT,
                                      preferred_element_type=F32))
            gate = jnp.where(row < i, gate, NEG_INF)
            cnt = jnp.zeros((nbp, blk), F32)
            for m in range(nb):
                other = jnp.broadcast_to(gate[m:m + 1, :], (nbp, blk))
                tie = jnp.where(row > m, 1.0, 0.0)
                cnt = cnt + jnp.where(other > gate, 1.0, 0.0) + jnp.where(other == gate, tie, 0.0)
            chosen = jnp.where(row < i, cnt, float(MOBA_TOPK)) < float(MOBA_TOPK)
            keep_t = jnp.where(row == i, 0.0, jnp.where(chosen, 0.0, NEG_INF))
            keep_t = jnp.concatenate([keep_t, jnp.full((LANES - nbp, blk), NEG_INF, F32)], axis=0)
            qh = jnp.where(low_half if hh == 0 else jnp.logical_not(low_half), q2, jnp.zeros_like(q2))
            qaug_sc[pr, hh * blk:(hh + 1) * blk, :] = jnp.concatenate([qh, keep_t.T.astype(BF16)], axis=1)

    one_hi = jnp.where(lane == HEAD_DIM, 1.0, 0.0).astype(BF16)
    one_lo = jnp.where(lane == 0, 1.0, 0.0).astype(BF16)

    def tile(j, bias_of_head, scalar_bias, first):
        start = pl.multiple_of(j * blk, blk)
        onehot = jnp.where(lane == j, 1.0, 0.0).astype(BF16)
        for pr in range(npair):
            kj = k_ref[0, pl.ds(start, blk), pr * LANES:(pr + 1) * LANES]
            vj = v_ref[0, pl.ds(start, blk), pr * LANES:(pr + 1) * LANES]
            kaug = jnp.concatenate([kj, onehot], axis=1)
            vaug = (jnp.where(low_half, vj, one_hi), jnp.where(low_half, one_lo, vj))
            sc_pair = lax.dot_general(qaug_sc[pr], kaug, _NT, preferred_element_type=F32)
            for hh in range(2):
                h = 2 * pr + hh
                sc = sc_pair[hh * blk:(hh + 1) * blk]
                bias = bias_of_head(h)
                if scalar_bias:
                    m_cur = jnp.max(sc, axis=1, keepdims=True) + bias
                else:
                    sc = sc + bias
                    m_cur = jnp.max(sc, axis=1, keepdims=True)
                if first:
                    m_new = jnp.broadcast_to(m_cur, (blk, LANES))
                else:
                    m_prev = m_sc[h]
                    m_new = jnp.maximum(m_prev, m_cur)
                shift = m_new - bias if scalar_bias else m_new
                pexp = jnp.exp2(sc - jnp.concatenate([shift, shift], axis=1))
                pv = jnp.dot(pexp.astype(BF16), vaug[hh], preferred_element_type=F32)
                if first:
                    acc_sc[h] = pv
                else:
                    acc_sc[h] = jnp.exp2(m_prev - m_new) * acc_sc[h] + pv
                m_sc[h] = m_new

    tile(i, lambda h: bias_ref[h, 0], False, True)

    @pl.when(i >= 1)
    def _():
        tile(i - 1, lambda h: bias_ref[h, 1], False, False)

    def far(j, carry):
        tile(j, lambda h: bfar_ref[h], True, False)
        return carry

    lax.fori_loop(0, jnp.maximum(i - 1, 0), far, 0)

    for pr in range(npair):
        acc_e = acc_sc[2 * pr]
        acc_o = acc_sc[2 * pr + 1]
        out = jnp.where(low_half, acc_e / acc_e[:, HEAD_DIM:HEAD_DIM + 1], acc_o / acc_o[:, 0:1])
        o_ref[0, :, pr * LANES:(pr + 1) * LANES] = out.astype(BF16)


def _moba(q, k, v, bias, bfar):
    bsz, s, aw = q.shape
    blk = MOBA_BLOCK
    nb = s // blk
    nbp = -(-nb // 16) * 16
    heads = bias.shape[0]
    grid_spec = pltpu.PrefetchScalarGridSpec(
        num_scalar_prefetch=1,
        grid=(bsz, nb),
        in_specs=[pl.BlockSpec((1, blk, aw), lambda b, i, _: (b, i, 0)),
                  pl.BlockSpec((1, s, aw), lambda b, i, _: (b, 0, 0)),
                  pl.BlockSpec((1, s, aw), lambda b, i, _: (b, 0, 0)),
                  pl.BlockSpec(bias.shape, lambda b, i, _: (0, 0, 0, 0))],
        out_specs=pl.BlockSpec((1, blk, aw), lambda b, i, _: (b, i, 0)),
        scratch_shapes=[pltpu.VMEM((nbp, aw), BF16), pltpu.VMEM((nbp, aw), BF16),
                        pltpu.VMEM((heads // 2, 2 * blk, 2 * LANES), BF16),
                        pltpu.VMEM((heads, blk, LANES), F32), pltpu.VMEM((heads, blk, LANES), F32)],
    )
    return pl.pallas_call(
        _moba_kernel,
        out_shape=jax.ShapeDtypeStruct((bsz, s, aw), BF16),
        grid_spec=grid_spec,
        compiler_params=_params(("parallel", "arbitrary")),
        name="moba",
    )(bfar, q, k, v, bias)


def _ssm_tables(lam_re, lam_im, log_dt, b_re, b_im, c_re, c_im, d_skip):
    L = SSM_CHUNK
    g = lam_re.shape[0]
    dt = jnp.exp(log_dt)[:, None]
    lr, li = lam_re, lam_im
    mag = jnp.exp(lr * dt)
    ab_re, ab_im = mag * jnp.cos(li * dt), mag * jnp.sin(li * dt)
    den = lr * lr + li * li
    nr, ni = ab_re - 1.0, ab_im
    f_re, f_im = (nr * lr + ni * li) / den, (ni * lr - nr * li) / den
    bb_re = f_re[..., None] * b_re - f_im[..., None] * b_im
    bb_im = f_re[..., None] * b_im + f_im[..., None] * b_re
    n = jnp.arange(L + 1, dtype=F32)[:, None, None]
    pw_mag = jnp.exp(n * (lr * dt)[None])
    pw_re, pw_im = pw_mag * jnp.cos(n * (li * dt)[None]), pw_mag * jnp.sin(n * (li * dt)[None])
    ca_re = c_re[None] * pw_re[:L, :, None, :] - c_im[None] * pw_im[:L, :, None, :]
    ca_im = c_re[None] * pw_im[:L, :, None, :] + c_im[None] * pw_re[:L, :, None, :]
    hp = lax.Precision.HIGHEST
    kern = (jnp.einsum("ngcp,gpd->gdcn", ca_re, bb_re, precision=hp)
            - jnp.einsum("ngcp,gpd->gdcn", ca_im, bb_im, precision=hp))
    skip = jnp.asarray(np.eye(SSM_GROUP, dtype=np.float32))[None, :, :, None] * d_skip.reshape(g, 1, SSM_GROUP, 1)
    kern = kern + skip * jnp.asarray((np.arange(L) == 0).astype(np.float32))
    kern = kern.reshape(g, SSM_GROUP * SSM_GROUP, L)
    rev_re, rev_im = pw_re[L - 1::-1][:L], pw_im[L - 1::-1][:L]
    win_re = rev_re[..., None] * bb_re[None] - rev_im[..., None] * bb_im[None]
    win_im = rev_re[..., None] * bb_im[None] + rev_im[..., None] * bb_re[None]
    w_in = jnp.concatenate([win_re, win_im], axis=2)
    w_in = w_in.transpose(1, 3, 0, 2).reshape(g, SSM_GROUP * L, 2 * SSM_STATE)
    fw_re, fw_im = pw_re[1:], pw_im[1:]
    wo_re = c_re[None] * fw_re[:, :, None, :] - c_im[None] * fw_im[:, :, None, :]
    wo_im = -(c_re[None] * fw_im[:, :, None, :] + c_im[None] * fw_re[:, :, None, :])
    w_out = jnp.concatenate([wo_re, wo_im], axis=3)
    w_out = w_out.transpose(1, 3, 2, 0).reshape(g, 2 * SSM_STATE, SSM_GROUP * L)
    a_chunk = jnp.stack([jnp.concatenate([pw_re[L], pw_re[L]], axis=-1),
                         jnp.concatenate([-pw_im[L], pw_im[L]], axis=-1)], axis=1)
    return kern, w_in.astype(BF16), w_out.astype(BF16), a_chunk


def _ssm_kernel(u_ref, kern_ref, win_ref, wout_ref, ac_ref, y_ref, toep_sc):
    bsz, ng, s = u_ref.shape
    L = SSM_CHUNK
    nc = s // L
    s_ix = lax.broadcasted_iota(I32, (L, L), 0)
    t_ix = lax.broadcasted_iota(I32, (L, L), 1)

    def build(cp, carry):
        r0 = pl.multiple_of(cp * L, L)
        for c in range(ng):
            vec = kern_ref[0, pl.ds(cp * ng + c, 1), :]
            lagged = pltpu.roll(jnp.broadcast_to(vec, (L, L)), 0, 1, stride=1, stride_axis=0)
            toep_sc[pl.ds(r0, L), c * L:(c + 1) * L] = jnp.where(t_ix >= s_ix, lagged, 0.0).astype(BF16)
        return carry

    lax.fori_loop(0, ng, build, 0)

    u = jnp.concatenate(
        [jnp.concatenate([u_ref[:, c, ch * L:(ch + 1) * L] for c in range(ng)], axis=1) for ch in range(nc)],
        axis=0).astype(BF16)
    st = jnp.dot(u, win_ref[0], preferred_element_type=F32)
    a1 = ac_ref[0, 0:1, :]
    a2 = ac_ref[0, 1:2, :]
    state = jnp.zeros((bsz, 2 * SSM_STATE), F32)
    prevs = []
    for ch in range(nc):
        prevs.append(state)
        state = a1 * state + a2 * pltpu.roll(state, SSM_STATE, 1) + st[ch * bsz:(ch + 1) * bsz]
    prev = jnp.concatenate(prevs, axis=0).astype(BF16)
    y = (jnp.dot(u, toep_sc[...], preferred_element_type=F32)
         + jnp.dot(prev, wout_ref[0], preferred_element_type=F32))
    for ch in range(nc):
        for c in range(ng):
            y_ref[:, c, ch * L:(ch + 1) * L] = y[ch * bsz:(ch + 1) * bsz, c * L:(c + 1) * L]


def _ssm(u_t, kern, w_in, w_out, a_chunk):
    bsz, sw, s = u_t.shape
    g = sw // SSM_GROUP
    blk = pl.BlockSpec((bsz, SSM_GROUP, s), lambda j: (0, j, 0))
    per_group = lambda a: pl.BlockSpec((1,) + a.shape[1:], lambda j: (j, 0, 0))
    return pl.pallas_call(
        _ssm_kernel,
        out_shape=jax.ShapeDtypeStruct(u_t.shape, F32),
        grid=(g,),
        in_specs=[blk, per_group(kern), per_group(w_in), per_group(w_out), per_group(a_chunk)],
        out_specs=blk,
        scratch_shapes=[pltpu.VMEM((SSM_GROUP * SSM_CHUNK, SSM_GROUP * SSM_CHUNK), BF16)],
        compiler_params=_params(("parallel",)),
        name="ssm",
    )(u_t, kern, w_in, w_out, a_chunk)


def _gelu_tanh(x):
    return 0.5 * x * (1.0 + jnp.tanh(math.sqrt(2.0 / math.pi) * (x + 0.044715 * (x * x * x))))


def _merge_kernel(x_ref, att_ref, yt_ref, sga_ref, sgs_ref, mod_ref, gpost_ref, gpre_ref,
                  wao_ref, wgv_ref, wgg_ref, wmo_ref, wrt_ref, brt_ref,
                  x1_ref, h2_ref, lt_ref):
    a_br = jnp.dot(att_ref[0], wao_ref[...], preferred_element_type=F32)
    z = _gelu_tanh(yt_ref[0]).T.astype(BF16)
    s_br = (jnp.dot(z, wgv_ref[...], preferred_element_type=F32)
            * jax.nn.sigmoid(jnp.dot(z, wgg_ref[...], preferred_element_type=F32)))
    merged = sga_ref[0].astype(F32) * a_br + sgs_ref[0].astype(F32) * s_br
    mix = jnp.dot(merged.astype(BF16), wmo_ref[...], preferred_element_type=F32)
    x1 = x_ref[0] + mod_ref[0, 2:3, :] * _rms(mix, gpost_ref[...])
    x1_ref[0] = x1
    h2 = _rms(x1, gpre_ref[...]) * (1.0 + mod_ref[0, 4:5, :]) + mod_ref[0, 3:4, :]
    h2_ref[0] = _pack_pairs(h2)
    lt_ref[...] = _dot3(wrt_ref[...], h2, _NT) + brt_ref[...]


def _merge(x, att, yt, sga, sgs, mod, g_post, g_pre, w_ao, w_gv, w_gg, w_mo, w_rt, b_rt, tm):
    bsz, s, d = x.shape
    aw = att.shape[2]
    sw = yt.shape[1]
    nt = s // tm
    tspec = lambda width: pl.BlockSpec((1, tm, width), lambda b, i: (b, i, 0))
    full = lambda a: pl.BlockSpec(a.shape, lambda b, i: (0,) * a.ndim)
    return pl.pallas_call(
        _merge_kernel,
        out_shape=(jax.ShapeDtypeStruct((bsz, s, d), F32), jax.ShapeDtypeStruct((bsz, s, d // 2), I32),
                   jax.ShapeDtypeStruct((ROUTER_ROWS, bsz * s), F32)),
        grid=(bsz, nt),
        in_specs=[tspec(d), tspec(aw), pl.BlockSpec((1, sw, tm), lambda b, i: (b, 0, i)),
                  tspec(d), tspec(d),
                  pl.BlockSpec((1, mod.shape[1], d), lambda b, i: (b, 0, 0)),
                  full(g_post), full(g_pre), full(w_ao), full(w_gv), full(w_gg), full(w_mo),
                  full(w_rt), full(b_rt)],
        out_specs=(tspec(d), tspec(d // 2), pl.BlockSpec((ROUTER_ROWS, tm), lambda b, i: (0, b * nt + i))),
        compiler_params=_params(("parallel", "parallel")),
        name="merge",
    )(x, att, yt, sga, sgs, mod, g_post, g_pre, w_ao, w_gv, w_gg, w_mo, w_rt, b_rt)


def _route_kernel(lt_ref, eid_ref, wt_ref, dest_ref, cnt_ref, base_sc, pstart_sc):
    tn = lt_ref.shape[1]
    epg = EXPERTS_PER_GROUP
    ph = pl.program_id(0)
    step = pl.program_id(1)

    @pl.when((ph == 0) & (step == 0))
    def _():
        base_sc[...] = jnp.zeros_like(base_sc)

    @pl.when((ph == 1) & (step == 0))
    def _():
        total = base_sc[...]
        cnt_ref[...] = total.astype(I32)
        padded = jnp.floor((total + (MOE_BLOCK - 1.0)) * (1.0 / MOE_BLOCK)) * MOE_BLOCK
        r = lax.broadcasted_iota(I32, (N_EXPERTS, N_EXPERTS), 0)
        c = lax.broadcasted_iota(I32, (N_EXPERTS, N_EXPERTS), 1)
        before = jnp.where(c < r, 1.0, 0.0)
        pstart_sc[...] = _dot3(before, padded, _NN)
        base_sc[...] = jnp.zeros_like(base_sc)

    row8 = lax.broadcasted_iota(I32, (epg, tn), 0)
    gl = lt_ref[N_EXPERTS:N_EXPERTS + epg, :]
    gl = jnp.where(row8 < N_GROUPS, gl, -jnp.inf)
    gmax = jnp.max(gl, axis=0, keepdims=True)
    gidx = jnp.min(jnp.where(gl == gmax, row8, epg), axis=0, keepdims=True)

    el = jnp.zeros((epg, tn), F32)
    for g in range(N_GROUPS):
        el = jnp.where(gidx == g, lt_ref[g * epg:(g + 1) * epg, :], el)
    m1 = jnp.max(el, axis=0, keepdims=True)
    i1 = jnp.min(jnp.where(el == m1, row8, epg), axis=0, keepdims=True)
    el2 = jnp.where(row8 == i1, -jnp.inf, el)
    m2 = jnp.max(el2, axis=0, keepdims=True)
    i2 = jnp.min(jnp.where(el2 == m2, row8, epg), axis=0, keepdims=True)
    e1 = gidx * epg + i1
    e2 = gidx * epg + i2

    row32 = lax.broadcasted_iota(I32, (N_EXPERTS, tn), 0)
    hit1 = row32 == e1
    hit2 = row32 == e2
    onehot = jnp.where(hit1, 1.0, jnp.where(hit2, 1.0, 0.0))

    @pl.when(ph == 1)
    def _():
        g_p = 1.0 / jnp.sum(jnp.exp(gl - gmax), axis=0, keepdims=True)
        zsum = jnp.sum(jnp.exp(el - m1), axis=0, keepdims=True)
        p1 = 1.0 / zsum
        p2 = jnp.exp(m2 - m1) / zsum
        sr = lax.broadcasted_iota(I32, (tn, tn), 0)
        tc = lax.broadcasted_iota(I32, (tn, tn), 1)
        earlier = jnp.where(sr < tc, 1.0, 0.0).astype(BF16)
        place = (jnp.dot(onehot.astype(BF16), earlier, preferred_element_type=F32)
                 + base_sc[:, 0:1] + pstart_sc[:, 0:1])
        d1 = jnp.sum(jnp.where(hit1, place, 0.0), axis=0, keepdims=True)
        d2 = jnp.sum(jnp.where(hit2, place, 0.0), axis=0, keepdims=True)
        eid_ref[...] = jnp.concatenate([e1, e2], axis=0)
        wt_ref[...] = jnp.concatenate([g_p * p1 / (p1 + p2), g_p * p2 / (p1 + p2)], axis=0)
        dest_ref[...] = jnp.concatenate([d1, d2], axis=0).astype(I32)

    base_sc[...] = base_sc[...] + jnp.sum(onehot, axis=1, keepdims=True)


def _route(logits_t, tn):
    n = logits_t.shape[1]
    two = lambda dt: jax.ShapeDtypeStruct((2, n), dt)
    tspec = pl.BlockSpec((2, tn), lambda ph, i: (0, i * ph))
    return pl.pallas_call(
        _route_kernel,
        out_shape=(two(I32), two(F32), two(I32), jax.ShapeDtypeStruct((N_EXPERTS, LANES), I32)),
        grid=(2, n // tn),
        in_specs=[pl.BlockSpec((ROUTER_ROWS, tn), lambda ph, i: (0, i))],
        out_specs=(tspec, tspec, tspec, pl.BlockSpec((N_EXPERTS, LANES), lambda ph, i: (0, 0))),
        scratch_shapes=[pltpu.VMEM((N_EXPERTS, LANES), F32), pltpu.VMEM((N_EXPERTS, LANES), F32)],
        compiler_params=_params(("arbitrary", "arbitrary")),
        name="route",
    )(logits_t)


def _sc_mesh():
    return plsc.VectorSubcoreMesh(core_axis_name="c", subcore_axis_name="s",
                                  num_cores=SC_CORES, num_subcores=SC_SUBCORES)


def _sc_worker_offset(per_worker):
    return (lax.axis_index("s") * SC_CORES + lax.axis_index("c")) * per_worker


def _sc_dispatch(rows, dest0, dest1, n_rows):
    n_tok, w = rows.shape
    per_worker = n_tok // (SC_CORES * SC_SUBCORES)
    n_chunks = per_worker // SC_WINDOW

    @functools.partial(
        pl.kernel, mesh=_sc_mesh(), out_type=jax.ShapeDtypeStruct((n_rows, w), rows.dtype),
        scratch_types=[pltpu.VMEM((SC_WINDOW,), I32)] * 4 + [pltpu.VMEM((SC_WINDOW, w), rows.dtype)] * 2
        + [pltpu.SemaphoreType.DMA] * 4,
        name="sc_dispatch")
    def scatter(rows_hbm, dest0_hbm, dest1_hbm, x_hbm, i0a, i1a, i0b, i1b, rows_a, rows_b,
                lsem_a, lsem_b, ssem_a, ssem_b):
        base = _sc_worker_offset(per_worker)
        bufs = ((i0a, i1a, rows_a, lsem_a, ssem_a), (i0b, i1b, rows_b, lsem_b, ssem_b))

        def load(chunk, b):
            i0, i1, rv, lsem, _ = bufs[b]
            off = base + chunk * SC_WINDOW
            pltpu.sync_copy(dest0_hbm.at[pl.ds(off, SC_WINDOW)], i0)
            pltpu.sync_copy(dest1_hbm.at[pl.ds(off, SC_WINDOW)], i1)
            return pltpu.make_async_copy(rows_hbm.at[pl.ds(off, SC_WINDOW)], rv, lsem)

        def stores(b):
            i0, i1, rv, _, ssem = bufs[b]
            return (pltpu.make_async_copy(rv, x_hbm.at[i0], ssem),
                    pltpu.make_async_copy(rv, x_hbm.at[i1], ssem))

        load(0, 0).start()

        @pl.loop(0, n_chunks, step=2)
        def _(chunk):
            for b in range(2):
                cur = chunk + b

                @pl.when(cur >= 1)
                def _():
                    for cp in stores(1 - b):
                        cp.wait()

                @pl.when(cur + 1 < n_chunks)
                def _():
                    load(cur + 1, 1 - b).start()

                pltpu.make_async_copy(rows_hbm.at[pl.ds(base, SC_WINDOW)], bufs[b][2], bufs[b][3]).wait()
                for cp in stores(b):
                    cp.start()

        for cp in stores((n_chunks - 1) % 2):
            cp.wait()

    return scatter(rows, dest0, dest1)


def _sc_gather(table, idx):
    n_idx = idx.shape[0]
    w = table.shape[1]
    per_worker = n_idx // (SC_CORES * SC_SUBCORES)
    n_chunks = per_worker // SC_WINDOW

    @functools.partial(
        pl.kernel, mesh=_sc_mesh(), out_type=jax.ShapeDtypeStruct((n_idx, w), table.dtype),
        scratch_types=[pltpu.VMEM((SC_WINDOW,), I32)] * 2 + [pltpu.VMEM((SC_WINDOW, w), table.dtype)] * 2
        + [pltpu.SemaphoreType.DMA] * 4,
        name="sc_gather")
    def gather(table_hbm, idx_hbm, out_hbm, idx_a, idx_b, rows_a, rows_b, gsem_a, gsem_b, wsem_a, wsem_b):
        base = _sc_worker_offset(per_worker)
        bufs = ((idx_a, rows_a, gsem_a, wsem_a), (idx_b, rows_b, gsem_b, wsem_b))

        def fetch(chunk, b):
            iv, rv, gsem, _ = bufs[b]
            pltpu.sync_copy(idx_hbm.at[pl.ds(base + chunk * SC_WINDOW, SC_WINDOW)], iv)
            return pltpu.make_async_copy(table_hbm.at[iv], rv, gsem)

        def write(chunk, b):
            _, rv, _, wsem = bufs[b]
            return pltpu.make_async_copy(rv, out_hbm.at[pl.ds(base + chunk * SC_WINDOW, SC_WINDOW)], wsem)

        fetch(0, 0).start()

        @pl.loop(0, n_chunks, step=2)
        def _(chunk):
            for b in range(2):
                cur = chunk + b

                @pl.when(cur >= 1)
                def _():
                    write(cur - 1, 1 - b).wait()

                @pl.when(cur + 1 < n_chunks)
                def _():
                    fetch(cur + 1, 1 - b).start()

                pltpu.make_async_copy(table_hbm.at[bufs[b][0]], bufs[b][1], bufs[b][2]).wait()
                write(cur, b).start()

        write(n_chunks - 1, (n_chunks - 1) % 2).wait()

    return gather(table, idx)


def _expert_kernel(be_ref, nv_ref, x_ref, wg_ref, wu_ref, wd_ref, y_ref):
    rowi = lax.broadcasted_iota(I32, x_ref.shape, 0)
    lo, hi = _unpack_pairs(jnp.where(rowi < nv_ref[pl.program_id(0)], x_ref[...], 0))
    xb = jnp.concatenate([lo.astype(BF16), hi.astype(BF16)], axis=1)
    gate = jnp.dot(xb, wg_ref[0], preferred_element_type=F32)
    up = jnp.dot(xb, wu_ref[0], preferred_element_type=F32)
    hid = (gate * jax.nn.sigmoid(gate) * up).astype(BF16)
    y_ref[...] = _pack_pairs(jnp.dot(hid, wd_ref[0], preferred_element_type=F32))


def _experts(x_rows, block_e, block_valid, w_gate, w_up, w_down):
    n_blocks = block_e.shape[0]
    d = w_gate.shape[1]
    de = w_gate.shape[2]
    grid_spec = pltpu.PrefetchScalarGridSpec(
        num_scalar_prefetch=2,
        grid=(n_blocks,),
        in_specs=[pl.BlockSpec((MOE_BLOCK, d // 2), lambda i, be, nv: (i, 0)),
                  pl.BlockSpec((1, d, de), lambda i, be, nv: (be[i], 0, 0)),
                  pl.BlockSpec((1, d, de), lambda i, be, nv: (be[i], 0, 0)),
                  pl.BlockSpec((1, de, d), lambda i, be, nv: (be[i], 0, 0))],
        out_specs=pl.BlockSpec((MOE_BLOCK, d // 2), lambda i, be, nv: (i, 0)),
    )
    return pl.pallas_call(
        _expert_kernel,
        out_shape=jax.ShapeDtypeStruct((n_blocks * MOE_BLOCK, d // 2), I32),
        grid_spec=grid_spec,
        compiler_params=_params(("arbitrary",)),
        name="experts",
    )(block_e, block_valid, x_rows, w_gate, w_up, w_down)


def _combine_kernel(y0_ref, y1_ref, x1_ref, wt_ref, g2_ref, gain_ref, o_ref):
    y0 = jnp.concatenate(_unpack_pairs(y0_ref[...]), axis=1)
    y1 = jnp.concatenate(_unpack_pairs(y1_ref[...]), axis=1)
    f = wt_ref[:, 0:1] * y0 + wt_ref[:, 1:2] * y1
    o_ref[...] = x1_ref[...] + g2_ref[0] * _rms(f, gain_ref[...])


def _combine(y2, wts, x1, gate2, gain, tm, row0):
    n, d = x1.shape
    s = n // gate2.shape[0]
    nt = y2.shape[0] // (2 * tm)
    t0 = row0 // tm
    return pl.pallas_call(
        _combine_kernel,
        out_shape=jax.ShapeDtypeStruct((n, d), F32),
        grid=(nt,),
        in_specs=[pl.BlockSpec((tm, d // 2), lambda i: (i, 0)),
                  pl.BlockSpec((tm, d // 2), lambda i: (nt + i, 0)),
                  pl.BlockSpec((tm, d), lambda i: (t0 + i, 0)),
                  pl.BlockSpec((tm, 2), lambda i: (t0 + i, 0)),
                  pl.BlockSpec((1, 1, d), lambda i: ((row0 + i * tm) // s, 0, 0)),
                  pl.BlockSpec((1, d), lambda i: (0, 0))],
        out_specs=pl.BlockSpec((tm, d), lambda i: (t0 + i, 0)),
        input_output_aliases={2: 0},
        compiler_params=_params(("parallel",)),
        name="combine",
    )(y2, y2, x1, wts, gate2, gain)


def _layer(x, mod, bias, bfar, p):
    bsz, s, d = x.shape
    n_tok = bsz * s
    aw = ATT_HEADS * HEAD_DIM
    sw = p["ssm_d"].shape[0]

    w_in = p["w_in"]
    w_rest = jnp.concatenate([w_in[:, :3 * aw], w_in[:, 3 * aw + sw:]], axis=1).astype(BF16)
    w_ut = w_in[:, 3 * aw:3 * aw + sw].T.astype(BF16)
    q, k, v, u_t, sga, sgs = _inproj(x, mod, p["g_pre_mix"].reshape(1, d), w_rest, w_ut, tm=512)
    att = _moba(q, k, v, bias, bfar)
    tables = _ssm_tables(p["ssm_lambda_re"], p["ssm_lambda_im"], p["ssm_log_dt"], p["ssm_b_re"],
                         p["ssm_b_im"], p["ssm_c_re"], p["ssm_c_im"], p["ssm_d"])
    y_t = _ssm(u_t, *tables)

    w_rt = jnp.concatenate([p["w_router_expert"].T, p["w_router_group"].T,
                            jnp.zeros((ROUTER_ROWS - N_EXPERTS - N_GROUPS, d), F32)], axis=0)
    b_rt = jnp.concatenate([p["b_router_expert"], p["b_router_group"],
                            jnp.zeros((ROUTER_ROWS - N_EXPERTS - N_GROUPS,), F32)]).reshape(ROUTER_ROWS, 1)
    x1, h2, logits_t = _merge(
        x, att, y_t, sga, sgs, mod, p["g_post_mix"].reshape(1, d), p["g_pre_ffn"].reshape(1, d),
        p["w_att_out"].astype(BF16), p["w_glu_val"].astype(BF16), p["w_glu_gate"].astype(BF16),
        p["w_mix_out"].astype(BF16), w_rt, b_rt, tm=512)

    eid, wts, dest, counts = _route(logits_t, tn=1024)
    counts = counts[:, 0]
    padded = (counts + MOE_BLOCK - 1) // MOE_BLOCK * MOE_BLOCK
    pend = jnp.cumsum(padded)
    n_blocks = -(-(n_tok * 2) // MOE_BLOCK) + N_EXPERTS
    block_start = jnp.arange(n_blocks, dtype=I32) * MOE_BLOCK
    block_e = jnp.minimum(jnp.sum((pend[None, :] <= block_start[:, None]).astype(I32), axis=1), N_EXPERTS - 1)
    block_valid = jnp.clip(counts[block_e] - (block_start - (pend - padded)[block_e]), 0, MOE_BLOCK)
    x_rows = _sc_dispatch(h2.reshape(n_tok, d // 2), dest[0], dest[1], n_blocks * MOE_BLOCK)
    y_rows = _experts(x_rows, block_e, block_valid, p["w_exp_gate"].astype(BF16), p["w_exp_up"].astype(BF16),
                      p["w_exp_down"].astype(BF16))
    out = x1.reshape(n_tok, d)
    chunk = n_tok // COMBINE_CHUNKS
    for c in range(COMBINE_CHUNKS):
        y2 = _sc_gather(y_rows, dest[:, c * chunk:(c + 1) * chunk].reshape(-1))
        out = _combine(y2, wts.T, out, mod[:, 5:6, :], p["g_post_ffn"].reshape(1, d), tm=512, row0=c * chunk)
    return out.reshape(bsz, s, d)


def kernel(x, c, rel_bias, w_ada, b_ada, g_pre_mix, g_post_mix, w_in, w_att_out, ssm_lambda_re, ssm_lambda_im, ssm_log_dt, ssm_b_re, ssm_b_im, ssm_c_re, ssm_c_im, ssm_d, w_glu_val, w_glu_gate, w_mix_out, g_pre_ffn, g_post_ffn, w_router_group, b_router_group, w_router_expert, b_router_expert, w_exp_gate, w_exp_up, w_exp_down):
    layered = dict(
        w_ada=w_ada, b_ada=b_ada, g_pre_mix=g_pre_mix, g_post_mix=g_post_mix, w_in=w_in,
        w_att_out=w_att_out, ssm_lambda_re=ssm_lambda_re, ssm_lambda_im=ssm_lambda_im,
        ssm_log_dt=ssm_log_dt, ssm_b_re=ssm_b_re, ssm_b_im=ssm_b_im, ssm_c_re=ssm_c_re,
        ssm_c_im=ssm_c_im, ssm_d=ssm_d, w_glu_val=w_glu_val, w_glu_gate=w_glu_gate,
        w_mix_out=w_mix_out, g_pre_ffn=g_pre_ffn, g_post_ffn=g_post_ffn,
        w_router_group=w_router_group, b_router_group=b_router_group,
        w_router_expert=w_router_expert, b_router_expert=b_router_expert,
        w_exp_gate=w_exp_gate, w_exp_up=w_exp_up, w_exp_down=w_exp_down)
    depth = w_ada.shape[0]
    bsz, d = c.shape
    bias = _bias_tiles(rel_bias)
    far_bucket = np.unique(_t5_bucket_np(np.arange(MOBA_BLOCK + 1, max(x.shape[1], MOBA_BLOCK + 2))))
    assert far_bucket.size == 1
    bfar = rel_bias[int(far_bucket[0])] * LOG2E
    for l in range(depth):
        p = {name: a[l] for name, a in layered.items()}
        mod = _ada(c, p["w_ada"], p["b_ada"]).reshape(bsz, -1, d)
        x = _layer(x, mod, bias, bfar, p)
    return x
```

```python
import functools
import math

import numpy as np
import jax
import jax.numpy as jnp
from jax import lax
from jax.experimental import pallas as pl
from jax.experimental.pallas import tpu as pltpu
from jax.experimental.pallas import tpu_sc as plsc

F32 = jnp.float32
BF16 = jnp.bfloat16
I32 = jnp.int32

ATT_HEADS = 8
HEAD_DIM = 64
MOBA_BLOCK = 256
MOBA_TOPK = 3
NUM_BUCKETS = 32
MAX_DISTANCE = 128
SSM_GROUP = 16
SSM_STATE = 64
SSM_CHUNK = 128
N_GROUPS = 4
EXPERTS_PER_GROUP = 8
N_EXPERTS = N_GROUPS * EXPERTS_PER_GROUP
MOE_BLOCK = 512
RMS_EPS = 1e-6
NEG_INF = -1e30
LOG2E = math.log2(math.e)
LANES = 128
ROUTER_ROWS = 40
VMEM_LIMIT = 56 * 1024 * 1024
SC_CORES = 2
SC_SUBCORES = 16
SC_WINDOW = 64
COMBINE_CHUNKS = 4

_NT = (((1,), (1,)), ((), ()))
_NN = (((1,), (0,)), ((), ()))


def _params(sem, vmem=VMEM_LIMIT):
    return pltpu.CompilerParams(dimension_semantics=sem, vmem_limit_bytes=vmem)


def _split_bf16(a):
    hi = a.astype(BF16)
    lo = (a - hi.astype(F32)).astype(BF16)
    return hi, lo


def _dot3(a, b, dims):
    a_hi, a_lo = _split_bf16(a)
    b_hi, b_lo = _split_bf16(b)
    dg = functools.partial(lax.dot_general, dimension_numbers=dims, preferred_element_type=F32)
    return dg(a_hi, b_hi) + (dg(a_hi, b_lo) + dg(a_lo, b_hi))


def _pack_pairs(a):
    w = a.shape[1] // 2
    bits = pltpu.bitcast(a.astype(BF16).astype(F32), jnp.uint32)
    return pltpu.bitcast((bits[:, :w] >> 16) | (bits[:, w:] & jnp.uint32(0xFFFF0000)), I32)


def _unpack_pairs(packed):
    words = pltpu.bitcast(packed, jnp.uint32)
    lo = pltpu.bitcast(words << 16, F32)
    hi = pltpu.bitcast(words & jnp.uint32(0xFFFF0000), F32)
    return lo, hi


def _ada_kernel(c_ref, w_ref, b_ref, o_ref):
    c = c_ref[...]
    ca = c * jax.nn.sigmoid(c)
    o_ref[...] = _dot3(ca, w_ref[...], _NN) + b_ref[...]


def _ada(c, w, b):
    bsz, d = c.shape
    n = w.shape[1]
    tn = 1536
    return pl.pallas_call(
        _ada_kernel,
        out_shape=jax.ShapeDtypeStruct((bsz, n), F32),
        grid=(n // tn,),
        in_specs=[pl.BlockSpec((bsz, d), lambda j: (0, 0)),
                  pl.BlockSpec((d, tn), lambda j: (0, j)),
                  pl.BlockSpec((1, tn), lambda j: (0, j))],
        out_specs=pl.BlockSpec((bsz, tn), lambda j: (0, j)),
        compiler_params=_params(("parallel",)),
        name="ada",
    )(c, w, b.reshape(1, n))


def _t5_bucket_np(dist):
    n = np.maximum(dist, 0)
    max_exact = NUM_BUCKETS // 2
    nf = np.maximum(n, 1).astype(np.float32)
    large = max_exact + (np.log(nf / np.float32(max_exact)) / np.float32(math.log(MAX_DISTANCE / max_exact))
                         * np.float32(NUM_BUCKETS - max_exact)).astype(np.int32)
    large = np.minimum(large, NUM_BUCKETS - 1)
    return np.where(n < max_exact, n, large).astype(np.int32)


def _bias_kernel(rb_ref, bk_ref, o_ref):
    h = pl.program_id(0)
    for t in range(2):
        bk = bk_ref[t]
        acc = jnp.where(bk < 0, NEG_INF, 0.0).astype(F32)
        for b in range(NUM_BUCKETS):
            acc = jnp.where(bk == b, rb_ref[b, h] * LOG2E, acc)
        o_ref[0, t] = acc


def _bias_tiles(rel_bias):
    qi = np.arange(MOBA_BLOCK)[:, None]
    kj = np.arange(MOBA_BLOCK)[None, :]
    own = np.where(qi >= kj, _t5_bucket_np(qi - kj), -1)
    adj = _t5_bucket_np(qi - kj + MOBA_BLOCK)
    buckets = jnp.asarray(np.stack([own, adj]).astype(np.int32))
    return pl.pallas_call(
        _bias_kernel,
        out_shape=jax.ShapeDtypeStruct((ATT_HEADS, 2, MOBA_BLOCK, MOBA_BLOCK), F32),
        grid=(ATT_HEADS,),
        in_specs=[pl.BlockSpec(memory_space=pltpu.SMEM),
                  pl.BlockSpec((2, MOBA_BLOCK, MOBA_BLOCK), lambda h: (0, 0, 0))],
        out_specs=pl.BlockSpec((1, 2, MOBA_BLOCK, MOBA_BLOCK), lambda h: (h, 0, 0, 0)),
        compiler_params=_params(("parallel",)),
        name="t5_bias",
    )(rel_bias, buckets)


def _rms(x, gain):
    ms = jnp.mean(x * x, axis=-1, keepdims=True)
    return x * lax.rsqrt(ms + RMS_EPS) * gain


def _inproj_kernel(x_ref, mod_ref, g_ref, w_ref, wut_ref,
                   q_ref, k_ref, v_ref, ut_ref, sga_ref, sgs_ref):
    aw = q_ref.shape[2]
    d = x_ref.shape[2]
    x = x_ref[0]
    h = _rms(x, g_ref[...]) * (1.0 + mod_ref[0, 1:2, :]) + mod_ref[0, 0:1, :]
    hb = h.astype(BF16)

    def proj(lo, width):
        return jnp.dot(hb, w_ref[:, lo:lo + width], preferred_element_type=F32)

    q_ref[0] = (proj(0, aw) * (HEAD_DIM ** -0.5 * LOG2E)).astype(BF16)
    k_ref[0] = proj(aw, aw).astype(BF16)
    v_ref[0] = proj(2 * aw, aw).astype(BF16)
    ut_ref[0] = lax.dot_general(wut_ref[...], hb, _NT, preferred_element_type=F32)
    sga_ref[0] = jax.nn.sigmoid(proj(3 * aw, d)).astype(BF16)
    sgs_ref[0] = jax.nn.sigmoid(proj(3 * aw + d, d)).astype(BF16)


def _inproj(x, mod, gain, w_rest, w_ut, tm):
    bsz, s, d = x.shape
    aw = ATT_HEADS * HEAD_DIM
    sw = w_ut.shape[0]
    tok = lambda width, dt: jax.ShapeDtypeStruct((bsz, s, width), dt)
    tspec = lambda width: pl.BlockSpec((1, tm, width), lambda b, i: (b, i, 0))
    return pl.pallas_call(
        _inproj_kernel,
        out_shape=(tok(aw, BF16), tok(aw, BF16), tok(aw, BF16),
                   jax.ShapeDtypeStruct((bsz, sw, s), F32), tok(d, BF16), tok(d, BF16)),
        grid=(bsz, s // tm),
        in_specs=[tspec(d),
                  pl.BlockSpec((1, mod.shape[1], d), lambda b, i: (b, 0, 0)),
                  pl.BlockSpec((1, d), lambda b, i: (0, 0)),
                  pl.BlockSpec(w_rest.shape, lambda b, i: (0, 0)),
                  pl.BlockSpec(w_ut.shape, lambda b, i: (0, 0))],
        out_specs=(tspec(aw), tspec(aw), tspec(aw),
                   pl.BlockSpec((1, sw, tm), lambda b, i: (b, 0, i)), tspec(d), tspec(d)),
        compiler_params=_params(("parallel", "parallel")),
        name="inproj",
    )(x, mod, gain, w_rest, w_ut)


def _moba_kernel(bfar_ref, q_ref, k_ref, v_ref, bias_ref, o_ref,
                 kmh_sc, kml_sc, qaug_sc, m_sc, acc_sc):
    blk = MOBA_BLOCK
    rows2 = 2 * blk
    sp = pl.program_id(1)
    s = k_ref.shape[1]
    nb = s // blk
    nbp = kmh_sc.shape[0]
    npair = q_ref.shape[2] // LANES
    lane = lax.broadcasted_iota(I32, (blk, LANES), 1)
    low_half = lane < HEAD_DIM
    lane2 = lax.broadcasted_iota(I32, (rows2, LANES), 1)
    low_half2 = lane2 < HEAD_DIM

    @pl.when(sp == 0)
    def _():
        r = lax.broadcasted_iota(I32, (nbp, s), 0)
        c = lax.broadcasted_iota(I32, (nbp, s), 1)
        avg = jnp.where((c >= r * blk) & (c < (r + 1) * blk), 1.0 / blk, 0.0).astype(BF16)
        km = jnp.dot(avg, k_ref[0], preferred_element_type=F32)
        hi, lo = _split_bf16(km)
        kmh_sc[...] = hi
        kml_sc[...] = lo

    row = lax.broadcasted_iota(I32, (nbp, blk), 0)
    half_k = lax.broadcasted_iota(I32, (nbp, LANES), 1) < HEAD_DIM
    for qb in range(2):
        i = 2 * sp + qb
        for pr in range(npair):
            q2 = q_ref[0, qb * blk:(qb + 1) * blk, pr * LANES:(pr + 1) * LANES]
            kmh = kmh_sc[:, pr * LANES:(pr + 1) * LANES]
            kml = kml_sc[:, pr * LANES:(pr + 1) * LANES]
            for hh in range(2):
                mine = half_k if hh == 0 else jnp.logical_not(half_k)
                gate = (lax.dot_general(jnp.where(mine, kmh, jnp.zeros_like(kmh)), q2, _NT,
                                        preferred_element_type=F32)
                        + lax.dot_general(jnp.where(mine, kml, jnp.zeros_like(kml)), q2, _NT,
                                          preferred_element_type=F32))
                gate = jnp.where(row < i, gate, NEG_INF)
                cnt = jnp.zeros((nbp, blk), F32)
                for m in range(nb):
                    other = jnp.broadcast_to(gate[m:m + 1, :], (nbp, blk))
                    tie = jnp.where(row > m, 1.0, 0.0)
                    cnt = cnt + jnp.where(other > gate, 1.0, 0.0) + jnp.where(other == gate, tie, 0.0)
                chosen = jnp.where(row < i, cnt, float(MOBA_TOPK)) < float(MOBA_TOPK)
                keep_t = jnp.where(row == i, 0.0, jnp.where(chosen, 0.0, NEG_INF))
                keep_t = jnp.concatenate([keep_t, jnp.full((LANES - nbp, blk), NEG_INF, F32)], axis=0)
                qh = jnp.where(low_half if hh == 0 else jnp.logical_not(low_half), q2, jnp.zeros_like(q2))
                r0 = hh * rows2 + qb * blk
                qaug_sc[pr, r0:r0 + blk, :] = jnp.concatenate([qh, keep_t.T.astype(BF16)], axis=1)

    one_hi = jnp.where(lane == HEAD_DIM, 1.0, 0.0).astype(BF16)
    one_lo = jnp.where(lane == 0, 1.0, 0.0).astype(BF16)

    def tile(j, bias_of_head, scalar_bias, first):
        start = pl.multiple_of(j * blk, blk)
        onehot = jnp.where(lane == j, 1.0, 0.0).astype(BF16)
        for pr in range(npair):
            kj = k_ref[0, pl.ds(start, blk), pr * LANES:(pr + 1) * LANES]
            vj = v_ref[0, pl.ds(start, blk), pr * LANES:(pr + 1) * LANES]
            kaug = jnp.concatenate([kj, onehot], axis=1)
            vaug = (jnp.where(low_half, vj, one_hi), jnp.where(low_half, one_lo, vj))
            sc_pair = lax.dot_general(qaug_sc[pr], kaug, _NT, preferred_element_type=F32)
            for hh in range(2):
                h = 2 * pr + hh
                sc = sc_pair[hh * rows2:(hh + 1) * rows2]
                bias = bias_of_head(h)
                if scalar_bias:
                    m_cur = jnp.max(sc, axis=1, keepdims=True) + bias
                else:
                    sc = sc + bias
                    m_cur = jnp.max(sc, axis=1, keepdims=True)
                if first:
                    m_new = jnp.broadcast_to(m_cur, (rows2, LANES))
                else:
                    m_prev = m_sc[h]
                    m_new = jnp.maximum(m_prev, m_cur)
                shift = m_new - bias if scalar_bias else m_new
                pexp = jnp.exp2(sc - jnp.concatenate([shift, shift], axis=1))
                pv = jnp.dot(pexp.astype(BF16), vaug[hh], preferred_element_type=F32)
                if first:
                    acc_sc[h] = pv
                else:
                    acc_sc[h] = jnp.exp2(m_prev - m_new) * acc_sc[h] + pv
                m_sc[h] = m_new

    i0 = 2 * sp
    tile(i0, lambda h: jnp.concatenate([bias_ref[h, 0], bias_ref[h, 1]], axis=0), False, True)
    tile(i0 + 1, lambda h: jnp.concatenate([jnp.zeros((blk, blk), F32), bias_ref[h, 0]], axis=0), False, False)

    @pl.when(sp >= 1)
    def _():
        tile(i0 - 1, lambda h: jnp.concatenate([bias_ref[h, 1], jnp.full((blk, blk), bfar_ref[h], F32)], axis=0),
             False, False)

    def far(j, carry):
        tile(j, lambda h: bfar_ref[h], True, False)
        return carry

    lax.fori_loop(0, jnp.maximum(i0 - 1, 0), far, 0)

    for pr in range(npair):
        acc_e = acc_sc[2 * pr]
        acc_o = acc_sc[2 * pr + 1]
        out = jnp.where(low_half2, acc_e / acc_e[:, HEAD_DIM:HEAD_DIM + 1], acc_o / acc_o[:, 0:1])
        o_ref[0, :, pr * LANES:(pr + 1) * LANES] = out.astype(BF16)


def _moba(q, k, v, bias, bfar):
    bsz, s, aw = q.shape
    blk = MOBA_BLOCK
    nb = s // blk
    nbp = -(-nb // 16) * 16
    heads = bias.shape[0]
    grid_spec = pltpu.PrefetchScalarGridSpec(
        num_scalar_prefetch=1,
        grid=(bsz, nb // 2),
        in_specs=[pl.BlockSpec((1, 2 * blk, aw), lambda b, i, _: (b, i, 0)),
                  pl.BlockSpec((1, s, aw), lambda b, i, _: (b, 0, 0)),
                  pl.BlockSpec((1, s, aw), lambda b, i, _: (b, 0, 0)),
                  pl.BlockSpec(bias.shape, lambda b, i, _: (0, 0, 0, 0))],
        out_specs=pl.BlockSpec((1, 2 * blk, aw), lambda b, i, _: (b, i, 0)),
        scratch_shapes=[pltpu.VMEM((nbp, aw), BF16), pltpu.VMEM((nbp, aw), BF16),
                        pltpu.VMEM((heads // 2, 4 * blk, 2 * LANES), BF16),
                        pltpu.VMEM((heads, 2 * blk, LANES), F32), pltpu.VMEM((heads, 2 * blk, LANES), F32)],
    )
    return pl.pallas_call(
        _moba_kernel,
        out_shape=jax.ShapeDtypeStruct((bsz, s, aw), BF16),
        grid_spec=grid_spec,
        compiler_params=_params(("parallel", "arbitrary")),
        name="moba",
    )(bfar, q, k, v, bias)


def _ssm_tables(lam_re, lam_im, log_dt, b_re, b_im, c_re, c_im, d_skip):
    L = SSM_CHUNK
    g = lam_re.shape[0]
    dt = jnp.exp(log_dt)[:, None]
    lr, li = lam_re, lam_im
    mag = jnp.exp(lr * dt)
    ab_re, ab_im = mag * jnp.cos(li * dt), mag * jnp.sin(li * dt)
    den = lr * lr + li * li
    nr, ni = ab_re - 1.0, ab_im
    f_re, f_im = (nr * lr + ni * li) / den, (ni * lr - nr * li) / den
    bb_re = f_re[..., None] * b_re - f_im[..., None] * b_im
    bb_im = f_re[..., None] * b_im + f_im[..., None] * b_re
    n = jnp.arange(L + 1, dtype=F32)[:, None, None]
    pw_mag = jnp.exp(n * (lr * dt)[None])
    pw_re, pw_im = pw_mag * jnp.cos(n * (li * dt)[None]), pw_mag * jnp.sin(n * (li * dt)[None])
    ca_re = c_re[None] * pw_re[:L, :, None, :] - c_im[None] * pw_im[:L, :, None, :]
    ca_im = c_re[None] * pw_im[:L, :, None, :] + c_im[None] * pw_re[:L, :, None, :]
    hp = lax.Precision.HIGHEST
    kern = (jnp.einsum("ngcp,gpd->gdcn", ca_re, bb_re, precision=hp)
            - jnp.einsum("ngcp,gpd->gdcn", ca_im, bb_im, precision=hp))
    skip = jnp.asarray(np.eye(SSM_GROUP, dtype=np.float32))[None, :, :, None] * d_skip.reshape(g, 1, SSM_GROUP, 1)
    kern = kern + skip * jnp.asarray((np.arange(L) == 0).astype(np.float32))
    kern = kern.reshape(g, SSM_GROUP * SSM_GROUP, L)
    rev_re, rev_im = pw_re[L - 1::-1][:L], pw_im[L - 1::-1][:L]
    win_re = rev_re[..., None] * bb_re[None] - rev_im[..., None] * bb_im[None]
    win_im = rev_re[..., None] * bb_im[None] + rev_im[..., None] * bb_re[None]
    w_in = jnp.concatenate([win_re, win_im], axis=2)
    w_in = w_in.transpose(1, 3, 0, 2).reshape(g, SSM_GROUP * L, 2 * SSM_STATE)
    fw_re, fw_im = pw_re[1:], pw_im[1:]
    wo_re = c_re[None] * fw_re[:, :, None, :] - c_im[None] * fw_im[:, :, None, :]
    wo_im = -(c_re[None] * fw_im[:, :, None, :] + c_im[None] * fw_re[:, :, None, :])
    w_out = jnp.concatenate([wo_re, wo_im], axis=3)
    w_out = w_out.transpose(1, 3, 2, 0).reshape(g, 2 * SSM_STATE, SSM_GROUP * L)
    a_chunk = jnp.stack([jnp.concatenate([pw_re[L], pw_re[L]], axis=-1),
                         jnp.concatenate([-pw_im[L], pw_im[L]], axis=-1)], axis=1)
    return kern, w_in.astype(BF16), w_out.astype(BF16), a_chunk


def _ssm_kernel(u_ref, kern_ref, win_ref, wout_ref, ac_ref, y_ref, toep_sc):
    bsz, ng, s = u_ref.shape
    L = SSM_CHUNK
    nc = s // L
    s_ix = lax.broadcasted_iota(I32, (L, L), 0)
    t_ix = lax.broadcasted_iota(I32, (L, L), 1)

    def build(cp, carry):
        r0 = pl.multiple_of(cp * L, L)
        for c in range(ng):
            vec = kern_ref[0, pl.ds(cp * ng + c, 1), :]
            lagged = pltpu.roll(jnp.broadcast_to(vec, (L, L)), 0, 1, stride=1, stride_axis=0)
            toep_sc[pl.ds(r0, L), c * L:(c + 1) * L] = jnp.where(t_ix >= s_ix, lagged, 0.0).astype(BF16)
        return carry

    lax.fori_loop(0, ng, build, 0)

    u = jnp.concatenate(
        [jnp.concatenate([u_ref[:, c, ch * L:(ch + 1) * L] for c in range(ng)], axis=1) for ch in range(nc)],
        axis=0).astype(BF16)
    st = jnp.dot(u, win_ref[0], preferred_element_type=F32)
    a1 = ac_ref[0, 0:1, :]
    a2 = ac_ref[0, 1:2, :]
    state = jnp.zeros((bsz, 2 * SSM_STATE), F32)
    prevs = []
    for ch in range(nc):
        prevs.append(state)
        state = a1 * state + a2 * pltpu.roll(state, SSM_STATE, 1) + st[ch * bsz:(ch + 1) * bsz]
    prev = jnp.concatenate(prevs, axis=0).astype(BF16)
    y = (jnp.dot(u, toep_sc[...], preferred_element_type=F32)
         + jnp.dot(prev, wout_ref[0], preferred_element_type=F32))
    for ch in range(nc):
        for c in range(ng):
            y_ref[:, c, ch * L:(ch + 1) * L] = y[ch * bsz:(ch + 1) * bsz, c * L:(c + 1) * L]


def _ssm(u_t, kern, w_in, w_out, a_chunk):
    bsz, sw, s = u_t.shape
    g = sw // SSM_GROUP
    blk = pl.BlockSpec((bsz, SSM_GROUP, s), lambda j: (0, j, 0))
    per_group = lambda a: pl.BlockSpec((1,) + a.shape[1:], lambda j: (j, 0, 0))
    return pl.pallas_call(
        _ssm_kernel,
        out_shape=jax.ShapeDtypeStruct(u_t.shape, F32),
        grid=(g,),
        in_specs=[blk, per_group(kern), per_group(w_in), per_group(w_out), per_group(a_chunk)],
        out_specs=blk,
        scratch_shapes=[pltpu.VMEM((SSM_GROUP * SSM_CHUNK, SSM_GROUP * SSM_CHUNK), BF16)],
        compiler_params=_params(("parallel",)),
        name="ssm",
    )(u_t, kern, w_in, w_out, a_chunk)


def _gelu_tanh(x):
    return 0.5 * x * (1.0 + jnp.tanh(math.sqrt(2.0 / math.pi) * (x + 0.044715 * (x * x * x))))


def _merge_kernel(x_ref, att_ref, yt_ref, sga_ref, sgs_ref, mod_ref, gpost_ref, gpre_ref,
                  wao_ref, wgv_ref, wgg_ref, wmo_ref, wrt_ref, brt_ref,
                  x1_ref, h2_ref, lt_ref):
    a_br = jnp.dot(att_ref[0], wao_ref[...], preferred_element_type=F32)
    z = _gelu_tanh(yt_ref[0]).T.astype(BF16)
    s_br = (jnp.dot(z, wgv_ref[...], preferred_element_type=F32)
            * jax.nn.sigmoid(jnp.dot(z, wgg_ref[...], preferred_element_type=F32)))
    merged = sga_ref[0].astype(F32) * a_br + sgs_ref[0].astype(F32) * s_br
    mix = jnp.dot(merged.astype(BF16), wmo_ref[...], preferred_element_type=F32)
    x1 = x_ref[0] + mod_ref[0, 2:3, :] * _rms(mix, gpost_ref[...])
    x1_ref[0] = x1
    h2 = _rms(x1, gpre_ref[...]) * (1.0 + mod_ref[0, 4:5, :]) + mod_ref[0, 3:4, :]
    h2_ref[0] = _pack_pairs(h2)
    lt_ref[...] = _dot3(wrt_ref[...], h2, _NT) + brt_ref[...]


def _merge(x, att, yt, sga, sgs, mod, g_post, g_pre, w_ao, w_gv, w_gg, w_mo, w_rt, b_rt, tm):
    bsz, s, d = x.shape
    aw = att.shape[2]
    sw = yt.shape[1]
    nt = s // tm
    tspec = lambda width: pl.BlockSpec((1, tm, width), lambda b, i: (b, i, 0))
    full = lambda a: pl.BlockSpec(a.shape, lambda b, i: (0,) * a.ndim)
    return pl.pallas_call(
        _merge_kernel,
        out_shape=(jax.ShapeDtypeStruct((bsz, s, d), F32), jax.ShapeDtypeStruct((bsz, s, d // 2), I32),
                   jax.ShapeDtypeStruct((ROUTER_ROWS, bsz * s), F32)),
        grid=(bsz, nt),
        in_specs=[tspec(d), tspec(aw), pl.BlockSpec((1, sw, tm), lambda b, i: (b, 0, i)),
                  tspec(d), tspec(d),
                  pl.BlockSpec((1, mod.shape[1], d), lambda b, i: (b, 0, 0)),
                  full(g_post), full(g_pre), full(w_ao), full(w_gv), full(w_gg), full(w_mo),
                  full(w_rt), full(b_rt)],
        out_specs=(tspec(d), tspec(d // 2), pl.BlockSpec((ROUTER_ROWS, tm), lambda b, i: (0, b * nt + i))),
        compiler_params=_params(("parallel", "parallel")),
        name="merge",
    )(x, att, yt, sga, sgs, mod, g_post, g_pre, w_ao, w_gv, w_gg, w_mo, w_rt, b_rt)


def _route_kernel(lt_ref, eid_ref, wt_ref, dest_ref, cnt_ref, base_sc, pstart_sc):
    tn = lt_ref.shape[1]
    epg = EXPERTS_PER_GROUP
    ph = pl.program_id(0)
    step = pl.program_id(1)

    @pl.when((ph == 0) & (step == 0))
    def _():
        base_sc[...] = jnp.zeros_like(base_sc)

    @pl.when((ph == 1) & (step == 0))
    def _():
        total = base_sc[...]
        cnt_ref[...] = total.astype(I32)
        padded = jnp.floor((total + (MOE_BLOCK - 1.0)) * (1.0 / MOE_BLOCK)) * MOE_BLOCK
        r = lax.broadcasted_iota(I32, (N_EXPERTS, N_EXPERTS), 0)
        c = lax.broadcasted_iota(I32, (N_EXPERTS, N_EXPERTS), 1)
        before = jnp.where(c < r, 1.0, 0.0)
        pstart_sc[...] = _dot3(before, padded, _NN)
        base_sc[...] = jnp.zeros_like(base_sc)

    row8 = lax.broadcasted_iota(I32, (epg, tn), 0)
    gl = lt_ref[N_EXPERTS:N_EXPERTS + epg, :]
    gl = jnp.where(row8 < N_GROUPS, gl, -jnp.inf)
    gmax = jnp.max(gl, axis=0, keepdims=True)
    gidx = jnp.min(jnp.where(gl == gmax, row8, epg), axis=0, keepdims=True)

    el = jnp.zeros((epg, tn), F32)
    for g in range(N_GROUPS):
        el = jnp.where(gidx == g, lt_ref[g * epg:(g + 1) * epg, :], el)
    m1 = jnp.max(el, axis=0, keepdims=True)
    i1 = jnp.min(jnp.where(el == m1, row8, epg), axis=0, keepdims=True)
    el2 = jnp.where(row8 == i1, -jnp.inf, el)
    m2 = jnp.max(el2, axis=0, keepdims=True)
    i2 = jnp.min(jnp.where(el2 == m2, row8, epg), axis=0, keepdims=True)
    e1 = gidx * epg + i1
    e2 = gidx * epg + i2

    row32 = lax.broadcasted_iota(I32, (N_EXPERTS, tn), 0)
    hit1 = row32 == e1
    hit2 = row32 == e2
    onehot = jnp.where(hit1, 1.0, jnp.where(hit2, 1.0, 0.0))

    @pl.when(ph == 1)
    def _():
        g_p = 1.0 / jnp.sum(jnp.exp(gl - gmax), axis=0, keepdims=True)
        zsum = jnp.sum(jnp.exp(el - m1), axis=0, keepdims=True)
        p1 = 1.0 / zsum
        p2 = jnp.exp(m2 - m1) / zsum
        sr = lax.broadcasted_iota(I32, (tn, tn), 0)
        tc = lax.broadcasted_iota(I32, (tn, tn), 1)
        earlier = jnp.where(sr < tc, 1.0, 0.0).astype(BF16)
        place = (jnp.dot(onehot.astype(BF16), earlier, preferred_element_type=F32)
                 + base_sc[:, 0:1] + pstart_sc[:, 0:1])
        d1 = jnp.sum(jnp.where(hit1, place, 0.0), axis=0, keepdims=True)
        d2 = jnp.sum(jnp.where(hit2, place, 0.0), axis=0, keepdims=True)
        eid_ref[...] = jnp.concatenate([e1, e2], axis=0)
        wt_ref[...] = jnp.concatenate([g_p * p1 / (p1 + p2), g_p * p2 / (p1 + p2)], axis=0)
        dest_ref[...] = jnp.concatenate([d1, d2], axis=0).astype(I32)

    base_sc[...] = base_sc[...] + jnp.sum(onehot, axis=1, keepdims=True)


def _route(logits_t, tn):
    n = logits_t.shape[1]
    two = lambda dt: jax.ShapeDtypeStruct((2, n), dt)
    tspec = pl.BlockSpec((2, tn), lambda ph, i: (0, i * ph))
    return pl.pallas_call(
        _route_kernel,
        out_shape=(two(I32), two(F32), two(I32), jax.ShapeDtypeStruct((N_EXPERTS, LANES), I32)),
        grid=(2, n // tn),
        in_specs=[pl.BlockSpec((ROUTER_ROWS, tn), lambda ph, i: (0, i))],
        out_specs=(tspec, tspec, tspec, pl.BlockSpec((N_EXPERTS, LANES), lambda ph, i: (0, 0))),
        scratch_shapes=[pltpu.VMEM((N_EXPERTS, LANES), F32), pltpu.VMEM((N_EXPERTS, LANES), F32)],
        compiler_params=_params(("arbitrary", "arbitrary")),
        name="route",
    )(logits_t)


def _sc_mesh():
    return plsc.VectorSubcoreMesh(core_axis_name="c", subcore_axis_name="s",
                                  num_cores=SC_CORES, num_subcores=SC_SUBCORES)


def _sc_worker_offset(per_worker):
    return (lax.axis_index("s") * SC_CORES + lax.axis_index("c")) * per_worker


def _sc_dispatch(rows, dest0, dest1, n_rows):
    n_tok, w = rows.shape
    per_worker = n_tok // (SC_CORES * SC_SUBCORES)
    n_chunks = per_worker // SC_WINDOW

    @functools.partial(
        pl.kernel, mesh=_sc_mesh(), out_type=jax.ShapeDtypeStruct((n_rows, w), rows.dtype),
        scratch_types=[pltpu.VMEM((SC_WINDOW,), I32)] * 4 + [pltpu.VMEM((SC_WINDOW, w), rows.dtype)] * 2
        + [pltpu.SemaphoreType.DMA] * 4,
        name="sc_dispatch")
    def scatter(rows_hbm, dest0_hbm, dest1_hbm, x_hbm, i0a, i1a, i0b, i1b, rows_a, rows_b,
                lsem_a, lsem_b, ssem_a, ssem_b):
        base = _sc_worker_offset(per_worker)
        bufs = ((i0a, i1a, rows_a, lsem_a, ssem_a), (i0b, i1b, rows_b, lsem_b, ssem_b))

        def load(chunk, b):
            i0, i1, rv, lsem, _ = bufs[b]
            off = base + chunk * SC_WINDOW
            pltpu.sync_copy(dest0_hbm.at[pl.ds(off, SC_WINDOW)], i0)
            pltpu.sync_copy(dest1_hbm.at[pl.ds(off, SC_WINDOW)], i1)
            return pltpu.make_async_copy(rows_hbm.at[pl.ds(off, SC_WINDOW)], rv, lsem)

        def stores(b):
            i0, i1, rv, _, ssem = bufs[b]
            return (pltpu.make_async_copy(rv, x_hbm.at[i0], ssem),
                    pltpu.make_async_copy(rv, x_hbm.at[i1], ssem))

        load(0, 0).start()

        @pl.loop(0, n_chunks, step=2)
        def _(chunk):
            for b in range(2):
                cur = chunk + b

                @pl.when(cur >= 1)
                def _():
                    for cp in stores(1 - b):
                        cp.wait()

                @pl.when(cur + 1 < n_chunks)
                def _():
                    load(cur + 1, 1 - b).start()

                pltpu.make_async_copy(rows_hbm.at[pl.ds(base, SC_WINDOW)], bufs[b][2], bufs[b][3]).wait()
                for cp in stores(b):
                    cp.start()

        for cp in stores((n_chunks - 1) % 2):
            cp.wait()

    return scatter(rows, dest0, dest1)


def _sc_gather(table, idx):
    n_idx = idx.shape[0]
    w = table.shape[1]
    per_worker = n_idx // (SC_CORES * SC_SUBCORES)
    n_chunks = per_worker // SC_WINDOW

    @functools.partial(
        pl.kernel, mesh=_sc_mesh(), out_type=jax.ShapeDtypeStruct((n_idx, w), table.dtype),
        scratch_types=[pltpu.VMEM((SC_WINDOW,), I32)] * 2 + [pltpu.VMEM((SC_WINDOW, w), table.dtype)] * 2
        + [pltpu.SemaphoreType.DMA] * 4,
        name="sc_gather")
    def gather(table_hbm, idx_hbm, out_hbm, idx_a, idx_b, rows_a, rows_b, gsem_a, gsem_b, wsem_a, wsem_b):
        base = _sc_worker_offset(per_worker)
        bufs = ((idx_a, rows_a, gsem_a, wsem_a), (idx_b, rows_b, gsem_b, wsem_b))

        def fetch(chunk, b):
            iv, rv, gsem, _ = bufs[b]
            pltpu.sync_copy(idx_hbm.at[pl.ds(base + chunk * SC_WINDOW, SC_WINDOW)], iv)
            return pltpu.make_async_copy(table_hbm.at[iv], rv, gsem)

        def write(chunk, b):
            _, rv, _, wsem = bufs[b]
            return pltpu.make_async_copy(rv, out_hbm.at[pl.ds(base + chunk * SC_WINDOW, SC_WINDOW)], wsem)

        fetch(0, 0).start()

        @pl.loop(0, n_chunks, step=2)
        def _(chunk):
            for b in range(2):
                cur = chunk + b

                @pl.when(cur >= 1)
                def _():
                    write(cur - 1, 1 - b).wait()

                @pl.when(cur + 1 < n_chunks)
                def _():
                    fetch(cur + 1, 1 - b).start()

                pltpu.make_async_copy(table_hbm.at[bufs[b][0]], bufs[b][1], bufs[b][2]).wait()
                write(cur, b).start()

        write(n_chunks - 1, (n_chunks - 1) % 2).wait()

    return gather(table, idx)


def _expert_kernel(be_ref, nv_ref, x_ref, wg_ref, wu_ref, wd_ref, y_ref):
    rowi = lax.broadcasted_iota(I32, x_ref.shape, 0)
    lo, hi = _unpack_pairs(jnp.where(rowi < nv_ref[pl.program_id(0)], x_ref[...], 0))
    xb = jnp.concatenate([lo.astype(BF16), hi.astype(BF16)], axis=1)
    gate = jnp.dot(xb, wg_ref[0], preferred_element_type=F32)
    up = jnp.dot(xb, wu_ref[0], preferred_element_type=F32)
    hid = (gate * jax.nn.sigmoid(gate) * up).astype(BF16)
    y_ref[...] = _pack_pairs(jnp.dot(hid, wd_ref[0], preferred_element_type=F32))


def _experts(x_rows, block_e, block_valid, w_gate, w_up, w_down):
    n_blocks = block_e.shape[0]
    d = w_gate.shape[1]
    de = w_gate.shape[2]
    grid_spec = pltpu.PrefetchScalarGridSpec(
        num_scalar_prefetch=2,
        grid=(n_blocks,),
        in_specs=[pl.BlockSpec((MOE_BLOCK, d // 2), lambda i, be, nv: (i, 0)),
                  pl.BlockSpec((1, d, de), lambda i, be, nv: (be[i], 0, 0)),
                  pl.BlockSpec((1, d, de), lambda i, be, nv: (be[i], 0, 0)),
                  pl.BlockSpec((1, de, d), lambda i, be, nv: (be[i], 0, 0))],
        out_specs=pl.BlockSpec((MOE_BLOCK, d // 2), lambda i, be, nv: (i, 0)),
    )
    return pl.pallas_call(
        _expert_kernel,
        out_shape=jax.ShapeDtypeStruct((n_blocks * MOE_BLOCK, d // 2), I32),
        grid_spec=grid_spec,
        compiler_params=_params(("arbitrary",)),
        name="experts",
    )(block_e, block_valid, x_rows, w_gate, w_up, w_down)


def _combine_kernel(y0_ref, y1_ref, x1_ref, wt_ref, g2_ref, gain_ref, o_ref):
    y0 = jnp.concatenate(_unpack_pairs(y0_ref[...]), axis=1)
    y1 = jnp.concatenate(_unpack_pairs(y1_ref[...]), axis=1)
    f = wt_ref[:, 0:1] * y0 + wt_ref[:, 1:2] * y1
    o_ref[...] = x1_ref[...] + g2_ref[0] * _rms(f, gain_ref[...])


def _combine(y2, wts, x1, gate2, gain, tm, row0):
    n, d = x1.shape
    s = n // gate2.shape[0]
    nt = y2.shape[0] // (2 * tm)
    t0 = row0 // tm
    return pl.pallas_call(
        _combine_kernel,
        out_shape=jax.ShapeDtypeStruct((n, d), F32),
        grid=(nt,),
        in_specs=[pl.BlockSpec((tm, d // 2), lambda i: (i, 0)),
                  pl.BlockSpec((tm, d // 2), lambda i: (nt + i, 0)),
                  pl.BlockSpec((tm, d), lambda i: (t0 + i, 0)),
                  pl.BlockSpec((tm, 2), lambda i: (t0 + i, 0)),
                  pl.BlockSpec((1, 1, d), lambda i: ((row0 + i * tm) // s, 0, 0)),
                  pl.BlockSpec((1, d), lambda i: (0, 0))],
        out_specs=pl.BlockSpec((tm, d), lambda i: (t0 + i, 0)),
        input_output_aliases={2: 0},
        compiler_params=_params(("parallel",)),
        name="combine",
    )(y2, y2, x1, wts, gate2, gain)


def _layer(x, mod, bias, bfar, p):
    bsz, s, d = x.shape
    n_tok = bsz * s
    aw = ATT_HEADS * HEAD_DIM
    sw = p["ssm_d"].shape[0]

    w_in = p["w_in"]
    w_rest = jnp.concatenate([w_in[:, :3 * aw], w_in[:, 3 * aw + sw:]], axis=1).astype(BF16)
    w_ut = w_in[:, 3 * aw:3 * aw + sw].T.astype(BF16)
    q, k, v, u_t, sga, sgs = _inproj(x, mod, p["g_pre_mix"].reshape(1, d), w_rest, w_ut, tm=512)
    att = _moba(q, k, v, bias, bfar)
    tables = _ssm_tables(p["ssm_lambda_re"], p["ssm_lambda_im"], p["ssm_log_dt"], p["ssm_b_re"],
                         p["ssm_b_im"], p["ssm_c_re"], p["ssm_c_im"], p["ssm_d"])
    y_t = _ssm(u_t, *tables)

    w_rt = jnp.concatenate([p["w_router_expert"].T, p["w_router_group"].T,
                            jnp.zeros((ROUTER_ROWS - N_EXPERTS - N_GROUPS, d), F32)], axis=0)
    b_rt = jnp.concatenate([p["b_router_expert"], p["b_router_group"],
                            jnp.zeros((ROUTER_ROWS - N_EXPERTS - N_GROUPS,), F32)]).reshape(ROUTER_ROWS, 1)
    x1, h2, logits_t = _merge(
        x, att, y_t, sga, sgs, mod, p["g_post_mix"].reshape(1, d), p["g_pre_ffn"].reshape(1, d),
        p["w_att_out"].astype(BF16), p["w_glu_val"].astype(BF16), p["w_glu_gate"].astype(BF16),
        p["w_mix_out"].astype(BF16), w_rt, b_rt, tm=512)

    eid, wts, dest, counts = _route(logits_t, tn=1024)
    counts = counts[:, 0]
    padded = (counts + MOE_BLOCK - 1) // MOE_BLOCK * MOE_BLOCK
    pend = jnp.cumsum(padded)
    n_blocks = -(-(n_tok * 2) // MOE_BLOCK) + N_EXPERTS
    block_start = jnp.arange(n_blocks, dtype=I32) * MOE_BLOCK
    block_e = jnp.minimum(jnp.sum((pend[None, :] <= block_start[:, None]).astype(I32), axis=1), N_EXPERTS - 1)
    block_valid = jnp.clip(counts[block_e] - (block_start - (pend - padded)[block_e]), 0, MOE_BLOCK)
    x_rows = _sc_dispatch(h2.reshape(n_tok, d // 2), dest[0], dest[1], n_blocks * MOE_BLOCK)
    y_rows = _experts(x_rows, block_e, block_valid, p["w_exp_gate"].astype(BF16), p["w_exp_up"].astype(BF16),
                      p["w_exp_down"].astype(BF16))
    out = x1.reshape(n_tok, d)
    chunk = n_tok // COMBINE_CHUNKS
    for c in range(COMBINE_CHUNKS):
        y2 = _sc_gather(y_rows, dest[:, c * chunk:(c + 1) * chunk].reshape(-1))
        out = _combine(y2, wts.T, out, mod[:, 5:6, :], p["g_post_ffn"].reshape(1, d), tm=512, row0=c * chunk)
    return out.reshape(bsz, s, d)


def kernel(x, c, rel_bias, w_ada, b_ada, g_pre_mix, g_post_mix, w_in, w_att_out, ssm_lambda_re, ssm_lambda_im, ssm_log_dt, ssm_b_re, ssm_b_im, ssm_c_re, ssm_c_im, ssm_d, w_glu_val, w_glu_gate, w_mix_out, g_pre_ffn, g_post_ffn, w_router_group, b_router_group, w_router_expert, b_router_expert, w_exp_gate, w_exp_up, w_exp_down):
    layered = dict(
        w_ada=w_ada, b_ada=b_ada, g_pre_mix=g_pre_mix, g_post_mix=g_post_mix, w_in=w_in,
        w_att_out=w_att_out, ssm_lambda_re=ssm_lambda_re, ssm_lambda_im=ssm_lambda_im,
        ssm_log_dt=ssm_log_dt, ssm_b_re=ssm_b_re, ssm_b_im=ssm_b_im, ssm_c_re=ssm_c_re,
        ssm_c_im=ssm_c_im, ssm_d=ssm_d, w_glu_val=w_glu_val, w_glu_gate=w_glu_gate,
        w_mix_out=w_mix_out, g_pre_ffn=g_pre_ffn, g_post_ffn=g_post_ffn,
        w_router_group=w_router_group, b_router_group=b_router_group,
        w_router_expert=w_router_expert, b_router_expert=b_router_expert,
        w_exp_gate=w_exp_gate, w_exp_up=w_exp_up, w_exp_down=w_exp_down)
    depth = w_ada.shape[0]
    bsz, d = c.shape
    bias = _bias_tiles(rel_bias)
    far_bucket = np.unique(_t5_bucket_np(np.arange(MOBA_BLOCK + 1, max(x.shape[1], MOBA_BLOCK + 2))))
    assert far_bucket.size == 1
    bfar = rel_bias[int(far_bucket[0])] * LOG2E
    for l in range(depth):
        p = {name: a[l] for name, a in layered.items()}
        mod = _ada(c, p["w_ada"], p["b_ada"]).reshape(bsz, -1, d)
        x = _layer(x, mod, bias, bfar, p)
    return x
```

```python
import functools
import math

import numpy as np
import jax
import jax.numpy as jnp
from jax import lax
from jax.experimental import pallas as pl
from jax.experimental.pallas import tpu as pltpu
from jax.experimental.pallas import tpu_sc as plsc

F32 = jnp.float32
BF16 = jnp.bfloat16
I32 = jnp.int32

ATT_HEADS = 8
HEAD_DIM = 64
MOBA_BLOCK = 256
MOBA_TOPK = 3
MOBA_QBLOCKS = 4
NUM_BUCKETS = 32
MAX_DISTANCE = 128
SSM_GROUP = 16
SSM_STATE = 64
SSM_CHUNK = 128
N_GROUPS = 4
EXPERTS_PER_GROUP = 8
N_EXPERTS = N_GROUPS * EXPERTS_PER_GROUP
MOE_BLOCK = 512
RMS_EPS = 1e-6
NEG_INF = -1e30
LOG2E = math.log2(math.e)
LANES = 128
ROUTER_ROWS = 40
VMEM_LIMIT = 56 * 1024 * 1024
SC_CORES = 2
SC_SUBCORES = 16
SC_WINDOW = 64
COMBINE_CHUNKS = 4

_NT = (((1,), (1,)), ((), ()))
_NN = (((1,), (0,)), ((), ()))


def _params(sem, vmem=VMEM_LIMIT):
    return pltpu.CompilerParams(dimension_semantics=sem, vmem_limit_bytes=vmem)


def _split_bf16(a):
    hi = a.astype(BF16)
    lo = (a - hi.astype(F32)).astype(BF16)
    return hi, lo


def _dot3(a, b, dims):
    a_hi, a_lo = _split_bf16(a)
    b_hi, b_lo = _split_bf16(b)
    dg = functools.partial(lax.dot_general, dimension_numbers=dims, preferred_element_type=F32)
    return dg(a_hi, b_hi) + (dg(a_hi, b_lo) + dg(a_lo, b_hi))


def _pack_pairs(a):
    w = a.shape[1] // 2
    bits = pltpu.bitcast(a.astype(BF16).astype(F32), jnp.uint32)
    return pltpu.bitcast((bits[:, :w] >> 16) | (bits[:, w:] & jnp.uint32(0xFFFF0000)), I32)


def _unpack_pairs(packed):
    words = pltpu.bitcast(packed, jnp.uint32)
    lo = pltpu.bitcast(words << 16, F32)
    hi = pltpu.bitcast(words & jnp.uint32(0xFFFF0000), F32)
    return lo, hi


def _ada_kernel(c_ref, w_ref, b_ref, o_ref):
    c = c_ref[...]
    ca = c * jax.nn.sigmoid(c)
    o_ref[...] = _dot3(ca, w_ref[...], _NN) + b_ref[...]


def _ada(c, w, b):
    bsz, d = c.shape
    n = w.shape[1]
    tn = 1536
    return pl.pallas_call(
        _ada_kernel,
        out_shape=jax.ShapeDtypeStruct((bsz, n), F32),
        grid=(n // tn,),
        in_specs=[pl.BlockSpec((bsz, d), lambda j: (0, 0)),
                  pl.BlockSpec((d, tn), lambda j: (0, j)),
                  pl.BlockSpec((1, tn), lambda j: (0, j))],
        out_specs=pl.BlockSpec((bsz, tn), lambda j: (0, j)),
        compiler_params=_params(("parallel",)),
        name="ada",
    )(c, w, b.reshape(1, n))


def _t5_bucket_np(dist):
    n = np.maximum(dist, 0)
    max_exact = NUM_BUCKETS // 2
    nf = np.maximum(n, 1).astype(np.float32)
    large = max_exact + (np.log(nf / np.float32(max_exact)) / np.float32(math.log(MAX_DISTANCE / max_exact))
                         * np.float32(NUM_BUCKETS - max_exact)).astype(np.int32)
    large = np.minimum(large, NUM_BUCKETS - 1)
    return np.where(n < max_exact, n, large).astype(np.int32)


def _bias_kernel(rb_ref, bk_ref, o_ref):
    h = pl.program_id(0)
    for t in range(2):
        bk = bk_ref[t]
        acc = jnp.where(bk < 0, NEG_INF, 0.0).astype(F32)
        for b in range(NUM_BUCKETS):
            acc = jnp.where(bk == b, rb_ref[b, h] * LOG2E, acc)
        o_ref[0, t] = acc


def _bias_tiles(rel_bias):
    qi = np.arange(MOBA_BLOCK)[:, None]
    kj = np.arange(MOBA_BLOCK)[None, :]
    own = np.where(qi >= kj, _t5_bucket_np(qi - kj), -1)
    adj = _t5_bucket_np(qi - kj + MOBA_BLOCK)
    buckets = jnp.asarray(np.stack([own, adj]).astype(np.int32))
    return pl.pallas_call(
        _bias_kernel,
        out_shape=jax.ShapeDtypeStruct((ATT_HEADS, 2, MOBA_BLOCK, MOBA_BLOCK), F32),
        grid=(ATT_HEADS,),
        in_specs=[pl.BlockSpec(memory_space=pltpu.SMEM),
                  pl.BlockSpec((2, MOBA_BLOCK, MOBA_BLOCK), lambda h: (0, 0, 0))],
        out_specs=pl.BlockSpec((1, 2, MOBA_BLOCK, MOBA_BLOCK), lambda h: (h, 0, 0, 0)),
        compiler_params=_params(("parallel",)),
        name="t5_bias",
    )(rel_bias, buckets)


def _rms(x, gain):
    ms = jnp.mean(x * x, axis=-1, keepdims=True)
    return x * lax.rsqrt(ms + RMS_EPS) * gain


def _inproj_kernel(x_ref, mod_ref, g_ref, w_ref, wut_ref,
                   q_ref, k_ref, v_ref, ut_ref, sga_ref, sgs_ref):
    aw = q_ref.shape[2]
    d = x_ref.shape[2]
    x = x_ref[0]
    h = _rms(x, g_ref[...]) * (1.0 + mod_ref[0, 1:2, :]) + mod_ref[0, 0:1, :]
    hb = h.astype(BF16)

    def proj(lo, width):
        return jnp.dot(hb, w_ref[:, lo:lo + width], preferred_element_type=F32)

    q_ref[0] = (proj(0, aw) * (HEAD_DIM ** -0.5 * LOG2E)).astype(BF16)
    k_ref[0] = proj(aw, aw).astype(BF16)
    v_ref[0] = proj(2 * aw, aw).astype(BF16)
    ut_ref[0] = lax.dot_general(wut_ref[...], hb, _NT, preferred_element_type=F32)
    sga_ref[0] = jax.nn.sigmoid(proj(3 * aw, d)).astype(BF16)
    sgs_ref[0] = jax.nn.sigmoid(proj(3 * aw + d, d)).astype(BF16)


def _inproj(x, mod, gain, w_rest, w_ut, tm):
    bsz, s, d = x.shape
    aw = ATT_HEADS * HEAD_DIM
    sw = w_ut.shape[0]
    tok = lambda width, dt: jax.ShapeDtypeStruct((bsz, s, width), dt)
    tspec = lambda width: pl.BlockSpec((1, tm, width), lambda b, i: (b, i, 0))
    return pl.pallas_call(
        _inproj_kernel,
        out_shape=(tok(aw, BF16), tok(aw, BF16), tok(aw, BF16),
                   jax.ShapeDtypeStruct((bsz, sw, s), F32), tok(d, BF16), tok(d, BF16)),
        grid=(bsz, s // tm),
        in_specs=[tspec(d),
                  pl.BlockSpec((1, mod.shape[1], d), lambda b, i: (b, 0, 0)),
                  pl.BlockSpec((1, d), lambda b, i: (0, 0)),
                  pl.BlockSpec(w_rest.shape, lambda b, i: (0, 0)),
                  pl.BlockSpec(w_ut.shape, lambda b, i: (0, 0))],
        out_specs=(tspec(aw), tspec(aw), tspec(aw),
                   pl.BlockSpec((1, sw, tm), lambda b, i: (b, 0, i)), tspec(d), tspec(d)),
        compiler_params=_params(("parallel", "parallel")),
        name="inproj",
    )(x, mod, gain, w_rest, w_ut)


def _moba_kernel(bfar_ref, q_ref, k_ref, v_ref, bias_ref, o_ref,
                 kmh_sc, kml_sc, qaug_sc, m_sc, acc_sc):
    blk = MOBA_BLOCK
    nq = MOBA_QBLOCKS
    rows_h = nq * blk
    sp = pl.program_id(1)
    i_base = nq * sp
    s = k_ref.shape[1]
    nb = s // blk
    nbp = kmh_sc.shape[0]
    npair = q_ref.shape[2] // LANES
    lane = lax.broadcasted_iota(I32, (blk, LANES), 1)
    low_half = lane < HEAD_DIM

    @pl.when(sp == 0)
    def _():
        r = lax.broadcasted_iota(I32, (nbp, s), 0)
        c = lax.broadcasted_iota(I32, (nbp, s), 1)
        avg = jnp.where((c >= r * blk) & (c < (r + 1) * blk), 1.0 / blk, 0.0).astype(BF16)
        km = jnp.dot(avg, k_ref[0], preferred_element_type=F32)
        hi, lo = _split_bf16(km)
        kmh_sc[...] = hi
        kml_sc[...] = lo

    row = lax.broadcasted_iota(I32, (nbp, blk), 0)
    half_k = lax.broadcasted_iota(I32, (nbp, LANES), 1) < HEAD_DIM
    for qb in range(nq):
        i = i_base + qb
        for pr in range(npair):
            q2 = q_ref[0, qb * blk:(qb + 1) * blk, pr * LANES:(pr + 1) * LANES]
            kmh = kmh_sc[:, pr * LANES:(pr + 1) * LANES]
            kml = kml_sc[:, pr * LANES:(pr + 1) * LANES]
            for hh in range(2):
                mine = half_k if hh == 0 else jnp.logical_not(half_k)
                gate = (lax.dot_general(jnp.where(mine, kmh, jnp.zeros_like(kmh)), q2, _NT,
                                        preferred_element_type=F32)
                        + lax.dot_general(jnp.where(mine, kml, jnp.zeros_like(kml)), q2, _NT,
                                          preferred_element_type=F32))
                gate = jnp.where(row < i, gate, NEG_INF)
                cnt = jnp.zeros((nbp, blk), F32)
                for m in range(nb):
                    other = jnp.broadcast_to(gate[m:m + 1, :], (nbp, blk))
                    tie = jnp.where(row > m, 1.0, 0.0)
                    cnt = cnt + jnp.where(other > gate, 1.0, 0.0) + jnp.where(other == gate, tie, 0.0)
                chosen = jnp.where(row < i, cnt, float(MOBA_TOPK)) < float(MOBA_TOPK)
                keep_t = jnp.where(row == i, 0.0, jnp.where(chosen, 0.0, NEG_INF))
                keep_t = jnp.concatenate([keep_t, jnp.full((LANES - nbp, blk), NEG_INF, F32)], axis=0)
                qh = jnp.where(low_half if hh == 0 else jnp.logical_not(low_half), q2, jnp.zeros_like(q2))
                r0 = hh * rows_h + qb * blk
                qaug_sc[pr, r0:r0 + blk, :] = jnp.concatenate([qh, keep_t.T.astype(BF16)], axis=1)

    one_hi = jnp.where(lane == HEAD_DIM, 1.0, 0.0).astype(BF16)
    one_lo = jnp.where(lane == 0, 1.0, 0.0).astype(BF16)

    def tile(j, bias_of_head, first, lo):
        start = pl.multiple_of(j * blk, blk)
        onehot = jnp.where(lane == j, 1.0, 0.0).astype(BF16)
        nrows = rows_h - lo * blk
        for pr in range(npair):
            kj = k_ref[0, pl.ds(start, blk), pr * LANES:(pr + 1) * LANES]
            vj = v_ref[0, pl.ds(start, blk), pr * LANES:(pr + 1) * LANES]
            kaug = jnp.concatenate([kj, onehot], axis=1)
            vaug = (jnp.where(low_half, vj, one_hi), jnp.where(low_half, one_lo, vj))
            if lo == 0:
                sc_pair = lax.dot_general(qaug_sc[pr], kaug, _NT, preferred_element_type=F32)
            for hh in range(2):
                h = 2 * pr + hh
                rows = slice(lo * blk, rows_h)
                if lo == 0:
                    sc = sc_pair[hh * rows_h:(hh + 1) * rows_h]
                else:
                    sc = lax.dot_general(qaug_sc[pr, hh * rows_h + lo * blk:(hh + 1) * rows_h, :], kaug, _NT,
                                         preferred_element_type=F32)
                bias = bias_of_head(h)
                scalar_bias = jnp.ndim(bias) == 0
                if scalar_bias:
                    m_cur = jnp.max(sc, axis=1, keepdims=True) + bias
                else:
                    sc = sc + bias
                    m_cur = jnp.max(sc, axis=1, keepdims=True)
                if first:
                    m_new = jnp.broadcast_to(m_cur, (nrows, LANES))
                else:
                    m_prev = m_sc[h, rows, :]
                    m_new = jnp.maximum(m_prev, m_cur)
                shift = m_new - bias if scalar_bias else m_new
                pexp = jnp.exp2(sc - jnp.concatenate([shift, shift], axis=1))
                pv = jnp.dot(pexp.astype(BF16), vaug[hh], preferred_element_type=F32)
                if first:
                    acc_sc[h, rows, :] = pv
                else:
                    acc_sc[h, rows, :] = jnp.exp2(m_prev - m_new) * acc_sc[h, rows, :] + pv
                m_sc[h, rows, :] = m_new

    def mixed_bias(h, kinds):
        parts = {"own": lambda: bias_ref[h, 0], "prev": lambda: bias_ref[h, 1],
                 "far": lambda: jnp.full((blk, blk), bfar_ref[h], F32)}
        return jnp.concatenate([parts[kind]() for kind in kinds], axis=0)

    def kinds_for(t, lo):
        return ["own" if qb == t else "prev" if qb == t + 1 else "far" for qb in range(lo, nq)]

    tile(i_base, lambda h: mixed_bias(h, kinds_for(0, 0)), True, 0)
    for t in range(1, nq):
        tile(i_base + t, lambda h, t=t: mixed_bias(h, kinds_for(t, t)), False, t)

    @pl.when(sp >= 1)
    def _():
        tile(i_base - 1, lambda h: mixed_bias(h, kinds_for(-1, 0)), False, 0)

    def far(j, carry):
        tile(j, lambda h: bfar_ref[h], False, 0)
        return carry

    lax.fori_loop(0, jnp.maximum(i_base - 1, 0), far, 0)

    lane_all = lax.broadcasted_iota(I32, (rows_h, LANES), 1) < HEAD_DIM
    for pr in range(npair):
        acc_e = acc_sc[2 * pr]
        acc_o = acc_sc[2 * pr + 1]
        out = jnp.where(lane_all, acc_e / acc_e[:, HEAD_DIM:HEAD_DIM + 1], acc_o / acc_o[:, 0:1])
        o_ref[0, :, pr * LANES:(pr + 1) * LANES] = out.astype(BF16)


def _moba(q, k, v, bias, bfar):
    bsz, s, aw = q.shape
    blk = MOBA_BLOCK
    nq = MOBA_QBLOCKS
    nb = s // blk
    nbp = -(-nb // 16) * 16
    heads = bias.shape[0]
    grid_spec = pltpu.PrefetchScalarGridSpec(
        num_scalar_prefetch=1,
        grid=(bsz, nb // nq),
        in_specs=[pl.BlockSpec((1, nq * blk, aw), lambda b, i, _: (b, i, 0)),
                  pl.BlockSpec((1, s, aw), lambda b, i, _: (b, 0, 0)),
                  pl.BlockSpec((1, s, aw), lambda b, i, _: (b, 0, 0)),
                  pl.BlockSpec(bias.shape, lambda b, i, _: (0, 0, 0, 0))],
        out_specs=pl.BlockSpec((1, nq * blk, aw), lambda b, i, _: (b, i, 0)),
        scratch_shapes=[pltpu.VMEM((nbp, aw), BF16), pltpu.VMEM((nbp, aw), BF16),
                        pltpu.VMEM((heads // 2, 2 * nq * blk, 2 * LANES), BF16),
                        pltpu.VMEM((heads, nq * blk, LANES), F32), pltpu.VMEM((heads, nq * blk, LANES), F32)],
    )
    return pl.pallas_call(
        _moba_kernel,
        out_shape=jax.ShapeDtypeStruct((bsz, s, aw), BF16),
        grid_spec=grid_spec,
        compiler_params=_params(("parallel", "arbitrary")),
        name="moba",
    )(bfar, q, k, v, bias)


def _ssm_tables(lam_re, lam_im, log_dt, b_re, b_im, c_re, c_im, d_skip):
    L = SSM_CHUNK
    g = lam_re.shape[0]
    dt = jnp.exp(log_dt)[:, None]
    lr, li = lam_re, lam_im
    mag = jnp.exp(lr * dt)
    ab_re, ab_im = mag * jnp.cos(li * dt), mag * jnp.sin(li * dt)
    den = lr * lr + li * li
    nr, ni = ab_re - 1.0, ab_im
    f_re, f_im = (nr * lr + ni * li) / den, (ni * lr - nr * li) / den
    bb_re = f_re[..., None] * b_re - f_im[..., None] * b_im
    bb_im = f_re[..., None] * b_im + f_im[..., None] * b_re
    n = jnp.arange(L + 1, dtype=F32)[:, None, None]
    pw_mag = jnp.exp(n * (lr * dt)[None])
    pw_re, pw_im = pw_mag * jnp.cos(n * (li * dt)[None]), pw_mag * jnp.sin(n * (li * dt)[None])
    ca_re = c_re[None] * pw_re[:L, :, None, :] - c_im[None] * pw_im[:L, :, None, :]
    ca_im = c_re[None] * pw_im[:L, :, None, :] + c_im[None] * pw_re[:L, :, None, :]
    hp = lax.Precision.HIGHEST
    kern = (jnp.einsum("ngcp,gpd->gdcn", ca_re, bb_re, precision=hp)
            - jnp.einsum("ngcp,gpd->gdcn", ca_im, bb_im, precision=hp))
    skip = jnp.asarray(np.eye(SSM_GROUP, dtype=np.float32))[None, :, :, None] * d_skip.reshape(g, 1, SSM_GROUP, 1)
    kern = kern + skip * jnp.asarray((np.arange(L) == 0).astype(np.float32))
    kern = kern.reshape(g, SSM_GROUP * SSM_GROUP, L)
    rev_re, rev_im = pw_re[L - 1::-1][:L], pw_im[L - 1::-1][:L]
    win_re = rev_re[..., None] * bb_re[None] - rev_im[..., None] * bb_im[None]
    win_im = rev_re[..., None] * bb_im[None] + rev_im[..., None] * bb_re[None]
    w_in = jnp.concatenate([win_re, win_im], axis=2)
    w_in = w_in.transpose(1, 3, 0, 2).reshape(g, SSM_GROUP * L, 2 * SSM_STATE)
    fw_re, fw_im = pw_re[1:], pw_im[1:]
    wo_re = c_re[None] * fw_re[:, :, None, :] - c_im[None] * fw_im[:, :, None, :]
    wo_im = -(c_re[None] * fw_im[:, :, None, :] + c_im[None] * fw_re[:, :, None, :])
    w_out = jnp.concatenate([wo_re, wo_im], axis=3)
    w_out = w_out.transpose(1, 3, 2, 0).reshape(g, 2 * SSM_STATE, SSM_GROUP * L)
    a_chunk = jnp.stack([jnp.concatenate([pw_re[L], pw_re[L]], axis=-1),
                         jnp.concatenate([-pw_im[L], pw_im[L]], axis=-1)], axis=1)
    return kern, w_in.astype(BF16), w_out.astype(BF16), a_chunk


def _ssm_kernel(u_ref, kern_ref, win_ref, wout_ref, ac_ref, y_ref, toep_sc):
    bsz, ng, s = u_ref.shape
    L = SSM_CHUNK
    nc = s // L
    s_ix = lax.broadcasted_iota(I32, (L, L), 0)
    t_ix = lax.broadcasted_iota(I32, (L, L), 1)

    def build(cp, carry):
        r0 = pl.multiple_of(cp * L, L)
        for c in range(ng):
            vec = kern_ref[0, pl.ds(cp * ng + c, 1), :]
            lagged = pltpu.roll(jnp.broadcast_to(vec, (L, L)), 0, 1, stride=1, stride_axis=0)
            toep_sc[pl.ds(r0, L), c * L:(c + 1) * L] = jnp.where(t_ix >= s_ix, lagged, 0.0).astype(BF16)
        return carry

    lax.fori_loop(0, ng, build, 0)

    u = jnp.concatenate(
        [jnp.concatenate([u_ref[:, c, ch * L:(ch + 1) * L] for c in range(ng)], axis=1) for ch in range(nc)],
        axis=0).astype(BF16)
    st = jnp.dot(u, win_ref[0], preferred_element_type=F32)
    a1 = ac_ref[0, 0:1, :]
    a2 = ac_ref[0, 1:2, :]
    state = jnp.zeros((bsz, 2 * SSM_STATE), F32)
    prevs = []
    for ch in range(nc):
        prevs.append(state)
        state = a1 * state + a2 * pltpu.roll(state, SSM_STATE, 1) + st[ch * bsz:(ch + 1) * bsz]
    prev = jnp.concatenate(prevs, axis=0).astype(BF16)
    y = (jnp.dot(u, toep_sc[...], preferred_element_type=F32)
         + jnp.dot(prev, wout_ref[0], preferred_element_type=F32))
    for ch in range(nc):
        for c in range(ng):
            y_ref[:, c, ch * L:(ch + 1) * L] = y[ch * bsz:(ch + 1) * bsz, c * L:(c + 1) * L]


def _ssm(u_t, kern, w_in, w_out, a_chunk):
    bsz, sw, s = u_t.shape
    g = sw // SSM_GROUP
    blk = pl.BlockSpec((bsz, SSM_GROUP, s), lambda j: (0, j, 0))
    per_group = lambda a: pl.BlockSpec((1,) + a.shape[1:], lambda j: (j, 0, 0))
    return pl.pallas_call(
        _ssm_kernel,
        out_shape=jax.ShapeDtypeStruct(u_t.shape, F32),
        grid=(g,),
        in_specs=[blk, per_group(kern), per_group(w_in), per_group(w_out), per_group(a_chunk)],
        out_specs=blk,
        scratch_shapes=[pltpu.VMEM((SSM_GROUP * SSM_CHUNK, SSM_GROUP * SSM_CHUNK), BF16)],
        compiler_params=_params(("parallel",)),
        name="ssm",
    )(u_t, kern, w_in, w_out, a_chunk)


def _gelu_tanh(x):
    return 0.5 * x * (1.0 + jnp.tanh(math.sqrt(2.0 / math.pi) * (x + 0.044715 * (x * x * x))))


def _merge_kernel(x_ref, att_ref, yt_ref, sga_ref, sgs_ref, mod_ref, gpost_ref, gpre_ref,
                  wao_ref, wgv_ref, wgg_ref, wmo_ref, wrt_ref, brt_ref,
                  x1_ref, h2_ref, lt_ref):
    a_br = jnp.dot(att_ref[0], wao_ref[...], preferred_element_type=F32)
    z = _gelu_tanh(yt_ref[0]).T.astype(BF16)
    s_br = (jnp.dot(z, wgv_ref[...], preferred_element_type=F32)
            * jax.nn.sigmoid(jnp.dot(z, wgg_ref[...], preferred_element_type=F32)))
    merged = sga_ref[0].astype(F32) * a_br + sgs_ref[0].astype(F32) * s_br
    mix = jnp.dot(merged.astype(BF16), wmo_ref[...], preferred_element_type=F32)
    x1 = x_ref[0] + mod_ref[0, 2:3, :] * _rms(mix, gpost_ref[...])
    x1_ref[0] = x1
    h2 = _rms(x1, gpre_ref[...]) * (1.0 + mod_ref[0, 4:5, :]) + mod_ref[0, 3:4, :]
    h2_ref[0] = _pack_pairs(h2)
    lt_ref[...] = _dot3(wrt_ref[...], h2, _NT) + brt_ref[...]


def _merge(x, att, yt, sga, sgs, mod, g_post, g_pre, w_ao, w_gv, w_gg, w_mo, w_rt, b_rt, tm):
    bsz, s, d = x.shape
    aw = att.shape[2]
    sw = yt.shape[1]
    nt = s // tm
    tspec = lambda width: pl.BlockSpec((1, tm, width), lambda b, i: (b, i, 0))
    full = lambda a: pl.BlockSpec(a.shape, lambda b, i: (0,) * a.ndim)
    return pl.pallas_call(
        _merge_kernel,
        out_shape=(jax.ShapeDtypeStruct((bsz, s, d), F32), jax.ShapeDtypeStruct((bsz, s, d // 2), I32),
                   jax.ShapeDtypeStruct((ROUTER_ROWS, bsz * s), F32)),
        grid=(bsz, nt),
        in_specs=[tspec(d), tspec(aw), pl.BlockSpec((1, sw, tm), lambda b, i: (b, 0, i)),
                  tspec(d), tspec(d),
                  pl.BlockSpec((1, mod.shape[1], d), lambda b, i: (b, 0, 0)),
                  full(g_post), full(g_pre), full(w_ao), full(w_gv), full(w_gg), full(w_mo),
                  full(w_rt), full(b_rt)],
        out_specs=(tspec(d), tspec(d // 2), pl.BlockSpec((ROUTER_ROWS, tm), lambda b, i: (0, b * nt + i))),
        compiler_params=_params(("parallel", "parallel")),
        name="merge",
    )(x, att, yt, sga, sgs, mod, g_post, g_pre, w_ao, w_gv, w_gg, w_mo, w_rt, b_rt)


def _route_kernel(lt_ref, eid_ref, wt_ref, dest_ref, cnt_ref, base_sc, pstart_sc):
    tn = lt_ref.shape[1]
    epg = EXPERTS_PER_GROUP
    ph = pl.program_id(0)
    step = pl.program_id(1)

    @pl.when((ph == 0) & (step == 0))
    def _():
        base_sc[...] = jnp.zeros_like(base_sc)

    @pl.when((ph == 1) & (step == 0))
    def _():
        total = base_sc[...]
        cnt_ref[...] = total.astype(I32)
        padded = jnp.floor((total + (MOE_BLOCK - 1.0)) * (1.0 / MOE_BLOCK)) * MOE_BLOCK
        r = lax.broadcasted_iota(I32, (N_EXPERTS, N_EXPERTS), 0)
        c = lax.broadcasted_iota(I32, (N_EXPERTS, N_EXPERTS), 1)
        before = jnp.where(c < r, 1.0, 0.0)
        pstart_sc[...] = _dot3(before, padded, _NN)
        base_sc[...] = jnp.zeros_like(base_sc)

    row8 = lax.broadcasted_iota(I32, (epg, tn), 0)
    gl = lt_ref[N_EXPERTS:N_EXPERTS + epg, :]
    gl = jnp.where(row8 < N_GROUPS, gl, -jnp.inf)
    gmax = jnp.max(gl, axis=0, keepdims=True)
    gidx = jnp.min(jnp.where(gl == gmax, row8, epg), axis=0, keepdims=True)

    el = jnp.zeros((epg, tn), F32)
    for g in range(N_GROUPS):
        el = jnp.where(gidx == g, lt_ref[g * epg:(g + 1) * epg, :], el)
    m1 = jnp.max(el, axis=0, keepdims=True)
    i1 = jnp.min(jnp.where(el == m1, row8, epg), axis=0, keepdims=True)
    el2 = jnp.where(row8 == i1, -jnp.inf, el)
    m2 = jnp.max(el2, axis=0, keepdims=True)
    i2 = jnp.min(jnp.where(el2 == m2, row8, epg), axis=0, keepdims=True)
    e1 = gidx * epg + i1
    e2 = gidx * epg + i2

    row32 = lax.broadcasted_iota(I32, (N_EXPERTS, tn), 0)
    hit1 = row32 == e1
    hit2 = row32 == e2
    onehot = jnp.where(hit1, 1.0, jnp.where(hit2, 1.0, 0.0))

    @pl.when(ph == 1)
    def _():
        g_p = 1.0 / jnp.sum(jnp.exp(gl - gmax), axis=0, keepdims=True)
        zsum = jnp.sum(jnp.exp(el - m1), axis=0, keepdims=True)
        p1 = 1.0 / zsum
        p2 = jnp.exp(m2 - m1) / zsum
        sr = lax.broadcasted_iota(I32, (tn, tn), 0)
        tc = lax.broadcasted_iota(I32, (tn, tn), 1)
        earlier = jnp.where(sr < tc, 1.0, 0.0).astype(BF16)
        place = (jnp.dot(onehot.astype(BF16), earlier, preferred_element_type=F32)
                 + base_sc[:, 0:1] + pstart_sc[:, 0:1])
        d1 = jnp.sum(jnp.where(hit1, place, 0.0), axis=0, keepdims=True)
        d2 = jnp.sum(jnp.where(hit2, place, 0.0), axis=0, keepdims=True)
        eid_ref[...] = jnp.concatenate([e1, e2], axis=0)
        wt_ref[...] = jnp.concatenate([g_p * p1 / (p1 + p2), g_p * p2 / (p1 + p2)], axis=0)
        dest_ref[...] = jnp.concatenate([d1, d2], axis=0).astype(I32)

    base_sc[...] = base_sc[...] + jnp.sum(onehot, axis=1, keepdims=True)


def _route(logits_t, tn):
    n = logits_t.shape[1]
    two = lambda dt: jax.ShapeDtypeStruct((2, n), dt)
    tspec = pl.BlockSpec((2, tn), lambda ph, i: (0, i * ph))
    return pl.pallas_call(
        _route_kernel,
        out_shape=(two(I32), two(F32), two(I32), jax.ShapeDtypeStruct((N_EXPERTS, LANES), I32)),
        grid=(2, n // tn),
        in_specs=[pl.BlockSpec((ROUTER_ROWS, tn), lambda ph, i: (0, i))],
        out_specs=(tspec, tspec, tspec, pl.BlockSpec((N_EXPERTS, LANES), lambda ph, i: (0, 0))),
        scratch_shapes=[pltpu.VMEM((N_EXPERTS, LANES), F32), pltpu.VMEM((N_EXPERTS, LANES), F32)],
        compiler_params=_params(("arbitrary", "arbitrary")),
        name="route",
    )(logits_t)


def _sc_mesh():
    return plsc.VectorSubcoreMesh(core_axis_name="c", subcore_axis_name="s",
                                  num_cores=SC_CORES, num_subcores=SC_SUBCORES)


def _sc_worker_offset(per_worker):
    return (lax.axis_index("s") * SC_CORES + lax.axis_index("c")) * per_worker


def _sc_dispatch(rows, dest0, dest1, n_rows):
    n_tok, w = rows.shape
    per_worker = n_tok // (SC_CORES * SC_SUBCORES)
    n_chunks = per_worker // SC_WINDOW

    @functools.partial(
        pl.kernel, mesh=_sc_mesh(), out_type=jax.ShapeDtypeStruct((n_rows, w), rows.dtype),
        scratch_types=[pltpu.VMEM((SC_WINDOW,), I32)] * 4 + [pltpu.VMEM((SC_WINDOW, w), rows.dtype)] * 2
        + [pltpu.SemaphoreType.DMA] * 4,
        name="sc_dispatch")
    def scatter(rows_hbm, dest0_hbm, dest1_hbm, x_hbm, i0a, i1a, i0b, i1b, rows_a, rows_b,
                lsem_a, lsem_b, ssem_a, ssem_b):
        base = _sc_worker_offset(per_worker)
        bufs = ((i0a, i1a, rows_a, lsem_a, ssem_a), (i0b, i1b, rows_b, lsem_b, ssem_b))

        def load(chunk, b):
            i0, i1, rv, lsem, _ = bufs[b]
            off = base + chunk * SC_WINDOW
            pltpu.sync_copy(dest0_hbm.at[pl.ds(off, SC_WINDOW)], i0)
            pltpu.sync_copy(dest1_hbm.at[pl.ds(off, SC_WINDOW)], i1)
            return pltpu.make_async_copy(rows_hbm.at[pl.ds(off, SC_WINDOW)], rv, lsem)

        def stores(b):
            i0, i1, rv, _, ssem = bufs[b]
            return (pltpu.make_async_copy(rv, x_hbm.at[i0], ssem),
                    pltpu.make_async_copy(rv, x_hbm.at[i1], ssem))

        load(0, 0).start()

        @pl.loop(0, n_chunks, step=2)
        def _(chunk):
            for b in range(2):
                cur = chunk + b

                @pl.when(cur >= 1)
                def _():
                    for cp in stores(1 - b):
                        cp.wait()

                @pl.when(cur + 1 < n_chunks)
                def _():
                    load(cur + 1, 1 - b).start()

                pltpu.make_async_copy(rows_hbm.at[pl.ds(base, SC_WINDOW)], bufs[b][2], bufs[b][3]).wait()
                for cp in stores(b):
                    cp.start()

        for cp in stores((n_chunks - 1) % 2):
            cp.wait()

    return scatter(rows, dest0, dest1)


def _sc_gather(table, idx):
    n_idx = idx.shape[0]
    w = table.shape[1]
    per_worker = n_idx // (SC_CORES * SC_SUBCORES)
    n_chunks = per_worker // SC_WINDOW

    @functools.partial(
        pl.kernel, mesh=_sc_mesh(), out_type=jax.ShapeDtypeStruct((n_idx, w), table.dtype),
        scratch_types=[pltpu.VMEM((SC_WINDOW,), I32)] * 2 + [pltpu.VMEM((SC_WINDOW, w), table.dtype)] * 2
        + [pltpu.SemaphoreType.DMA] * 4,
        name="sc_gather")
    def gather(table_hbm, idx_hbm, out_hbm, idx_a, idx_b, rows_a, rows_b, gsem_a, gsem_b, wsem_a, wsem_b):
        base = _sc_worker_offset(per_worker)
        bufs = ((idx_a, rows_a, gsem_a, wsem_a), (idx_b, rows_b, gsem_b, wsem_b))

        def fetch(chunk, b):
            iv, rv, gsem, _ = bufs[b]
            pltpu.sync_copy(idx_hbm.at[pl.ds(base + chunk * SC_WINDOW, SC_WINDOW)], iv)
            return pltpu.make_async_copy(table_hbm.at[iv], rv, gsem)

        def write(chunk, b):
            _, rv, _, wsem = bufs[b]
            return pltpu.make_async_copy(rv, out_hbm.at[pl.ds(base + chunk * SC_WINDOW, SC_WINDOW)], wsem)

        fetch(0, 0).start()

        @pl.loop(0, n_chunks, step=2)
        def _(chunk):
            for b in range(2):
                cur = chunk + b

                @pl.when(cur >= 1)
                def _():
                    write(cur - 1, 1 - b).wait()

                @pl.when(cur + 1 < n_chunks)
                def _():
                    fetch(cur + 1, 1 - b).start()

                pltpu.make_async_copy(table_hbm.at[bufs[b][0]], bufs[b][1], bufs[b][2]).wait()
                write(cur, b).start()

        write(n_chunks - 1, (n_chunks - 1) % 2).wait()

    return gather(table, idx)


def _expert_kernel(be_ref, nv_ref, x_ref, wg_ref, wu_ref, wd_ref, y_ref):
    rowi = lax.broadcasted_iota(I32, x_ref.shape, 0)
    lo, hi = _unpack_pairs(jnp.where(rowi < nv_ref[pl.program_id(0)], x_ref[...], 0))
    xb = jnp.concatenate([lo.astype(BF16), hi.astype(BF16)], axis=1)
    gate = jnp.dot(xb, wg_ref[0], preferred_element_type=F32)
    up = jnp.dot(xb, wu_ref[0], preferred_element_type=F32)
    hid = (gate * jax.nn.sigmoid(gate) * up).astype(BF16)
    y_ref[...] = _pack_pairs(jnp.dot(hid, wd_ref[0], preferred_element_type=F32))


def _experts(x_rows, block_e, block_valid, w_gate, w_up, w_down):
    n_blocks = block_e.shape[0]
    d = w_gate.shape[1]
    de = w_gate.shape[2]
    grid_spec = pltpu.PrefetchScalarGridSpec(
        num_scalar_prefetch=2,
        grid=(n_blocks,),
        in_specs=[pl.BlockSpec((MOE_BLOCK, d // 2), lambda i, be, nv: (i, 0)),
                  pl.BlockSpec((1, d, de), lambda i, be, nv: (be[i], 0, 0)),
                  pl.BlockSpec((1, d, de), lambda i, be, nv: (be[i], 0, 0)),
                  pl.BlockSpec((1, de, d), lambda i, be, nv: (be[i], 0, 0))],
        out_specs=pl.BlockSpec((MOE_BLOCK, d // 2), lambda i, be, nv: (i, 0)),
    )
    return pl.pallas_call(
        _expert_kernel,
        out_shape=jax.ShapeDtypeStruct((n_blocks * MOE_BLOCK, d // 2), I32),
        grid_spec=grid_spec,
        compiler_params=_params(("arbitrary",)),
        name="experts",
    )(block_e, block_valid, x_rows, w_gate, w_up, w_down)


def _combine_kernel(y0_ref, y1_ref, x1_ref, wt_ref, g2_ref, gain_ref, o_ref):
    y0 = jnp.concatenate(_unpack_pairs(y0_ref[...]), axis=1)
    y1 = jnp.concatenate(_unpack_pairs(y1_ref[...]), axis=1)
    f = wt_ref[:, 0:1] * y0 + wt_ref[:, 1:2] * y1
    o_ref[...] = x1_ref[...] + g2_ref[0] * _rms(f, gain_ref[...])


def _combine(y2, wts, x1, gate2, gain, tm, row0):
    n, d = x1.shape
    s = n // gate2.shape[0]
    nt = y2.shape[0] // (2 * tm)
    t0 = row0 // tm
    return pl.pallas_call(
        _combine_kernel,
        out_shape=jax.ShapeDtypeStruct((n, d), F32),
        grid=(nt,),
        in_specs=[pl.BlockSpec((tm, d // 2), lambda i: (i, 0)),
                  pl.BlockSpec((tm, d // 2), lambda i: (nt + i, 0)),
                  pl.BlockSpec((tm, d), lambda i: (t0 + i, 0)),
                  pl.BlockSpec((tm, 2), lambda i: (t0 + i, 0)),
                  pl.BlockSpec((1, 1, d), lambda i: ((row0 + i * tm) // s, 0, 0)),
                  pl.BlockSpec((1, d), lambda i: (0, 0))],
        out_specs=pl.BlockSpec((tm, d), lambda i: (t0 + i, 0)),
        input_output_aliases={2: 0},
        compiler_params=_params(("parallel",)),
        name="combine",
    )(y2, y2, x1, wts, gate2, gain)


def _layer(x, mod, bias, bfar, p):
    bsz, s, d = x.shape
    n_tok = bsz * s
    aw = ATT_HEADS * HEAD_DIM
    sw = p["ssm_d"].shape[0]

    w_in = p["w_in"]
    w_rest = jnp.concatenate([w_in[:, :3 * aw], w_in[:, 3 * aw + sw:]], axis=1).astype(BF16)
    w_ut = w_in[:, 3 * aw:3 * aw + sw].T.astype(BF16)
    q, k, v, u_t, sga, sgs = _inproj(x, mod, p["g_pre_mix"].reshape(1, d), w_rest, w_ut, tm=512)
    att = _moba(q, k, v, bias, bfar)
    tables = _ssm_tables(p["ssm_lambda_re"], p["ssm_lambda_im"], p["ssm_log_dt"], p["ssm_b_re"],
                         p["ssm_b_im"], p["ssm_c_re"], p["ssm_c_im"], p["ssm_d"])
    y_t = _ssm(u_t, *tables)

    w_rt = jnp.concatenate([p["w_router_expert"].T, p["w_router_group"].T,
                            jnp.zeros((ROUTER_ROWS - N_EXPERTS - N_GROUPS, d), F32)], axis=0)
    b_rt = jnp.concatenate([p["b_router_expert"], p["b_router_group"],
                            jnp.zeros((ROUTER_ROWS - N_EXPERTS - N_GROUPS,), F32)]).reshape(ROUTER_ROWS, 1)
    x1, h2, logits_t = _merge(
        x, att, y_t, sga, sgs, mod, p["g_post_mix"].reshape(1, d), p["g_pre_ffn"].reshape(1, d),
        p["w_att_out"].astype(BF16), p["w_glu_val"].astype(BF16), p["w_glu_gate"].astype(BF16),
        p["w_mix_out"].astype(BF16), w_rt, b_rt, tm=512)

    eid, wts, dest, counts = _route(logits_t, tn=1024)
    counts = counts[:, 0]
    padded = (counts + MOE_BLOCK - 1) // MOE_BLOCK * MOE_BLOCK
    pend = jnp.cumsum(padded)
    n_blocks = -(-(n_tok * 2) // MOE_BLOCK) + N_EXPERTS
    block_start = jnp.arange(n_blocks, dtype=I32) * MOE_BLOCK
    block_e = jnp.minimum(jnp.sum((pend[None, :] <= block_start[:, None]).astype(I32), axis=1), N_EXPERTS - 1)
    block_valid = jnp.clip(counts[block_e] - (block_start - (pend - padded)[block_e]), 0, MOE_BLOCK)
    x_rows = _sc_dispatch(h2.reshape(n_tok, d // 2), dest[0], dest[1], n_blocks * MOE_BLOCK)
    y_rows = _experts(x_rows, block_e, block_valid, p["w_exp_gate"].astype(BF16), p["w_exp_up"].astype(BF16),
                      p["w_exp_down"].astype(BF16))
    out = x1.reshape(n_tok, d)
    chunk = n_tok // COMBINE_CHUNKS
    for c in range(COMBINE_CHUNKS):
        y2 = _sc_gather(y_rows, dest[:, c * chunk:(c + 1) * chunk].reshape(-1))
        out = _combine(y2, wts.T, out, mod[:, 5:6, :], p["g_post_ffn"].reshape(1, d), tm=512, row0=c * chunk)
    return out.reshape(bsz, s, d)


def kernel(x, c, rel_bias, w_ada, b_ada, g_pre_mix, g_post_mix, w_in, w_att_out, ssm_lambda_re, ssm_lambda_im, ssm_log_dt, ssm_b_re, ssm_b_im, ssm_c_re, ssm_c_im, ssm_d, w_glu_val, w_glu_gate, w_mix_out, g_pre_ffn, g_post_ffn, w_router_group, b_router_group, w_router_expert, b_router_expert, w_exp_gate, w_exp_up, w_exp_down):
    layered = dict(
        w_ada=w_ada, b_ada=b_ada, g_pre_mix=g_pre_mix, g_post_mix=g_post_mix, w_in=w_in,
        w_att_out=w_att_out, ssm_lambda_re=ssm_lambda_re, ssm_lambda_im=ssm_lambda_im,
        ssm_log_dt=ssm_log_dt, ssm_b_re=ssm_b_re, ssm_b_im=ssm_b_im, ssm_c_re=ssm_c_re,
        ssm_c_im=ssm_c_im, ssm_d=ssm_d, w_glu_val=w_glu_val, w_glu_gate=w_glu_gate,
        w_mix_out=w_mix_out, g_pre_ffn=g_pre_ffn, g_post_ffn=g_post_ffn,
        w_router_group=w_router_group, b_router_group=b_router_group,
        w_router_expert=w_router_expert, b_router_expert=b_router_expert,
        w_exp_gate=w_exp_gate, w_exp_up=w_exp_up, w_exp_down=w_exp_down)
    depth = w_ada.shape[0]
    bsz, d = c.shape
    bias = _bias_tiles(rel_bias)
    far_bucket = np.unique(_t5_bucket_np(np.arange(MOBA_BLOCK + 1, max(x.shape[1], MOBA_BLOCK + 2))))
    assert far_bucket.size == 1
    bfar = rel_bias[int(far_bucket[0])] * LOG2E
    for l in range(depth):
        p = {name: a[l] for name, a in layered.items()}
        mod = _ada(c, p["w_ada"], p["b_ada"]).reshape(bsz, -1, d)
        x = _layer(x, mod, bias, bfar, p)
    return x
```

```python
import functools
import math

import numpy as np
import jax
import jax.numpy as jnp
from jax import lax
from jax.experimental import pallas as pl
from jax.experimental.pallas import tpu as pltpu
from jax.experimental.pallas import tpu_sc as plsc

F32 = jnp.float32
BF16 = jnp.bfloat16
I32 = jnp.int32

ATT_HEADS = 8
HEAD_DIM = 64
MOBA_BLOCK = 256
MOBA_TOPK = 3
MOBA_QBLOCKS = 4
NUM_BUCKETS = 32
MAX_DISTANCE = 128
SSM_GROUP = 16
SSM_STATE = 64
SSM_CHUNK = 128
N_GROUPS = 4
EXPERTS_PER_GROUP = 8
N_EXPERTS = N_GROUPS * EXPERTS_PER_GROUP
MOE_BLOCK = 512
RMS_EPS = 1e-6
NEG_INF = -1e30
LOG2E = math.log2(math.e)
LANES = 128
ROUTER_ROWS = 40
VMEM_LIMIT = 56 * 1024 * 1024
SC_CORES = 2
SC_SUBCORES = 16
SC_WINDOW = 64
COMBINE_CHUNKS = 4

_NT = (((1,), (1,)), ((), ()))
_NN = (((1,), (0,)), ((), ()))


def _params(sem, vmem=VMEM_LIMIT):
    return pltpu.CompilerParams(dimension_semantics=sem, vmem_limit_bytes=vmem)


def _split_bf16(a):
    hi = a.astype(BF16)
    lo = (a - hi.astype(F32)).astype(BF16)
    return hi, lo


def _dot3(a, b, dims):
    a_hi, a_lo = _split_bf16(a)
    b_hi, b_lo = _split_bf16(b)
    dg = functools.partial(lax.dot_general, dimension_numbers=dims, preferred_element_type=F32)
    return dg(a_hi, b_hi) + (dg(a_hi, b_lo) + dg(a_lo, b_hi))


def _pack_pairs(a):
    w = a.shape[1] // 2
    bits = pltpu.bitcast(a.astype(BF16).astype(F32), jnp.uint32)
    return pltpu.bitcast((bits[:, :w] >> 16) | (bits[:, w:] & jnp.uint32(0xFFFF0000)), I32)


def _unpack_pairs(packed):
    words = pltpu.bitcast(packed, jnp.uint32)
    lo = pltpu.bitcast(words << 16, F32)
    hi = pltpu.bitcast(words & jnp.uint32(0xFFFF0000), F32)
    return lo, hi


def _ada_kernel(c_ref, w_ref, b_ref, o_ref):
    c = c_ref[...]
    ca = c * jax.nn.sigmoid(c)
    o_ref[...] = _dot3(ca, w_ref[...], _NN) + b_ref[...]


def _ada(c, w, b):
    bsz, d = c.shape
    n = w.shape[1]
    tn = 1536
    return pl.pallas_call(
        _ada_kernel,
        out_shape=jax.ShapeDtypeStruct((bsz, n), F32),
        grid=(n // tn,),
        in_specs=[pl.BlockSpec((bsz, d), lambda j: (0, 0)),
                  pl.BlockSpec((d, tn), lambda j: (0, j)),
                  pl.BlockSpec((1, tn), lambda j: (0, j))],
        out_specs=pl.BlockSpec((bsz, tn), lambda j: (0, j)),
        compiler_params=_params(("parallel",)),
        name="ada",
    )(c, w, b.reshape(1, n))


def _t5_bucket_np(dist):
    n = np.maximum(dist, 0)
    max_exact = NUM_BUCKETS // 2
    nf = np.maximum(n, 1).astype(np.float32)
    large = max_exact + (np.log(nf / np.float32(max_exact)) / np.float32(math.log(MAX_DISTANCE / max_exact))
                         * np.float32(NUM_BUCKETS - max_exact)).astype(np.int32)
    large = np.minimum(large, NUM_BUCKETS - 1)
    return np.where(n < max_exact, n, large).astype(np.int32)


def _bias_kernel(rb_ref, bk_ref, o_ref):
    h = pl.program_id(0)
    for t in range(2):
        bk = bk_ref[t]
        acc = jnp.where(bk < 0, NEG_INF, 0.0).astype(F32)
        for b in range(NUM_BUCKETS):
            acc = jnp.where(bk == b, rb_ref[b, h] * LOG2E, acc)
        o_ref[0, t] = acc


def _bias_tiles(rel_bias):
    qi = np.arange(MOBA_BLOCK)[:, None]
    kj = np.arange(MOBA_BLOCK)[None, :]
    own = np.where(qi >= kj, _t5_bucket_np(qi - kj), -1)
    adj = _t5_bucket_np(qi - kj + MOBA_BLOCK)
    buckets = jnp.asarray(np.stack([own, adj]).astype(np.int32))
    return pl.pallas_call(
        _bias_kernel,
        out_shape=jax.ShapeDtypeStruct((ATT_HEADS, 2, MOBA_BLOCK, MOBA_BLOCK), F32),
        grid=(ATT_HEADS,),
        in_specs=[pl.BlockSpec(memory_space=pltpu.SMEM),
                  pl.BlockSpec((2, MOBA_BLOCK, MOBA_BLOCK), lambda h: (0, 0, 0))],
        out_specs=pl.BlockSpec((1, 2, MOBA_BLOCK, MOBA_BLOCK), lambda h: (h, 0, 0, 0)),
        compiler_params=_params(("parallel",)),
        name="t5_bias",
    )(rel_bias, buckets)


def _rms(x, gain):
    ms = jnp.mean(x * x, axis=-1, keepdims=True)
    return x * lax.rsqrt(ms + RMS_EPS) * gain


def _inproj_kernel(x_ref, mod_ref, g_ref, w_ref, wut_ref,
                   q_ref, k_ref, v_ref, ut_ref, sga_ref, sgs_ref):
    aw = q_ref.shape[2]
    d = x_ref.shape[2]
    x = x_ref[0]
    h = _rms(x, g_ref[...]) * (1.0 + mod_ref[0, 1:2, :]) + mod_ref[0, 0:1, :]
    hb = h.astype(BF16)

    def proj(lo, width):
        return jnp.dot(hb, w_ref[:, lo:lo + width], preferred_element_type=F32)

    q_ref[0] = (proj(0, aw) * (HEAD_DIM ** -0.5 * LOG2E)).astype(BF16)
    k_ref[0] = proj(aw, aw).astype(BF16)
    v_ref[0] = proj(2 * aw, aw).astype(BF16)
    ut_ref[0] = lax.dot_general(wut_ref[...], hb, _NT, preferred_element_type=F32)
    sga_ref[0] = jax.nn.sigmoid(proj(3 * aw, d)).astype(BF16)
    sgs_ref[0] = jax.nn.sigmoid(proj(3 * aw + d, d)).astype(BF16)


def _inproj(x, mod, gain, w_rest, w_ut, tm):
    bsz, s, d = x.shape
    aw = ATT_HEADS * HEAD_DIM
    sw = w_ut.shape[0]
    tok = lambda width, dt: jax.ShapeDtypeStruct((bsz, s, width), dt)
    tspec = lambda width: pl.BlockSpec((1, tm, width), lambda b, i: (b, i, 0))
    return pl.pallas_call(
        _inproj_kernel,
        out_shape=(tok(aw, BF16), tok(aw, BF16), tok(aw, BF16),
                   jax.ShapeDtypeStruct((bsz, sw, s), F32), tok(d, BF16), tok(d, BF16)),
        grid=(bsz, s // tm),
        in_specs=[tspec(d),
                  pl.BlockSpec((1, mod.shape[1], d), lambda b, i: (b, 0, 0)),
                  pl.BlockSpec((1, d), lambda b, i: (0, 0)),
                  pl.BlockSpec(w_rest.shape, lambda b, i: (0, 0), pipeline_mode=pl.Buffered(1)),
                  pl.BlockSpec(w_ut.shape, lambda b, i: (0, 0), pipeline_mode=pl.Buffered(1))],
        out_specs=(tspec(aw), tspec(aw), tspec(aw),
                   pl.BlockSpec((1, sw, tm), lambda b, i: (b, 0, i)), tspec(d), tspec(d)),
        compiler_params=_params(("parallel", "parallel")),
        name="inproj",
    )(x, mod, gain, w_rest, w_ut)


def _moba_kernel(bfar_ref, q_ref, k_ref, v_ref, bias_ref, o_ref,
                 kmh_sc, kml_sc, qaug_sc, m_sc, acc_sc):
    blk = MOBA_BLOCK
    nq = MOBA_QBLOCKS
    rows_h = nq * blk
    sp = pl.program_id(1)
    i_base = nq * sp
    s = k_ref.shape[1]
    nb = s // blk
    nbp = kmh_sc.shape[0]
    npair = q_ref.shape[2] // LANES
    lane = lax.broadcasted_iota(I32, (blk, LANES), 1)
    low_half = lane < HEAD_DIM

    @pl.when(sp == 0)
    def _():
        r = lax.broadcasted_iota(I32, (nbp, s), 0)
        c = lax.broadcasted_iota(I32, (nbp, s), 1)
        avg = jnp.where((c >= r * blk) & (c < (r + 1) * blk), 1.0 / blk, 0.0).astype(BF16)
        km = jnp.dot(avg, k_ref[0], preferred_element_type=F32)
        hi, lo = _split_bf16(km)
        kmh_sc[...] = hi
        kml_sc[...] = lo

    row = lax.broadcasted_iota(I32, (nbp, blk), 0)
    half_k = lax.broadcasted_iota(I32, (nbp, LANES), 1) < HEAD_DIM
    for qb in range(nq):
        i = i_base + qb
        for pr in range(npair):
            q2 = q_ref[0, qb * blk:(qb + 1) * blk, pr * LANES:(pr + 1) * LANES]
            kmh = kmh_sc[:, pr * LANES:(pr + 1) * LANES]
            kml = kml_sc[:, pr * LANES:(pr + 1) * LANES]
            for hh in range(2):
                mine = half_k if hh == 0 else jnp.logical_not(half_k)
                gate = (lax.dot_general(jnp.where(mine, kmh, jnp.zeros_like(kmh)), q2, _NT,
                                        preferred_element_type=F32)
                        + lax.dot_general(jnp.where(mine, kml, jnp.zeros_like(kml)), q2, _NT,
                                          preferred_element_type=F32))
                gate = jnp.where(row < i, gate, NEG_INF)
                cnt = jnp.zeros((nbp, blk), F32)
                for m in range(nb):
                    other = jnp.broadcast_to(gate[m:m + 1, :], (nbp, blk))
                    tie = jnp.where(row > m, 1.0, 0.0)
                    cnt = cnt + jnp.where(other > gate, 1.0, 0.0) + jnp.where(other == gate, tie, 0.0)
                chosen = jnp.where(row < i, cnt, float(MOBA_TOPK)) < float(MOBA_TOPK)
                keep_t = jnp.where(row == i, 0.0, jnp.where(chosen, 0.0, NEG_INF))
                keep_t = jnp.concatenate([keep_t, jnp.full((LANES - nbp, blk), NEG_INF, F32)], axis=0)
                qh = jnp.where(low_half if hh == 0 else jnp.logical_not(low_half), q2, jnp.zeros_like(q2))
                r0 = hh * rows_h + qb * blk
                qaug_sc[pr, r0:r0 + blk, :] = jnp.concatenate([qh, keep_t.T.astype(BF16)], axis=1)

    one_hi = jnp.where(lane == HEAD_DIM, 1.0, 0.0).astype(BF16)
    one_lo = jnp.where(lane == 0, 1.0, 0.0).astype(BF16)

    def tile(j, bias_of_head, first, lo):
        start = pl.multiple_of(j * blk, blk)
        onehot = jnp.where(lane == j, 1.0, 0.0).astype(BF16)
        nrows = rows_h - lo * blk
        for pr in range(npair):
            kj = k_ref[0, pl.ds(start, blk), pr * LANES:(pr + 1) * LANES]
            vj = v_ref[0, pl.ds(start, blk), pr * LANES:(pr + 1) * LANES]
            kaug = jnp.concatenate([kj, onehot], axis=1)
            vaug = (jnp.where(low_half, vj, one_hi), jnp.where(low_half, one_lo, vj))
            if lo == 0:
                sc_pair = lax.dot_general(qaug_sc[pr], kaug, _NT, preferred_element_type=F32)
            for hh in range(2):
                h = 2 * pr + hh
                rows = slice(lo * blk, rows_h)
                if lo == 0:
                    sc = sc_pair[hh * rows_h:(hh + 1) * rows_h]
                else:
                    sc = lax.dot_general(qaug_sc[pr, hh * rows_h + lo * blk:(hh + 1) * rows_h, :], kaug, _NT,
                                         preferred_element_type=F32)
                bias = bias_of_head(h)
                scalar_bias = jnp.ndim(bias) == 0
                if scalar_bias:
                    m_cur = jnp.max(sc, axis=1, keepdims=True) + bias
                else:
                    sc = sc + bias
                    m_cur = jnp.max(sc, axis=1, keepdims=True)
                if first:
                    m_new = jnp.broadcast_to(m_cur, (nrows, LANES))
                else:
                    m_prev = m_sc[h, rows, :]
                    m_new = jnp.maximum(m_prev, m_cur)
                shift = m_new - bias if scalar_bias else m_new
                pexp = jnp.exp2(sc - jnp.concatenate([shift, shift], axis=1))
                pv = jnp.dot(pexp.astype(BF16), vaug[hh], preferred_element_type=F32)
                if first:
                    acc_sc[h, rows, :] = pv
                else:
                    acc_sc[h, rows, :] = jnp.exp2(m_prev - m_new) * acc_sc[h, rows, :] + pv
                m_sc[h, rows, :] = m_new

    def mixed_bias(h, kinds):
        parts = {"own": lambda: bias_ref[h, 0], "prev": lambda: bias_ref[h, 1],
                 "far": lambda: jnp.full((blk, blk), bfar_ref[h], F32)}
        return jnp.concatenate([parts[kind]() for kind in kinds], axis=0)

    def kinds_for(t, lo):
        return ["own" if qb == t else "prev" if qb == t + 1 else "far" for qb in range(lo, nq)]

    tile(i_base, lambda h: mixed_bias(h, kinds_for(0, 0)), True, 0)
    for t in range(1, nq):
        tile(i_base + t, lambda h, t=t: mixed_bias(h, kinds_for(t, t)), False, t)

    @pl.when(sp >= 1)
    def _():
        tile(i_base - 1, lambda h: mixed_bias(h, kinds_for(-1, 0)), False, 0)

    def far(j, carry):
        tile(j, lambda h: bfar_ref[h], False, 0)
        return carry

    lax.fori_loop(0, jnp.maximum(i_base - 1, 0), far, 0)

    lane_all = lax.broadcasted_iota(I32, (rows_h, LANES), 1) < HEAD_DIM
    for pr in range(npair):
        acc_e = acc_sc[2 * pr]
        acc_o = acc_sc[2 * pr + 1]
        out = jnp.where(lane_all, acc_e / acc_e[:, HEAD_DIM:HEAD_DIM + 1], acc_o / acc_o[:, 0:1])
        o_ref[0, :, pr * LANES:(pr + 1) * LANES] = out.astype(BF16)


def _moba(q, k, v, bias, bfar):
    bsz, s, aw = q.shape
    blk = MOBA_BLOCK
    nq = MOBA_QBLOCKS
    nb = s // blk
    nbp = -(-nb // 16) * 16
    heads = bias.shape[0]
    grid_spec = pltpu.PrefetchScalarGridSpec(
        num_scalar_prefetch=1,
        grid=(bsz, nb // nq),
        in_specs=[pl.BlockSpec((1, nq * blk, aw), lambda b, i, _: (b, i, 0)),
                  pl.BlockSpec((1, s, aw), lambda b, i, _: (b, 0, 0)),
                  pl.BlockSpec((1, s, aw), lambda b, i, _: (b, 0, 0)),
                  pl.BlockSpec(bias.shape, lambda b, i, _: (0, 0, 0, 0))],
        out_specs=pl.BlockSpec((1, nq * blk, aw), lambda b, i, _: (b, i, 0)),
        scratch_shapes=[pltpu.VMEM((nbp, aw), BF16), pltpu.VMEM((nbp, aw), BF16),
                        pltpu.VMEM((heads // 2, 2 * nq * blk, 2 * LANES), BF16),
                        pltpu.VMEM((heads, nq * blk, LANES), F32), pltpu.VMEM((heads, nq * blk, LANES), F32)],
    )
    return pl.pallas_call(
        _moba_kernel,
        out_shape=jax.ShapeDtypeStruct((bsz, s, aw), BF16),
        grid_spec=grid_spec,
        compiler_params=_params(("parallel", "arbitrary")),
        name="moba",
    )(bfar, q, k, v, bias)


def _ssm_tables(lam_re, lam_im, log_dt, b_re, b_im, c_re, c_im, d_skip):
    L = SSM_CHUNK
    g = lam_re.shape[0]
    dt = jnp.exp(log_dt)[:, None]
    lr, li = lam_re, lam_im
    mag = jnp.exp(lr * dt)
    ab_re, ab_im = mag * jnp.cos(li * dt), mag * jnp.sin(li * dt)
    den = lr * lr + li * li
    nr, ni = ab_re - 1.0, ab_im
    f_re, f_im = (nr * lr + ni * li) / den, (ni * lr - nr * li) / den
    bb_re = f_re[..., None] * b_re - f_im[..., None] * b_im
    bb_im = f_re[..., None] * b_im + f_im[..., None] * b_re
    n = jnp.arange(L + 1, dtype=F32)[:, None, None]
    pw_mag = jnp.exp(n * (lr * dt)[None])
    pw_re, pw_im = pw_mag * jnp.cos(n * (li * dt)[None]), pw_mag * jnp.sin(n * (li * dt)[None])
    ca_re = c_re[None] * pw_re[:L, :, None, :] - c_im[None] * pw_im[:L, :, None, :]
    ca_im = c_re[None] * pw_im[:L, :, None, :] + c_im[None] * pw_re[:L, :, None, :]
    hp = lax.Precision.HIGHEST
    kern = (jnp.einsum("ngcp,gpd->gdcn", ca_re, bb_re, precision=hp)
            - jnp.einsum("ngcp,gpd->gdcn", ca_im, bb_im, precision=hp))
    skip = jnp.asarray(np.eye(SSM_GROUP, dtype=np.float32))[None, :, :, None] * d_skip.reshape(g, 1, SSM_GROUP, 1)
    kern = kern + skip * jnp.asarray((np.arange(L) == 0).astype(np.float32))
    kern = kern.reshape(g, SSM_GROUP * SSM_GROUP, L)
    rev_re, rev_im = pw_re[L - 1::-1][:L], pw_im[L - 1::-1][:L]
    win_re = rev_re[..., None] * bb_re[None] - rev_im[..., None] * bb_im[None]
    win_im = rev_re[..., None] * bb_im[None] + rev_im[..., None] * bb_re[None]
    w_in = jnp.concatenate([win_re, win_im], axis=2)
    w_in = w_in.transpose(1, 3, 0, 2).reshape(g, SSM_GROUP * L, 2 * SSM_STATE)
    fw_re, fw_im = pw_re[1:], pw_im[1:]
    wo_re = c_re[None] * fw_re[:, :, None, :] - c_im[None] * fw_im[:, :, None, :]
    wo_im = -(c_re[None] * fw_im[:, :, None, :] + c_im[None] * fw_re[:, :, None, :])
    w_out = jnp.concatenate([wo_re, wo_im], axis=3)
    w_out = w_out.transpose(1, 3, 2, 0).reshape(g, 2 * SSM_STATE, SSM_GROUP * L)
    a_chunk = jnp.stack([jnp.concatenate([pw_re[L], pw_re[L]], axis=-1),
                         jnp.concatenate([-pw_im[L], pw_im[L]], axis=-1)], axis=1)
    return kern, w_in.astype(BF16), w_out.astype(BF16), a_chunk


def _ssm_kernel(u_ref, kern_ref, win_ref, wout_ref, ac_ref, y_ref, toep_sc):
    bsz, ng, s = u_ref.shape
    L = SSM_CHUNK
    nc = s // L
    s_ix = lax.broadcasted_iota(I32, (L, L), 0)
    t_ix = lax.broadcasted_iota(I32, (L, L), 1)

    def build(cp, carry):
        r0 = pl.multiple_of(cp * L, L)
        for c in range(ng):
            vec = kern_ref[0, pl.ds(cp * ng + c, 1), :]
            lagged = pltpu.roll(jnp.broadcast_to(vec, (L, L)), 0, 1, stride=1, stride_axis=0)
            toep_sc[pl.ds(r0, L), c * L:(c + 1) * L] = jnp.where(t_ix >= s_ix, lagged, 0.0).astype(BF16)
        return carry

    lax.fori_loop(0, ng, build, 0)

    u = jnp.concatenate(
        [jnp.concatenate([u_ref[:, c, ch * L:(ch + 1) * L] for c in range(ng)], axis=1) for ch in range(nc)],
        axis=0).astype(BF16)
    st = jnp.dot(u, win_ref[0], preferred_element_type=F32)
    a1 = ac_ref[0, 0:1, :]
    a2 = ac_ref[0, 1:2, :]
    state = jnp.zeros((bsz, 2 * SSM_STATE), F32)
    prevs = []
    for ch in range(nc):
        prevs.append(state)
        state = a1 * state + a2 * pltpu.roll(state, SSM_STATE, 1) + st[ch * bsz:(ch + 1) * bsz]
    prev = jnp.concatenate(prevs, axis=0).astype(BF16)
    y = (jnp.dot(u, toep_sc[...], preferred_element_type=F32)
         + jnp.dot(prev, wout_ref[0], preferred_element_type=F32))
    for ch in range(nc):
        for c in range(ng):
            y_ref[:, c, ch * L:(ch + 1) * L] = y[ch * bsz:(ch + 1) * bsz, c * L:(c + 1) * L]


def _ssm(u_t, kern, w_in, w_out, a_chunk):
    bsz, sw, s = u_t.shape
    g = sw // SSM_GROUP
    blk = pl.BlockSpec((bsz, SSM_GROUP, s), lambda j: (0, j, 0))
    per_group = lambda a: pl.BlockSpec((1,) + a.shape[1:], lambda j: (j, 0, 0))
    return pl.pallas_call(
        _ssm_kernel,
        out_shape=jax.ShapeDtypeStruct(u_t.shape, F32),
        grid=(g,),
        in_specs=[blk, per_group(kern), per_group(w_in), per_group(w_out), per_group(a_chunk)],
        out_specs=blk,
        scratch_shapes=[pltpu.VMEM((SSM_GROUP * SSM_CHUNK, SSM_GROUP * SSM_CHUNK), BF16)],
        compiler_params=_params(("parallel",)),
        name="ssm",
    )(u_t, kern, w_in, w_out, a_chunk)


def _gelu_tanh(x):
    return 0.5 * x * (1.0 + jnp.tanh(math.sqrt(2.0 / math.pi) * (x + 0.044715 * (x * x * x))))


def _merge_kernel(x_ref, att_ref, yt_ref, sga_ref, sgs_ref, mod_ref, gpost_ref, gpre_ref,
                  wao_ref, wgv_ref, wgg_ref, wmo_ref, wrt_ref, brt_ref,
                  x1_ref, h2_ref, lt_ref):
    a_br = jnp.dot(att_ref[0], wao_ref[...], preferred_element_type=F32)
    z = _gelu_tanh(yt_ref[0]).T.astype(BF16)
    s_br = (jnp.dot(z, wgv_ref[...], preferred_element_type=F32)
            * jax.nn.sigmoid(jnp.dot(z, wgg_ref[...], preferred_element_type=F32)))
    merged = sga_ref[0].astype(F32) * a_br + sgs_ref[0].astype(F32) * s_br
    mix = jnp.dot(merged.astype(BF16), wmo_ref[...], preferred_element_type=F32)
    x1 = x_ref[0] + mod_ref[0, 2:3, :] * _rms(mix, gpost_ref[...])
    x1_ref[0] = x1
    h2 = _rms(x1, gpre_ref[...]) * (1.0 + mod_ref[0, 4:5, :]) + mod_ref[0, 3:4, :]
    h2_ref[0] = _pack_pairs(h2)
    lt_ref[...] = _dot3(wrt_ref[...], h2, _NT) + brt_ref[...]


def _merge(x, att, yt, sga, sgs, mod, g_post, g_pre, w_ao, w_gv, w_gg, w_mo, w_rt, b_rt, tm):
    bsz, s, d = x.shape
    aw = att.shape[2]
    sw = yt.shape[1]
    nt = s // tm
    tspec = lambda width: pl.BlockSpec((1, tm, width), lambda b, i: (b, i, 0))
    full = lambda a: pl.BlockSpec(a.shape, lambda b, i: (0,) * a.ndim)
    return pl.pallas_call(
        _merge_kernel,
        out_shape=(jax.ShapeDtypeStruct((bsz, s, d), F32), jax.ShapeDtypeStruct((bsz, s, d // 2), I32),
                   jax.ShapeDtypeStruct((ROUTER_ROWS, bsz * s), F32)),
        grid=(bsz, nt),
        in_specs=[tspec(d), tspec(aw), pl.BlockSpec((1, sw, tm), lambda b, i: (b, 0, i)),
                  tspec(d), tspec(d),
                  pl.BlockSpec((1, mod.shape[1], d), lambda b, i: (b, 0, 0)),
                  full(g_post), full(g_pre), full(w_ao), full(w_gv), full(w_gg), full(w_mo),
                  full(w_rt), full(b_rt)],
        out_specs=(tspec(d), tspec(d // 2), pl.BlockSpec((ROUTER_ROWS, tm), lambda b, i: (0, b * nt + i))),
        compiler_params=_params(("parallel", "parallel")),
        name="merge",
    )(x, att, yt, sga, sgs, mod, g_post, g_pre, w_ao, w_gv, w_gg, w_mo, w_rt, b_rt)


def _route_kernel(lt_ref, eid_ref, wt_ref, dest_ref, cnt_ref, base_sc, pstart_sc):
    tn = lt_ref.shape[1]
    epg = EXPERTS_PER_GROUP
    ph = pl.program_id(0)
    step = pl.program_id(1)

    @pl.when((ph == 0) & (step == 0))
    def _():
        base_sc[...] = jnp.zeros_like(base_sc)

    @pl.when((ph == 1) & (step == 0))
    def _():
        total = base_sc[...]
        cnt_ref[...] = total.astype(I32)
        padded = jnp.floor((total + (MOE_BLOCK - 1.0)) * (1.0 / MOE_BLOCK)) * MOE_BLOCK
        r = lax.broadcasted_iota(I32, (N_EXPERTS, N_EXPERTS), 0)
        c = lax.broadcasted_iota(I32, (N_EXPERTS, N_EXPERTS), 1)
        before = jnp.where(c < r, 1.0, 0.0)
        pstart_sc[...] = _dot3(before, padded, _NN)
        base_sc[...] = jnp.zeros_like(base_sc)

    row8 = lax.broadcasted_iota(I32, (epg, tn), 0)
    gl = lt_ref[N_EXPERTS:N_EXPERTS + epg, :]
    gl = jnp.where(row8 < N_GROUPS, gl, -jnp.inf)
    gmax = jnp.max(gl, axis=0, keepdims=True)
    gidx = jnp.min(jnp.where(gl == gmax, row8, epg), axis=0, keepdims=True)

    el = jnp.zeros((epg, tn), F32)
    for g in range(N_GROUPS):
        el = jnp.where(gidx == g, lt_ref[g * epg:(g + 1) * epg, :], el)
    m1 = jnp.max(el, axis=0, keepdims=True)
    i1 = jnp.min(jnp.where(el == m1, row8, epg), axis=0, keepdims=True)
    el2 = jnp.where(row8 == i1, -jnp.inf, el)
    m2 = jnp.max(el2, axis=0, keepdims=True)
    i2 = jnp.min(jnp.where(el2 == m2, row8, epg), axis=0, keepdims=True)
    e1 = gidx * epg + i1
    e2 = gidx * epg + i2

    row32 = lax.broadcasted_iota(I32, (N_EXPERTS, tn), 0)
    hit1 = row32 == e1
    hit2 = row32 == e2
    onehot = jnp.where(hit1, 1.0, jnp.where(hit2, 1.0, 0.0))

    @pl.when(ph == 1)
    def _():
        g_p = 1.0 / jnp.sum(jnp.exp(gl - gmax), axis=0, keepdims=True)
        zsum = jnp.sum(jnp.exp(el - m1), axis=0, keepdims=True)
        p1 = 1.0 / zsum
        p2 = jnp.exp(m2 - m1) / zsum
        sr = lax.broadcasted_iota(I32, (tn, tn), 0)
        tc = lax.broadcasted_iota(I32, (tn, tn), 1)
        earlier = jnp.where(sr < tc, 1.0, 0.0).astype(BF16)
        place = (jnp.dot(onehot.astype(BF16), earlier, preferred_element_type=F32)
                 + base_sc[:, 0:1] + pstart_sc[:, 0:1])
        d1 = jnp.sum(jnp.where(hit1, place, 0.0), axis=0, keepdims=True)
        d2 = jnp.sum(jnp.where(hit2, place, 0.0), axis=0, keepdims=True)
        eid_ref[...] = jnp.concatenate([e1, e2], axis=0)
        wt_ref[...] = jnp.concatenate([g_p * p1 / (p1 + p2), g_p * p2 / (p1 + p2)], axis=0)
        dest_ref[...] = jnp.concatenate([d1, d2], axis=0).astype(I32)

    base_sc[...] = base_sc[...] + jnp.sum(onehot, axis=1, keepdims=True)


def _route(logits_t, tn):
    n = logits_t.shape[1]
    two = lambda dt: jax.ShapeDtypeStruct((2, n), dt)
    tspec = pl.BlockSpec((2, tn), lambda ph, i: (0, i * ph))
    return pl.pallas_call(
        _route_kernel,
        out_shape=(two(I32), two(F32), two(I32), jax.ShapeDtypeStruct((N_EXPERTS, LANES), I32)),
        grid=(2, n // tn),
        in_specs=[pl.BlockSpec((ROUTER_ROWS, tn), lambda ph, i: (0, i))],
        out_specs=(tspec, tspec, tspec, pl.BlockSpec((N_EXPERTS, LANES), lambda ph, i: (0, 0))),
        scratch_shapes=[pltpu.VMEM((N_EXPERTS, LANES), F32), pltpu.VMEM((N_EXPERTS, LANES), F32)],
        compiler_params=_params(("arbitrary", "arbitrary")),
        name="route",
    )(logits_t)


def _sc_mesh():
    return plsc.VectorSubcoreMesh(core_axis_name="c", subcore_axis_name="s",
                                  num_cores=SC_CORES, num_subcores=SC_SUBCORES)


def _sc_worker_offset(per_worker):
    return (lax.axis_index("s") * SC_CORES + lax.axis_index("c")) * per_worker


def _sc_dispatch(rows, dest0, dest1, n_rows):
    n_tok, w = rows.shape
    per_worker = n_tok // (SC_CORES * SC_SUBCORES)
    n_chunks = per_worker // SC_WINDOW

    @functools.partial(
        pl.kernel, mesh=_sc_mesh(), out_type=jax.ShapeDtypeStruct((n_rows, w), rows.dtype),
        scratch_types=[pltpu.VMEM((SC_WINDOW,), I32)] * 4 + [pltpu.VMEM((SC_WINDOW, w), rows.dtype)] * 2
        + [pltpu.SemaphoreType.DMA] * 4,
        name="sc_dispatch")
    def scatter(rows_hbm, dest0_hbm, dest1_hbm, x_hbm, i0a, i1a, i0b, i1b, rows_a, rows_b,
                lsem_a, lsem_b, ssem_a, ssem_b):
        base = _sc_worker_offset(per_worker)
        bufs = ((i0a, i1a, rows_a, lsem_a, ssem_a), (i0b, i1b, rows_b, lsem_b, ssem_b))

        def load(chunk, b):
            i0, i1, rv, lsem, _ = bufs[b]
            off = base + chunk * SC_WINDOW
            pltpu.sync_copy(dest0_hbm.at[pl.ds(off, SC_WINDOW)], i0)
            pltpu.sync_copy(dest1_hbm.at[pl.ds(off, SC_WINDOW)], i1)
            return pltpu.make_async_copy(rows_hbm.at[pl.ds(off, SC_WINDOW)], rv, lsem)

        def stores(b):
            i0, i1, rv, _, ssem = bufs[b]
            return (pltpu.make_async_copy(rv, x_hbm.at[i0], ssem),
                    pltpu.make_async_copy(rv, x_hbm.at[i1], ssem))

        load(0, 0).start()

        @pl.loop(0, n_chunks, step=2)
        def _(chunk):
            for b in range(2):
                cur = chunk + b

                @pl.when(cur >= 1)
                def _():
                    for cp in stores(1 - b):
                        cp.wait()

                @pl.when(cur + 1 < n_chunks)
                def _():
                    load(cur + 1, 1 - b).start()

                pltpu.make_async_copy(rows_hbm.at[pl.ds(base, SC_WINDOW)], bufs[b][2], bufs[b][3]).wait()
                for cp in stores(b):
                    cp.start()

        for cp in stores((n_chunks - 1) % 2):
            cp.wait()

    return scatter(rows, dest0, dest1)


def _sc_gather(table, idx):
    n_idx = idx.shape[0]
    w = table.shape[1]
    per_worker = n_idx // (SC_CORES * SC_SUBCORES)
    n_chunks = per_worker // SC_WINDOW

    @functools.partial(
        pl.kernel, mesh=_sc_mesh(), out_type=jax.ShapeDtypeStruct((n_idx, w), table.dtype),
        scratch_types=[pltpu.VMEM((SC_WINDOW,), I32)] * 2 + [pltpu.VMEM((SC_WINDOW, w), table.dtype)] * 2
        + [pltpu.SemaphoreType.DMA] * 4,
        name="sc_gather")
    def gather(table_hbm, idx_hbm, out_hbm, idx_a, idx_b, rows_a, rows_b, gsem_a, gsem_b, wsem_a, wsem_b):
        base = _sc_worker_offset(per_worker)
        bufs = ((idx_a, rows_a, gsem_a, wsem_a), (idx_b, rows_b, gsem_b, wsem_b))

        def fetch(chunk, b):
            iv, rv, gsem, _ = bufs[b]
            pltpu.sync_copy(idx_hbm.at[pl.ds(base + chunk * SC_WINDOW, SC_WINDOW)], iv)
            return pltpu.make_async_copy(table_hbm.at[iv], rv, gsem)

        def write(chunk, b):
            _, rv, _, wsem = bufs[b]
            return pltpu.make_async_copy(rv, out_hbm.at[pl.ds(base + chunk * SC_WINDOW, SC_WINDOW)], wsem)

        fetch(0, 0).start()

        @pl.loop(0, n_chunks, step=2)
        def _(chunk):
            for b in range(2):
                cur = chunk + b

                @pl.when(cur >= 1)
                def _():
                    write(cur - 1, 1 - b).wait()

                @pl.when(cur + 1 < n_chunks)
                def _():
                    fetch(cur + 1, 1 - b).start()

                pltpu.make_async_copy(table_hbm.at[bufs[b][0]], bufs[b][1], bufs[b][2]).wait()
                write(cur, b).start()

        write(n_chunks - 1, (n_chunks - 1) % 2).wait()

    return gather(table, idx)


def _expert_kernel(be_ref, nv_ref, x_ref, wg_ref, wu_ref, wd_ref, y_ref):
    rowi = lax.broadcasted_iota(I32, x_ref.shape, 0)
    lo, hi = _unpack_pairs(jnp.where(rowi < nv_ref[pl.program_id(0)], x_ref[...], 0))
    xb = jnp.concatenate([lo.astype(BF16), hi.astype(BF16)], axis=1)
    gate = jnp.dot(xb, wg_ref[0], preferred_element_type=F32)
    up = jnp.dot(xb, wu_ref[0], preferred_element_type=F32)
    hid = (gate * jax.nn.sigmoid(gate) * up).astype(BF16)
    y_ref[...] = _pack_pairs(jnp.dot(hid, wd_ref[0], preferred_element_type=F32))


def _experts(x_rows, block_e, block_valid, w_gate, w_up, w_down):
    n_blocks = block_e.shape[0]
    d = w_gate.shape[1]
    de = w_gate.shape[2]
    grid_spec = pltpu.PrefetchScalarGridSpec(
        num_scalar_prefetch=2,
        grid=(n_blocks,),
        in_specs=[pl.BlockSpec((MOE_BLOCK, d // 2), lambda i, be, nv: (i, 0)),
                  pl.BlockSpec((1, d, de), lambda i, be, nv: (be[i], 0, 0)),
                  pl.BlockSpec((1, d, de), lambda i, be, nv: (be[i], 0, 0)),
                  pl.BlockSpec((1, de, d), lambda i, be, nv: (be[i], 0, 0))],
        out_specs=pl.BlockSpec((MOE_BLOCK, d // 2), lambda i, be, nv: (i, 0)),
    )
    return pl.pallas_call(
        _expert_kernel,
        out_shape=jax.ShapeDtypeStruct((n_blocks * MOE_BLOCK, d // 2), I32),
        grid_spec=grid_spec,
        compiler_params=_params(("arbitrary",)),
        name="experts",
    )(block_e, block_valid, x_rows, w_gate, w_up, w_down)


def _combine_kernel(y0_ref, y1_ref, x1_ref, wt_ref, g2_ref, gain_ref, o_ref):
    y0 = jnp.concatenate(_unpack_pairs(y0_ref[...]), axis=1)
    y1 = jnp.concatenate(_unpack_pairs(y1_ref[...]), axis=1)
    f = wt_ref[:, 0:1] * y0 + wt_ref[:, 1:2] * y1
    o_ref[...] = x1_ref[...] + g2_ref[0] * _rms(f, gain_ref[...])


def _combine(y2, wts, x1, gate2, gain, tm, row0):
    n, d = x1.shape
    s = n // gate2.shape[0]
    nt = y2.shape[0] // (2 * tm)
    t0 = row0 // tm
    return pl.pallas_call(
        _combine_kernel,
        out_shape=jax.ShapeDtypeStruct((n, d), F32),
        grid=(nt,),
        in_specs=[pl.BlockSpec((tm, d // 2), lambda i: (i, 0)),
                  pl.BlockSpec((tm, d // 2), lambda i: (nt + i, 0)),
                  pl.BlockSpec((tm, d), lambda i: (t0 + i, 0)),
                  pl.BlockSpec((tm, 2), lambda i: (t0 + i, 0)),
                  pl.BlockSpec((1, 1, d), lambda i: ((row0 + i * tm) // s, 0, 0)),
                  pl.BlockSpec((1, d), lambda i: (0, 0))],
        out_specs=pl.BlockSpec((tm, d), lambda i: (t0 + i, 0)),
        input_output_aliases={2: 0},
        compiler_params=_params(("parallel",)),
        name="combine",
    )(y2, y2, x1, wts, gate2, gain)


def _layer(x, mod, bias, bfar, p):
    bsz, s, d = x.shape
    n_tok = bsz * s
    aw = ATT_HEADS * HEAD_DIM
    sw = p["ssm_d"].shape[0]

    w_in = p["w_in"]
    w_rest = jnp.concatenate([w_in[:, :3 * aw], w_in[:, 3 * aw + sw:]], axis=1).astype(BF16)
    w_ut = w_in[:, 3 * aw:3 * aw + sw].T.astype(BF16)
    q, k, v, u_t, sga, sgs = _inproj(x, mod, p["g_pre_mix"].reshape(1, d), w_rest, w_ut, tm=1024)
    att = _moba(q, k, v, bias, bfar)
    tables = _ssm_tables(p["ssm_lambda_re"], p["ssm_lambda_im"], p["ssm_log_dt"], p["ssm_b_re"],
                         p["ssm_b_im"], p["ssm_c_re"], p["ssm_c_im"], p["ssm_d"])
    y_t = _ssm(u_t, *tables)

    w_rt = jnp.concatenate([p["w_router_expert"].T, p["w_router_group"].T,
                            jnp.zeros((ROUTER_ROWS - N_EXPERTS - N_GROUPS, d), F32)], axis=0)
    b_rt = jnp.concatenate([p["b_router_expert"], p["b_router_group"],
                            jnp.zeros((ROUTER_ROWS - N_EXPERTS - N_GROUPS,), F32)]).reshape(ROUTER_ROWS, 1)
    x1, h2, logits_t = _merge(
        x, att, y_t, sga, sgs, mod, p["g_post_mix"].reshape(1, d), p["g_pre_ffn"].reshape(1, d),
        p["w_att_out"].astype(BF16), p["w_glu_val"].astype(BF16), p["w_glu_gate"].astype(BF16),
        p["w_mix_out"].astype(BF16), w_rt, b_rt, tm=512)

    eid, wts, dest, counts = _route(logits_t, tn=1024)
    counts = counts[:, 0]
    padded = (counts + MOE_BLOCK - 1) // MOE_BLOCK * MOE_BLOCK
    pend = jnp.cumsum(padded)
    n_blocks = -(-(n_tok * 2) // MOE_BLOCK) + N_EXPERTS
    block_start = jnp.arange(n_blocks, dtype=I32) * MOE_BLOCK
    block_e = jnp.minimum(jnp.sum((pend[None, :] <= block_start[:, None]).astype(I32), axis=1), N_EXPERTS - 1)
    block_valid = jnp.clip(counts[block_e] - (block_start - (pend - padded)[block_e]), 0, MOE_BLOCK)
    x_rows = _sc_dispatch(h2.reshape(n_tok, d // 2), dest[0], dest[1], n_blocks * MOE_BLOCK)
    y_rows = _experts(x_rows, block_e, block_valid, p["w_exp_gate"].astype(BF16), p["w_exp_up"].astype(BF16),
                      p["w_exp_down"].astype(BF16))
    out = x1.reshape(n_tok, d)
    chunk = n_tok // COMBINE_CHUNKS
    for c in range(COMBINE_CHUNKS):
        y2 = _sc_gather(y_rows, dest[:, c * chunk:(c + 1) * chunk].reshape(-1))
        out = _combine(y2, wts.T, out, mod[:, 5:6, :], p["g_post_ffn"].reshape(1, d), tm=512, row0=c * chunk)
    return out.reshape(bsz, s, d)


def kernel(x, c, rel_bias, w_ada, b_ada, g_pre_mix, g_post_mix, w_in, w_att_out, ssm_lambda_re, ssm_lambda_im, ssm_log_dt, ssm_b_re, ssm_b_im, ssm_c_re, ssm_c_im, ssm_d, w_glu_val, w_glu_gate, w_mix_out, g_pre_ffn, g_post_ffn, w_router_group, b_router_group, w_router_expert, b_router_expert, w_exp_gate, w_exp_up, w_exp_down):
    layered = dict(
        w_ada=w_ada, b_ada=b_ada, g_pre_mix=g_pre_mix, g_post_mix=g_post_mix, w_in=w_in,
        w_att_out=w_att_out, ssm_lambda_re=ssm_lambda_re, ssm_lambda_im=ssm_lambda_im,
        ssm_log_dt=ssm_log_dt, ssm_b_re=ssm_b_re, ssm_b_im=ssm_b_im, ssm_c_re=ssm_c_re,
        ssm_c_im=ssm_c_im, ssm_d=ssm_d, w_glu_val=w_glu_val, w_glu_gate=w_glu_gate,
        w_mix_out=w_mix_out, g_pre_ffn=g_pre_ffn, g_post_ffn=g_post_ffn,
        w_router_group=w_router_group, b_router_group=b_router_group,
        w_router_expert=w_router_expert, b_router_expert=b_router_expert,
        w_exp_gate=w_exp_gate, w_exp_up=w_exp_up, w_exp_down=w_exp_down)
    depth = w_ada.shape[0]
    bsz, d = c.shape
    bias = _bias_tiles(rel_bias)
    far_bucket = np.unique(_t5_bucket_np(np.arange(MOBA_BLOCK + 1, max(x.shape[1], MOBA_BLOCK + 2))))
    assert far_bucket.size == 1
    bfar = rel_bias[int(far_bucket[0])] * LOG2E
    for l in range(depth):
        p = {name: a[l] for name, a in layered.items()}
        mod = _ada(c, p["w_ada"], p["b_ada"]).reshape(bsz, -1, d)
        x = _layer(x, mod, bias, bfar, p)
    return x
```

```python
import functools
import math

import numpy as np
import jax
import jax.numpy as jnp
from jax import lax
from jax.experimental import pallas as pl
from jax.experimental.pallas import tpu as pltpu
from jax.experimental.pallas import tpu_sc as plsc

F32 = jnp.float32
BF16 = jnp.bfloat16
I32 = jnp.int32

ATT_HEADS = 8
HEAD_DIM = 64
MOBA_BLOCK = 256
MOBA_TOPK = 3
MOBA_QBLOCKS = 4
NUM_BUCKETS = 32
MAX_DISTANCE = 128
SSM_GROUP = 16
SSM_STATE = 64
SSM_CHUNK = 128
N_GROUPS = 4
EXPERTS_PER_GROUP = 8
N_EXPERTS = N_GROUPS * EXPERTS_PER_GROUP
MOE_BLOCK = 512
RMS_EPS = 1e-6
NEG_INF = -1e30
LOG2E = math.log2(math.e)
LANES = 128
ROUTER_ROWS = 40
VMEM_LIMIT = 56 * 1024 * 1024
SC_CORES = 2
SC_SUBCORES = 16
SC_WINDOW = 64
COMBINE_CHUNKS = 4

_NT = (((1,), (1,)), ((), ()))
_NN = (((1,), (0,)), ((), ()))


def _params(sem, vmem=VMEM_LIMIT):
    return pltpu.CompilerParams(dimension_semantics=sem, vmem_limit_bytes=vmem)


def _split_bf16(a):
    hi = a.astype(BF16)
    lo = (a - hi.astype(F32)).astype(BF16)
    return hi, lo


def _dot3(a, b, dims):
    a_hi, a_lo = _split_bf16(a)
    b_hi, b_lo = _split_bf16(b)
    dg = functools.partial(lax.dot_general, dimension_numbers=dims, preferred_element_type=F32)
    return dg(a_hi, b_hi) + (dg(a_hi, b_lo) + dg(a_lo, b_hi))


def _pack_pairs(a):
    w = a.shape[1] // 2
    bits = pltpu.bitcast(a.astype(BF16).astype(F32), jnp.uint32)
    return pltpu.bitcast((bits[:, :w] >> 16) | (bits[:, w:] & jnp.uint32(0xFFFF0000)), I32)


def _unpack_pairs(packed):
    words = pltpu.bitcast(packed, jnp.uint32)
    lo = pltpu.bitcast(words << 16, F32)
    hi = pltpu.bitcast(words & jnp.uint32(0xFFFF0000), F32)
    return lo, hi


def _ada_kernel(c_ref, w_ref, b_ref, o_ref):
    c = c_ref[...]
    ca = c * jax.nn.sigmoid(c)
    o_ref[...] = _dot3(ca, w_ref[...], _NN) + b_ref[...]


def _ada(c, w, b):
    bsz, d = c.shape
    n = w.shape[1]
    tn = 1536
    return pl.pallas_call(
        _ada_kernel,
        out_shape=jax.ShapeDtypeStruct((bsz, n), F32),
        grid=(n // tn,),
        in_specs=[pl.BlockSpec((bsz, d), lambda j: (0, 0)),
                  pl.BlockSpec((d, tn), lambda j: (0, j)),
                  pl.BlockSpec((1, tn), lambda j: (0, j))],
        out_specs=pl.BlockSpec((bsz, tn), lambda j: (0, j)),
        compiler_params=_params(("parallel",)),
        name="ada",
    )(c, w, b.reshape(1, n))


def _t5_bucket_np(dist):
    n = np.maximum(dist, 0)
    max_exact = NUM_BUCKETS // 2
    nf = np.maximum(n, 1).astype(np.float32)
    large = max_exact + (np.log(nf / np.float32(max_exact)) / np.float32(math.log(MAX_DISTANCE / max_exact))
                         * np.float32(NUM_BUCKETS - max_exact)).astype(np.int32)
    large = np.minimum(large, NUM_BUCKETS - 1)
    return np.where(n < max_exact, n, large).astype(np.int32)


def _bias_kernel(rb_ref, bk_ref, o_ref):
    h = pl.program_id(0)
    for t in range(2):
        bk = bk_ref[t]
        acc = jnp.where(bk < 0, NEG_INF, 0.0).astype(F32)
        for b in range(NUM_BUCKETS):
            acc = jnp.where(bk == b, rb_ref[b, h] * LOG2E, acc)
        o_ref[0, t] = acc


def _bias_tiles(rel_bias):
    qi = np.arange(MOBA_BLOCK)[:, None]
    kj = np.arange(MOBA_BLOCK)[None, :]
    own = np.where(qi >= kj, _t5_bucket_np(qi - kj), -1)
    adj = _t5_bucket_np(qi - kj + MOBA_BLOCK)
    buckets = jnp.asarray(np.stack([own, adj]).astype(np.int32))
    return pl.pallas_call(
        _bias_kernel,
        out_shape=jax.ShapeDtypeStruct((ATT_HEADS, 2, MOBA_BLOCK, MOBA_BLOCK), F32),
        grid=(ATT_HEADS,),
        in_specs=[pl.BlockSpec(memory_space=pltpu.SMEM),
                  pl.BlockSpec((2, MOBA_BLOCK, MOBA_BLOCK), lambda h: (0, 0, 0))],
        out_specs=pl.BlockSpec((1, 2, MOBA_BLOCK, MOBA_BLOCK), lambda h: (h, 0, 0, 0)),
        compiler_params=_params(("parallel",)),
        name="t5_bias",
    )(rel_bias, buckets)


def _rms(x, gain):
    ms = jnp.mean(x * x, axis=-1, keepdims=True)
    return x * lax.rsqrt(ms + RMS_EPS) * gain


def _inproj_kernel(x_ref, mod_ref, g_ref, w_ref, wut_ref,
                   q_ref, k_ref, v_ref, ut_ref, sga_ref, sgs_ref):
    aw = q_ref.shape[2]
    d = x_ref.shape[2]
    x = x_ref[0]
    h = _rms(x, g_ref[...]) * (1.0 + mod_ref[0, 1:2, :]) + mod_ref[0, 0:1, :]
    hb = h.astype(BF16)

    def proj(lo, width):
        return jnp.dot(hb, w_ref[:, lo:lo + width], preferred_element_type=F32)

    q_ref[0] = (proj(0, aw) * (HEAD_DIM ** -0.5 * LOG2E)).astype(BF16)
    k_ref[0] = proj(aw, aw).astype(BF16)
    v_ref[0] = proj(2 * aw, aw).astype(BF16)
    ut_ref[0] = lax.dot_general(wut_ref[...], hb, _NT, preferred_element_type=F32)
    sga_ref[0] = jax.nn.sigmoid(proj(3 * aw, d)).astype(BF16)
    sgs_ref[0] = jax.nn.sigmoid(proj(3 * aw + d, d)).astype(BF16)


def _inproj(x, mod, gain, w_rest, w_ut, tm):
    bsz, s, d = x.shape
    aw = ATT_HEADS * HEAD_DIM
    sw = w_ut.shape[0]
    tok = lambda width, dt: jax.ShapeDtypeStruct((bsz, s, width), dt)
    tspec = lambda width: pl.BlockSpec((1, tm, width), lambda b, i: (b, i, 0))
    return pl.pallas_call(
        _inproj_kernel,
        out_shape=(tok(aw, BF16), tok(aw, BF16), tok(aw, BF16),
                   jax.ShapeDtypeStruct((bsz, sw, s), F32), tok(d, BF16), tok(d, BF16)),
        grid=(bsz, s // tm),
        in_specs=[tspec(d),
                  pl.BlockSpec((1, mod.shape[1], d), lambda b, i: (b, 0, 0)),
                  pl.BlockSpec((1, d), lambda b, i: (0, 0)),
                  pl.BlockSpec(w_rest.shape, lambda b, i: (0, 0), pipeline_mode=pl.Buffered(1)),
                  pl.BlockSpec(w_ut.shape, lambda b, i: (0, 0), pipeline_mode=pl.Buffered(1))],
        out_specs=(tspec(aw), tspec(aw), tspec(aw),
                   pl.BlockSpec((1, sw, tm), lambda b, i: (b, 0, i)), tspec(d), tspec(d)),
        compiler_params=_params(("parallel", "parallel")),
        name="inproj",
    )(x, mod, gain, w_rest, w_ut)


def _moba_kernel(bfar_ref, q_ref, k_ref, v_ref, bias_ref, o_ref,
                 kmh_sc, kml_sc, qaug_sc, m_sc, acc_sc):
    blk = MOBA_BLOCK
    nq = MOBA_QBLOCKS
    rows_h = nq * blk
    sp = pl.program_id(1)
    i_base = nq * sp
    s = k_ref.shape[1]
    nb = s // blk
    nbp = kmh_sc.shape[0]
    npair = q_ref.shape[2] // LANES
    lane = lax.broadcasted_iota(I32, (blk, LANES), 1)
    low_half = lane < HEAD_DIM

    @pl.when(sp == 0)
    def _():
        r = lax.broadcasted_iota(I32, (nbp, s), 0)
        c = lax.broadcasted_iota(I32, (nbp, s), 1)
        avg = jnp.where((c >= r * blk) & (c < (r + 1) * blk), 1.0 / blk, 0.0).astype(BF16)
        km = jnp.dot(avg, k_ref[0], preferred_element_type=F32)
        hi, lo = _split_bf16(km)
        kmh_sc[...] = hi
        kml_sc[...] = lo

    row = lax.broadcasted_iota(I32, (nbp, blk), 0)
    half_k = lax.broadcasted_iota(I32, (nbp, LANES), 1) < HEAD_DIM
    for qb in range(nq):
        i = i_base + qb
        for pr in range(npair):
            q2 = q_ref[0, qb * blk:(qb + 1) * blk, pr * LANES:(pr + 1) * LANES]
            kmh = kmh_sc[:, pr * LANES:(pr + 1) * LANES]
            kml = kml_sc[:, pr * LANES:(pr + 1) * LANES]
            for hh in range(2):
                mine = half_k if hh == 0 else jnp.logical_not(half_k)
                gate = (lax.dot_general(jnp.where(mine, kmh, jnp.zeros_like(kmh)), q2, _NT,
                                        preferred_element_type=F32)
                        + lax.dot_general(jnp.where(mine, kml, jnp.zeros_like(kml)), q2, _NT,
                                          preferred_element_type=F32))
                gate = jnp.where(row < i, gate, NEG_INF)
                cnt = jnp.zeros((nbp, blk), F32)
                for m in range(nb):
                    other = jnp.broadcast_to(gate[m:m + 1, :], (nbp, blk))
                    tie = jnp.where(row > m, 1.0, 0.0)
                    cnt = cnt + jnp.where(other > gate, 1.0, 0.0) + jnp.where(other == gate, tie, 0.0)
                chosen = jnp.where(row < i, cnt, float(MOBA_TOPK)) < float(MOBA_TOPK)
                keep_t = jnp.where(row == i, 0.0, jnp.where(chosen, 0.0, NEG_INF))
                keep_t = jnp.concatenate([keep_t, jnp.full((LANES - nbp, blk), NEG_INF, F32)], axis=0)
                qh = jnp.where(low_half if hh == 0 else jnp.logical_not(low_half), q2, jnp.zeros_like(q2))
                r0 = hh * rows_h + qb * blk
                qaug_sc[pr, r0:r0 + blk, :] = jnp.concatenate([qh, keep_t.T.astype(BF16)], axis=1)

    one_hi = jnp.where(lane == HEAD_DIM, 1.0, 0.0).astype(BF16)
    one_lo = jnp.where(lane == 0, 1.0, 0.0).astype(BF16)

    def tile(j, bias_of_head, first, lo):
        start = pl.multiple_of(j * blk, blk)
        onehot = jnp.where(lane == j, 1.0, 0.0).astype(BF16)
        nrows = rows_h - lo * blk
        for pr in range(npair):
            kj = k_ref[0, pl.ds(start, blk), pr * LANES:(pr + 1) * LANES]
            vj = v_ref[0, pl.ds(start, blk), pr * LANES:(pr + 1) * LANES]
            kaug = jnp.concatenate([kj, onehot], axis=1)
            vaug = (jnp.where(low_half, vj, one_hi), jnp.where(low_half, one_lo, vj))
            if lo == 0:
                sc_pair = lax.dot_general(qaug_sc[pr], kaug, _NT, preferred_element_type=F32)
            for hh in range(2):
                h = 2 * pr + hh
                rows = slice(lo * blk, rows_h)
                if lo == 0:
                    sc = sc_pair[hh * rows_h:(hh + 1) * rows_h]
                else:
                    sc = lax.dot_general(qaug_sc[pr, hh * rows_h + lo * blk:(hh + 1) * rows_h, :], kaug, _NT,
                                         preferred_element_type=F32)
                bias = bias_of_head(h)
                scalar_bias = jnp.ndim(bias) == 0
                if scalar_bias:
                    m_cur = jnp.max(sc, axis=1, keepdims=True) + bias
                else:
                    sc = sc + bias
                    m_cur = jnp.max(sc, axis=1, keepdims=True)
                if first:
                    m_new = jnp.broadcast_to(m_cur, (nrows, LANES))
                else:
                    m_prev = m_sc[h, rows, :]
                    m_new = jnp.maximum(m_prev, m_cur)
                shift = m_new - bias if scalar_bias else m_new
                pexp = jnp.exp2(sc - jnp.concatenate([shift, shift], axis=1))
                pv = jnp.dot(pexp.astype(BF16), vaug[hh], preferred_element_type=F32)
                if first:
                    acc_sc[h, rows, :] = pv
                else:
                    acc_sc[h, rows, :] = jnp.exp2(m_prev - m_new) * acc_sc[h, rows, :] + pv
                m_sc[h, rows, :] = m_new

    def mixed_bias(h, kinds):
        parts = {"own": lambda: bias_ref[h, 0], "prev": lambda: bias_ref[h, 1],
                 "far": lambda: jnp.full((blk, blk), bfar_ref[h], F32)}
        return jnp.concatenate([parts[kind]() for kind in kinds], axis=0)

    def kinds_for(t, lo):
        return ["own" if qb == t else "prev" if qb == t + 1 else "far" for qb in range(lo, nq)]

    tile(i_base, lambda h: mixed_bias(h, kinds_for(0, 0)), True, 0)
    for t in range(1, nq):
        tile(i_base + t, lambda h, t=t: mixed_bias(h, kinds_for(t, t)), False, t)

    @pl.when(sp >= 1)
    def _():
        tile(i_base - 1, lambda h: mixed_bias(h, kinds_for(-1, 0)), False, 0)

    def far(j, carry):
        tile(j, lambda h: bfar_ref[h], False, 0)
        return carry

    lax.fori_loop(0, jnp.maximum(i_base - 1, 0), far, 0)

    lane_all = lax.broadcasted_iota(I32, (rows_h, LANES), 1) < HEAD_DIM
    for pr in range(npair):
        acc_e = acc_sc[2 * pr]
        acc_o = acc_sc[2 * pr + 1]
        out = jnp.where(lane_all, acc_e / acc_e[:, HEAD_DIM:HEAD_DIM + 1], acc_o / acc_o[:, 0:1])
        o_ref[0, :, pr * LANES:(pr + 1) * LANES] = out.astype(BF16)


def _moba(q, k, v, bias, bfar):
    bsz, s, aw = q.shape
    blk = MOBA_BLOCK
    nq = MOBA_QBLOCKS
    nb = s // blk
    nbp = -(-nb // 16) * 16
    heads = bias.shape[0]
    grid_spec = pltpu.PrefetchScalarGridSpec(
        num_scalar_prefetch=1,
        grid=(bsz, nb // nq),
        in_specs=[pl.BlockSpec((1, nq * blk, aw), lambda b, i, _: (b, i, 0)),
                  pl.BlockSpec((1, s, aw), lambda b, i, _: (b, 0, 0)),
                  pl.BlockSpec((1, s, aw), lambda b, i, _: (b, 0, 0)),
                  pl.BlockSpec(bias.shape, lambda b, i, _: (0, 0, 0, 0))],
        out_specs=pl.BlockSpec((1, nq * blk, aw), lambda b, i, _: (b, i, 0)),
        scratch_shapes=[pltpu.VMEM((nbp, aw), BF16), pltpu.VMEM((nbp, aw), BF16),
                        pltpu.VMEM((heads // 2, 2 * nq * blk, 2 * LANES), BF16),
                        pltpu.VMEM((heads, nq * blk, LANES), F32), pltpu.VMEM((heads, nq * blk, LANES), F32)],
    )
    return pl.pallas_call(
        _moba_kernel,
        out_shape=jax.ShapeDtypeStruct((bsz, s, aw), BF16),
        grid_spec=grid_spec,
        compiler_params=_params(("parallel", "arbitrary")),
        name="moba",
    )(bfar, q, k, v, bias)


def _ssm_tables(lam_re, lam_im, log_dt, b_re, b_im, c_re, c_im, d_skip):
    L = SSM_CHUNK
    g = lam_re.shape[0]
    dt = jnp.exp(log_dt)[:, None]
    lr, li = lam_re, lam_im
    mag = jnp.exp(lr * dt)
    ab_re, ab_im = mag * jnp.cos(li * dt), mag * jnp.sin(li * dt)
    den = lr * lr + li * li
    nr, ni = ab_re - 1.0, ab_im
    f_re, f_im = (nr * lr + ni * li) / den, (ni * lr - nr * li) / den
    bb_re = f_re[..., None] * b_re - f_im[..., None] * b_im
    bb_im = f_re[..., None] * b_im + f_im[..., None] * b_re
    n = jnp.arange(L + 1, dtype=F32)[:, None, None]
    pw_mag = jnp.exp(n * (lr * dt)[None])
    pw_re, pw_im = pw_mag * jnp.cos(n * (li * dt)[None]), pw_mag * jnp.sin(n * (li * dt)[None])
    ca_re = c_re[None] * pw_re[:L, :, None, :] - c_im[None] * pw_im[:L, :, None, :]
    ca_im = c_re[None] * pw_im[:L, :, None, :] + c_im[None] * pw_re[:L, :, None, :]
    hp = lax.Precision.HIGHEST
    kern = (jnp.einsum("ngcp,gpd->gdcn", ca_re, bb_re, precision=hp)
            - jnp.einsum("ngcp,gpd->gdcn", ca_im, bb_im, precision=hp))
    skip = jnp.asarray(np.eye(SSM_GROUP, dtype=np.float32))[None, :, :, None] * d_skip.reshape(g, 1, SSM_GROUP, 1)
    kern = kern + skip * jnp.asarray((np.arange(L) == 0).astype(np.float32))
    kern = kern.reshape(g, SSM_GROUP * SSM_GROUP, L)
    rev_re, rev_im = pw_re[L - 1::-1][:L], pw_im[L - 1::-1][:L]
    win_re = rev_re[..., None] * bb_re[None] - rev_im[..., None] * bb_im[None]
    win_im = rev_re[..., None] * bb_im[None] + rev_im[..., None] * bb_re[None]
    w_in = jnp.concatenate([win_re, win_im], axis=2)
    w_in = w_in.transpose(1, 3, 0, 2).reshape(g, SSM_GROUP * L, 2 * SSM_STATE)
    fw_re, fw_im = pw_re[1:], pw_im[1:]
    wo_re = c_re[None] * fw_re[:, :, None, :] - c_im[None] * fw_im[:, :, None, :]
    wo_im = -(c_re[None] * fw_im[:, :, None, :] + c_im[None] * fw_re[:, :, None, :])
    w_out = jnp.concatenate([wo_re, wo_im], axis=3)
    w_out = w_out.transpose(1, 3, 2, 0).reshape(g, 2 * SSM_STATE, SSM_GROUP * L)
    a_chunk = jnp.stack([jnp.concatenate([pw_re[L], pw_re[L]], axis=-1),
                         jnp.concatenate([-pw_im[L], pw_im[L]], axis=-1)], axis=1)
    return kern, w_in.astype(BF16), w_out.astype(BF16), a_chunk


def _ssm_kernel(u_ref, kern_ref, win_ref, wout_ref, ac_ref, y_ref, toep_sc):
    bsz, ng, s = u_ref.shape
    L = SSM_CHUNK
    nc = s // L
    s_ix = lax.broadcasted_iota(I32, (L, L), 0)
    t_ix = lax.broadcasted_iota(I32, (L, L), 1)

    def build(cp, carry):
        r0 = pl.multiple_of(cp * L, L)
        for c in range(ng):
            vec = kern_ref[0, pl.ds(cp * ng + c, 1), :]
            lagged = pltpu.roll(jnp.broadcast_to(vec, (L, L)), 0, 1, stride=1, stride_axis=0)
            toep_sc[pl.ds(r0, L), c * L:(c + 1) * L] = jnp.where(t_ix >= s_ix, lagged, 0.0).astype(BF16)
        return carry

    lax.fori_loop(0, ng, build, 0)

    u = jnp.concatenate(
        [jnp.concatenate([u_ref[:, c, ch * L:(ch + 1) * L] for c in range(ng)], axis=1) for ch in range(nc)],
        axis=0).astype(BF16)
    st = jnp.dot(u, win_ref[0], preferred_element_type=F32)
    a1 = ac_ref[0, 0:1, :]
    a2 = ac_ref[0, 1:2, :]
    st_sw = pltpu.roll(st, SSM_STATE, 1)
    state = jnp.zeros((bsz, 2 * SSM_STATE), F32)
    swapped = state
    prevs = []
    for ch in range(nc):
        prevs.append(state)
        rows = slice(ch * bsz, (ch + 1) * bsz)
        state, swapped = a1 * state + a2 * swapped + st[rows], a1 * swapped - a2 * state + st_sw[rows]
    prev = jnp.concatenate(prevs, axis=0).astype(BF16)
    y = (jnp.dot(u, toep_sc[...], preferred_element_type=F32)
         + jnp.dot(prev, wout_ref[0], preferred_element_type=F32))
    for ch in range(nc):
        for c in range(ng):
            y_ref[:, c, ch * L:(ch + 1) * L] = y[ch * bsz:(ch + 1) * bsz, c * L:(c + 1) * L]


def _ssm(u_t, kern, w_in, w_out, a_chunk):
    bsz, sw, s = u_t.shape
    g = sw // SSM_GROUP
    blk = pl.BlockSpec((bsz, SSM_GROUP, s), lambda j: (0, j, 0))
    per_group = lambda a: pl.BlockSpec((1,) + a.shape[1:], lambda j: (j, 0, 0))
    return pl.pallas_call(
        _ssm_kernel,
        out_shape=jax.ShapeDtypeStruct(u_t.shape, F32),
        grid=(g,),
        in_specs=[blk, per_group(kern), per_group(w_in), per_group(w_out), per_group(a_chunk)],
        out_specs=blk,
        scratch_shapes=[pltpu.VMEM((SSM_GROUP * SSM_CHUNK, SSM_GROUP * SSM_CHUNK), BF16)],
        compiler_params=_params(("parallel",)),
        name="ssm",
    )(u_t, kern, w_in, w_out, a_chunk)


def _gelu_tanh(x):
    return 0.5 * x * (1.0 + jnp.tanh(math.sqrt(2.0 / math.pi) * (x + 0.044715 * (x * x * x))))


def _merge_kernel(x_ref, att_ref, yt_ref, sga_ref, sgs_ref, mod_ref, gpost_ref, gpre_ref,
                  wao_ref, wgv_ref, wgg_ref, wmo_ref, wrt_ref, brt_ref,
                  x1_ref, h2_ref, lt_ref):
    a_br = jnp.dot(att_ref[0], wao_ref[...], preferred_element_type=F32)
    z = _gelu_tanh(yt_ref[0]).T.astype(BF16)
    s_br = (jnp.dot(z, wgv_ref[...], preferred_element_type=F32)
            * jax.nn.sigmoid(jnp.dot(z, wgg_ref[...], preferred_element_type=F32)))
    merged = sga_ref[0].astype(F32) * a_br + sgs_ref[0].astype(F32) * s_br
    mix = jnp.dot(merged.astype(BF16), wmo_ref[...], preferred_element_type=F32)
    x1 = x_ref[0] + mod_ref[0, 2:3, :] * _rms(mix, gpost_ref[...])
    x1_ref[0] = x1
    h2 = _rms(x1, gpre_ref[...]) * (1.0 + mod_ref[0, 4:5, :]) + mod_ref[0, 3:4, :]
    h2_ref[0] = _pack_pairs(h2)
    lt_ref[...] = _dot3(wrt_ref[...], h2, _NT) + brt_ref[...]


def _merge(x, att, yt, sga, sgs, mod, g_post, g_pre, w_ao, w_gv, w_gg, w_mo, w_rt, b_rt, tm):
    bsz, s, d = x.shape
    aw = att.shape[2]
    sw = yt.shape[1]
    nt = s // tm
    tspec = lambda width: pl.BlockSpec((1, tm, width), lambda b, i: (b, i, 0))
    full = lambda a: pl.BlockSpec(a.shape, lambda b, i: (0,) * a.ndim, pipeline_mode=pl.Buffered(1))
    return pl.pallas_call(
        _merge_kernel,
        out_shape=(jax.ShapeDtypeStruct((bsz, s, d), F32), jax.ShapeDtypeStruct((bsz, s, d // 2), I32),
                   jax.ShapeDtypeStruct((ROUTER_ROWS, bsz * s), F32)),
        grid=(bsz, nt),
        in_specs=[tspec(d), tspec(aw), pl.BlockSpec((1, sw, tm), lambda b, i: (b, 0, i)),
                  tspec(d), tspec(d),
                  pl.BlockSpec((1, mod.shape[1], d), lambda b, i: (b, 0, 0)),
                  full(g_post), full(g_pre), full(w_ao), full(w_gv), full(w_gg), full(w_mo),
                  full(w_rt), full(b_rt)],
        out_specs=(tspec(d), tspec(d // 2), pl.BlockSpec((ROUTER_ROWS, tm), lambda b, i: (0, b * nt + i))),
        compiler_params=_params(("parallel", "parallel")),
        name="merge",
    )(x, att, yt, sga, sgs, mod, g_post, g_pre, w_ao, w_gv, w_gg, w_mo, w_rt, b_rt)


def _route_kernel(lt_ref, eid_ref, wt_ref, dest_ref, cnt_ref, base_sc, pstart_sc):
    tn = lt_ref.shape[1]
    epg = EXPERTS_PER_GROUP
    ph = pl.program_id(0)
    step = pl.program_id(1)

    @pl.when((ph == 0) & (step == 0))
    def _():
        base_sc[...] = jnp.zeros_like(base_sc)

    @pl.when((ph == 1) & (step == 0))
    def _():
        total = base_sc[...]
        cnt_ref[...] = total.astype(I32)
        padded = jnp.floor((total + (MOE_BLOCK - 1.0)) * (1.0 / MOE_BLOCK)) * MOE_BLOCK
        r = lax.broadcasted_iota(I32, (N_EXPERTS, N_EXPERTS), 0)
        c = lax.broadcasted_iota(I32, (N_EXPERTS, N_EXPERTS), 1)
        before = jnp.where(c < r, 1.0, 0.0)
        pstart_sc[...] = _dot3(before, padded, _NN)
        base_sc[...] = jnp.zeros_like(base_sc)

    row8 = lax.broadcasted_iota(I32, (epg, tn), 0)
    gl = lt_ref[N_EXPERTS:N_EXPERTS + epg, :]
    gl = jnp.where(row8 < N_GROUPS, gl, -jnp.inf)
    gmax = jnp.max(gl, axis=0, keepdims=True)
    gidx = jnp.min(jnp.where(gl == gmax, row8, epg), axis=0, keepdims=True)

    el = jnp.zeros((epg, tn), F32)
    for g in range(N_GROUPS):
        el = jnp.where(gidx == g, lt_ref[g * epg:(g + 1) * epg, :], el)
    m1 = jnp.max(el, axis=0, keepdims=True)
    i1 = jnp.min(jnp.where(el == m1, row8, epg), axis=0, keepdims=True)
    el2 = jnp.where(row8 == i1, -jnp.inf, el)
    m2 = jnp.max(el2, axis=0, keepdims=True)
    i2 = jnp.min(jnp.where(el2 == m2, row8, epg), axis=0, keepdims=True)
    e1 = gidx * epg + i1
    e2 = gidx * epg + i2

    row32 = lax.broadcasted_iota(I32, (N_EXPERTS, tn), 0)
    hit1 = row32 == e1
    hit2 = row32 == e2
    onehot = jnp.where(hit1, 1.0, jnp.where(hit2, 1.0, 0.0))

    @pl.when(ph == 1)
    def _():
        g_p = 1.0 / jnp.sum(jnp.exp(gl - gmax), axis=0, keepdims=True)
        zsum = jnp.sum(jnp.exp(el - m1), axis=0, keepdims=True)
        p1 = 1.0 / zsum
        p2 = jnp.exp(m2 - m1) / zsum
        sr = lax.broadcasted_iota(I32, (tn, tn), 0)
        tc = lax.broadcasted_iota(I32, (tn, tn), 1)
        earlier = jnp.where(sr < tc, 1.0, 0.0).astype(BF16)
        place = (jnp.dot(onehot.astype(BF16), earlier, preferred_element_type=F32)
                 + base_sc[:, 0:1] + pstart_sc[:, 0:1])
        d1 = jnp.sum(jnp.where(hit1, place, 0.0), axis=0, keepdims=True)
        d2 = jnp.sum(jnp.where(hit2, place, 0.0), axis=0, keepdims=True)
        eid_ref[...] = jnp.concatenate([e1, e2], axis=0)
        wt_ref[...] = jnp.concatenate([g_p * p1 / (p1 + p2), g_p * p2 / (p1 + p2)], axis=0)
        dest_ref[...] = jnp.concatenate([d1, d2], axis=0).astype(I32)

    base_sc[...] = base_sc[...] + jnp.sum(onehot, axis=1, keepdims=True)


def _route(logits_t, tn):
    n = logits_t.shape[1]
    two = lambda dt: jax.ShapeDtypeStruct((2, n), dt)
    tspec = pl.BlockSpec((2, tn), lambda ph, i: (0, i * ph))
    return pl.pallas_call(
        _route_kernel,
        out_shape=(two(I32), two(F32), two(I32), jax.ShapeDtypeStruct((N_EXPERTS, LANES), I32)),
        grid=(2, n // tn),
        in_specs=[pl.BlockSpec((ROUTER_ROWS, tn), lambda ph, i: (0, i))],
        out_specs=(tspec, tspec, tspec, pl.BlockSpec((N_EXPERTS, LANES), lambda ph, i: (0, 0))),
        scratch_shapes=[pltpu.VMEM((N_EXPERTS, LANES), F32), pltpu.VMEM((N_EXPERTS, LANES), F32)],
        compiler_params=_params(("arbitrary", "arbitrary")),
        name="route",
    )(logits_t)


def _sc_mesh():
    return plsc.VectorSubcoreMesh(core_axis_name="c", subcore_axis_name="s",
                                  num_cores=SC_CORES, num_subcores=SC_SUBCORES)


def _sc_worker_offset(per_worker):
    return (lax.axis_index("s") * SC_CORES + lax.axis_index("c")) * per_worker


def _sc_dispatch(rows, dest0, dest1, n_rows):
    n_tok, w = rows.shape
    per_worker = n_tok // (SC_CORES * SC_SUBCORES)
    n_chunks = per_worker // SC_WINDOW

    @functools.partial(
        pl.kernel, mesh=_sc_mesh(), out_type=jax.ShapeDtypeStruct((n_rows, w), rows.dtype),
        scratch_types=[pltpu.VMEM((SC_WINDOW,), I32)] * 4 + [pltpu.VMEM((SC_WINDOW, w), rows.dtype)] * 2
        + [pltpu.SemaphoreType.DMA] * 4,
        name="sc_dispatch")
    def scatter(rows_hbm, dest0_hbm, dest1_hbm, x_hbm, i0a, i1a, i0b, i1b, rows_a, rows_b,
                lsem_a, lsem_b, ssem_a, ssem_b):
        base = _sc_worker_offset(per_worker)
        bufs = ((i0a, i1a, rows_a, lsem_a, ssem_a), (i0b, i1b, rows_b, lsem_b, ssem_b))

        def load(chunk, b):
            i0, i1, rv, lsem, _ = bufs[b]
            off = base + chunk * SC_WINDOW
            pltpu.sync_copy(dest0_hbm.at[pl.ds(off, SC_WINDOW)], i0)
            pltpu.sync_copy(dest1_hbm.at[pl.ds(off, SC_WINDOW)], i1)
            return pltpu.make_async_copy(rows_hbm.at[pl.ds(off, SC_WINDOW)], rv, lsem)

        def stores(b):
            i0, i1, rv, _, ssem = bufs[b]
            return (pltpu.make_async_copy(rv, x_hbm.at[i0], ssem),
                    pltpu.make_async_copy(rv, x_hbm.at[i1], ssem))

        load(0, 0).start()

        @pl.loop(0, n_chunks, step=2)
        def _(chunk):
            for b in range(2):
                cur = chunk + b

                @pl.when(cur >= 1)
                def _():
                    for cp in stores(1 - b):
                        cp.wait()

                @pl.when(cur + 1 < n_chunks)
                def _():
                    load(cur + 1, 1 - b).start()

                pltpu.make_async_copy(rows_hbm.at[pl.ds(base, SC_WINDOW)], bufs[b][2], bufs[b][3]).wait()
                for cp in stores(b):
                    cp.start()

        for cp in stores((n_chunks - 1) % 2):
            cp.wait()

    return scatter(rows, dest0, dest1)


def _sc_gather(table, idx):
    n_idx = idx.shape[0]
    w = table.shape[1]
    per_worker = n_idx // (SC_CORES * SC_SUBCORES)
    n_chunks = per_worker // SC_WINDOW

    @functools.partial(
        pl.kernel, mesh=_sc_mesh(), out_type=jax.ShapeDtypeStruct((n_idx, w), table.dtype),
        scratch_types=[pltpu.VMEM((SC_WINDOW,), I32)] * 2 + [pltpu.VMEM((SC_WINDOW, w), table.dtype)] * 2
        + [pltpu.SemaphoreType.DMA] * 4,
        name="sc_gather")
    def gather(table_hbm, idx_hbm, out_hbm, idx_a, idx_b, rows_a, rows_b, gsem_a, gsem_b, wsem_a, wsem_b):
        base = _sc_worker_offset(per_worker)
        bufs = ((idx_a, rows_a, gsem_a, wsem_a), (idx_b, rows_b, gsem_b, wsem_b))

        def fetch(chunk, b):
            iv, rv, gsem, _ = bufs[b]
            pltpu.sync_copy(idx_hbm.at[pl.ds(base + chunk * SC_WINDOW, SC_WINDOW)], iv)
            return pltpu.make_async_copy(table_hbm.at[iv], rv, gsem)

        def write(chunk, b):
            _, rv, _, wsem = bufs[b]
            return pltpu.make_async_copy(rv, out_hbm.at[pl.ds(base + chunk * SC_WINDOW, SC_WINDOW)], wsem)

        fetch(0, 0).start()

        @pl.loop(0, n_chunks, step=2)
        def _(chunk):
            for b in range(2):
                cur = chunk + b

                @pl.when(cur >= 1)
                def _():
                    write(cur - 1, 1 - b).wait()

                @pl.when(cur + 1 < n_chunks)
                def _():
                    fetch(cur + 1, 1 - b).start()

                pltpu.make_async_copy(table_hbm.at[bufs[b][0]], bufs[b][1], bufs[b][2]).wait()
                write(cur, b).start()

        write(n_chunks - 1, (n_chunks - 1) % 2).wait()

    return gather(table, idx)


def _expert_kernel(be_ref, nv_ref, x_ref, wg_ref, wu_ref, wd_ref, y_ref):
    rowi = lax.broadcasted_iota(I32, x_ref.shape, 0)
    lo, hi = _unpack_pairs(jnp.where(rowi < nv_ref[pl.program_id(0)], x_ref[...], 0))
    xb = jnp.concatenate([lo.astype(BF16), hi.astype(BF16)], axis=1)
    gate = jnp.dot(xb, wg_ref[0], preferred_element_type=F32)
    up = jnp.dot(xb, wu_ref[0], preferred_element_type=F32)
    hid = (gate * jax.nn.sigmoid(gate) * up).astype(BF16)
    y_ref[...] = _pack_pairs(jnp.dot(hid, wd_ref[0], preferred_element_type=F32))


def _experts(x_rows, block_e, block_valid, w_gate, w_up, w_down):
    n_blocks = block_e.shape[0]
    d = w_gate.shape[1]
    de = w_gate.shape[2]
    grid_spec = pltpu.PrefetchScalarGridSpec(
        num_scalar_prefetch=2,
        grid=(n_blocks,),
        in_specs=[pl.BlockSpec((MOE_BLOCK, d // 2), lambda i, be, nv: (i, 0)),
                  pl.BlockSpec((1, d, de), lambda i, be, nv: (be[i], 0, 0)),
                  pl.BlockSpec((1, d, de), lambda i, be, nv: (be[i], 0, 0)),
                  pl.BlockSpec((1, de, d), lambda i, be, nv: (be[i], 0, 0))],
        out_specs=pl.BlockSpec((MOE_BLOCK, d // 2), lambda i, be, nv: (i, 0)),
    )
    return pl.pallas_call(
        _expert_kernel,
        out_shape=jax.ShapeDtypeStruct((n_blocks * MOE_BLOCK, d // 2), I32),
        grid_spec=grid_spec,
        compiler_params=_params(("arbitrary",)),
        name="experts",
    )(block_e, block_valid, x_rows, w_gate, w_up, w_down)


def _combine_kernel(y0_ref, y1_ref, x1_ref, wt_ref, g2_ref, gain_ref, o_ref):
    y0 = jnp.concatenate(_unpack_pairs(y0_ref[...]), axis=1)
    y1 = jnp.concatenate(_unpack_pairs(y1_ref[...]), axis=1)
    f = wt_ref[:, 0:1] * y0 + wt_ref[:, 1:2] * y1
    o_ref[...] = x1_ref[...] + g2_ref[0] * _rms(f, gain_ref[...])


def _combine(y2, wts, x1, gate2, gain, tm, row0):
    n, d = x1.shape
    s = n // gate2.shape[0]
    nt = y2.shape[0] // (2 * tm)
    t0 = row0 // tm
    return pl.pallas_call(
        _combine_kernel,
        out_shape=jax.ShapeDtypeStruct((n, d), F32),
        grid=(nt,),
        in_specs=[pl.BlockSpec((tm, d // 2), lambda i: (i, 0)),
                  pl.BlockSpec((tm, d // 2), lambda i: (nt + i, 0)),
                  pl.BlockSpec((tm, d), lambda i: (t0 + i, 0)),
                  pl.BlockSpec((tm, 2), lambda i: (t0 + i, 0)),
                  pl.BlockSpec((1, 1, d), lambda i: ((row0 + i * tm) // s, 0, 0)),
                  pl.BlockSpec((1, d), lambda i: (0, 0))],
        out_specs=pl.BlockSpec((tm, d), lambda i: (t0 + i, 0)),
        input_output_aliases={2: 0},
        compiler_params=_params(("parallel",)),
        name="combine",
    )(y2, y2, x1, wts, gate2, gain)


def _layer(x, mod, bias, bfar, p):
    bsz, s, d = x.shape
    n_tok = bsz * s
    aw = ATT_HEADS * HEAD_DIM
    sw = p["ssm_d"].shape[0]

    w_in = p["w_in"]
    w_rest = jnp.concatenate([w_in[:, :3 * aw], w_in[:, 3 * aw + sw:]], axis=1).astype(BF16)
    w_ut = w_in[:, 3 * aw:3 * aw + sw].T.astype(BF16)
    q, k, v, u_t, sga, sgs = _inproj(x, mod, p["g_pre_mix"].reshape(1, d), w_rest, w_ut, tm=1024)
    att = _moba(q, k, v, bias, bfar)
    tables = _ssm_tables(p["ssm_lambda_re"], p["ssm_lambda_im"], p["ssm_log_dt"], p["ssm_b_re"],
                         p["ssm_b_im"], p["ssm_c_re"], p["ssm_c_im"], p["ssm_d"])
    y_t = _ssm(u_t, *tables)

    w_rt = jnp.concatenate([p["w_router_expert"].T, p["w_router_group"].T,
                            jnp.zeros((ROUTER_ROWS - N_EXPERTS - N_GROUPS, d), F32)], axis=0)
    b_rt = jnp.concatenate([p["b_router_expert"], p["b_router_group"],
                            jnp.zeros((ROUTER_ROWS - N_EXPERTS - N_GROUPS,), F32)]).reshape(ROUTER_ROWS, 1)
    x1, h2, logits_t = _merge(
        x, att, y_t, sga, sgs, mod, p["g_post_mix"].reshape(1, d), p["g_pre_ffn"].reshape(1, d),
        p["w_att_out"].astype(BF16), p["w_glu_val"].astype(BF16), p["w_glu_gate"].astype(BF16),
        p["w_mix_out"].astype(BF16), w_rt, b_rt, tm=1024)

    eid, wts, dest, counts = _route(logits_t, tn=1024)
    counts = counts[:, 0]
    padded = (counts + MOE_BLOCK - 1) // MOE_BLOCK * MOE_BLOCK
    pend = jnp.cumsum(padded)
    n_blocks = -(-(n_tok * 2) // MOE_BLOCK) + N_EXPERTS
    block_start = jnp.arange(n_blocks, dtype=I32) * MOE_BLOCK
    block_e = jnp.minimum(jnp.sum((pend[None, :] <= block_start[:, None]).astype(I32), axis=1), N_EXPERTS - 1)
    block_valid = jnp.clip(counts[block_e] - (block_start - (pend - padded)[block_e]), 0, MOE_BLOCK)
    x_rows = _sc_dispatch(h2.reshape(n_tok, d // 2), dest[0], dest[1], n_blocks * MOE_BLOCK)
    y_rows = _experts(x_rows, block_e, block_valid, p["w_exp_gate"].astype(BF16), p["w_exp_up"].astype(BF16),
                      p["w_exp_down"].astype(BF16))
    out = x1.reshape(n_tok, d)
    chunk = n_tok // COMBINE_CHUNKS
    for c in range(COMBINE_CHUNKS):
        y2 = _sc_gather(y_rows, dest[:, c * chunk:(c + 1) * chunk].reshape(-1))
        out = _combine(y2, wts.T, out, mod[:, 5:6, :], p["g_post_ffn"].reshape(1, d), tm=512, row0=c * chunk)
    return out.reshape(bsz, s, d)


def kernel(x, c, rel_bias, w_ada, b_ada, g_pre_mix, g_post_mix, w_in, w_att_out, ssm_lambda_re, ssm_lambda_im, ssm_log_dt, ssm_b_re, ssm_b_im, ssm_c_re, ssm_c_im, ssm_d, w_glu_val, w_glu_gate, w_mix_out, g_pre_ffn, g_post_ffn, w_router_group, b_router_group, w_router_expert, b_router_expert, w_exp_gate, w_exp_up, w_exp_down):
    layered = dict(
        w_ada=w_ada, b_ada=b_ada, g_pre_mix=g_pre_mix, g_post_mix=g_post_mix, w_in=w_in,
        w_att_out=w_att_out, ssm_lambda_re=ssm_lambda_re, ssm_lambda_im=ssm_lambda_im,
        ssm_log_dt=ssm_log_dt, ssm_b_re=ssm_b_re, ssm_b_im=ssm_b_im, ssm_c_re=ssm_c_re,
        ssm_c_im=ssm_c_im, ssm_d=ssm_d, w_glu_val=w_glu_val, w_glu_gate=w_glu_gate,
        w_mix_out=w_mix_out, g_pre_ffn=g_pre_ffn, g_post_ffn=g_post_ffn,
        w_router_group=w_router_group, b_router_group=b_router_group,
        w_router_expert=w_router_expert, b_router_expert=b_router_expert,
        w_exp_gate=w_exp_gate, w_exp_up=w_exp_up, w_exp_down=w_exp_down)
    depth = w_ada.shape[0]
    bsz, d = c.shape
    bias = _bias_tiles(rel_bias)
    far_bucket = np.unique(_t5_bucket_np(np.arange(MOBA_BLOCK + 1, max(x.shape[1], MOBA_BLOCK + 2))))
    assert far_bucket.size == 1
    bfar = rel_bias[int(far_bucket[0])] * LOG2E
    for l in range(depth):
        p = {name: a[l] for name, a in layered.items()}
        mod = _ada(c, p["w_ada"], p["b_ada"]).reshape(bsz, -1, d)
        x = _layer(x, mod, bias, bfar, p)
    return x
```
